```python
import jax, jax.numpy as jnp
from jax import lax
import numpy as np

D_MODEL = 2048
BATCH = 4
SEQ = 4096
DEPTH = 1
DEC_BATCH = 8
DEC_SEQ = 16
PAST_LEN = 2048

CHUNK = 64
D_MIX = D_MODEL
D_A = D_MIX // 2
D_B = D_MIX - D_A
HEAD_DIM = 128
H_A = D_A // HEAD_DIM
GMLP_CHUNK = 128
G_B = 8
D_GB = D_B // G_B
Q_BLOCK = 128
RMS_EPS = 1e-6
LN_EPS = 1e-5
FORGET_BIAS = 3.0
OFF_Q = 0
OFF_K = OFF_Q + D_A
OFF_V = OFF_K + D_A
OFF_F = OFF_V + D_A
OFF_GA = OFF_F + H_A
OFF_U = OFF_GA + D_A
OFF_VB = OFF_U + D_B
OFF_GB = OFF_VB + D_B
D_IN = OFF_GB + D_B

kernel_name = "hybrid_fox_gmlp_streaming_step"


def rmsnorm(x, g):
    xf = x.astype(jnp.float32)
    y = xf * lax.rsqrt(jnp.mean(xf * xf, axis=-1, keepdims=True) + RMS_EPS)
    return (y * g.astype(jnp.float32)).astype(x.dtype)


def layernorm(x, g, b):
    xf = x.astype(jnp.float32)
    mu = jnp.mean(xf, axis=-1, keepdims=True)
    var = jnp.mean(jnp.square(xf - mu), axis=-1, keepdims=True)
    y = (xf - mu) * lax.rsqrt(var + LN_EPS)
    return (y * g.astype(jnp.float32) + b.astype(jnp.float32)).astype(x.dtype)


def fox_prompt(q, k, v, logf):
    B, S, H, Dh = q.shape
    scale = HEAD_DIM ** -0.5
    cum = jnp.cumsum(logf, axis=1)
    cum_k = cum.transpose(0, 2, 1)
    nblk = S // Q_BLOCK
    qb = q.reshape(B, nblk, Q_BLOCK, H, Dh).transpose(1, 0, 2, 3, 4)
    cb = cum.reshape(B, nblk, Q_BLOCK, H).transpose(1, 0, 3, 2)
    pos_k = jnp.arange(S)

    def block(args):
        i, qi, ci = args
        s = jnp.einsum('bqhd,bkhd->bhqk', qi, k, preferred_element_type=jnp.float32) * scale
        s = s + ci[..., :, None] - cum_k[:, :, None, :]
        pos_q = i * Q_BLOCK + jnp.arange(Q_BLOCK)
        s = jnp.where(pos_k[None, :] <= pos_q[:, None], s, -jnp.inf)
        p = jax.nn.softmax(s, axis=-1)
        return jnp.einsum('bhqk,bkhd->bqhd', p.astype(v.dtype), v)

    out = lax.map(block, (jnp.arange(nblk), qb, cb))
    return out.transpose(1, 0, 2, 3, 4).reshape(B, S, H, Dh)


def fox_sample(q, k_new, v_new, logf_new, cache_k, cache_v, cache_logf):
    P = cache_k.shape[1]
    T = q.shape[1]
    scale = HEAD_DIM ** -0.5
    k_all = jnp.concatenate([cache_k.astype(k_new.dtype), k_new], axis=1)
    v_all = jnp.concatenate([cache_v.astype(v_new.dtype), v_new], axis=1)
    logf_all = jnp.concatenate([cache_logf.astype(jnp.float32), logf_new], axis=1)
    cum = jnp.cumsum(logf_all, axis=1).transpose(0, 2, 1)
    s = jnp.einsum('bqhd,bkhd->bhqk', q, k_all, preferred_element_type=jnp.float32) * scale
    s = s + cum[:, :, P:, None] - cum[:, :, None, :]
    pos_q = P + jnp.arange(T)
    pos_k = jnp.arange(P + T)
    s = jnp.where(pos_k[None, :] <= pos_q[:, None], s, -jnp.inf)
    p = jax.nn.softmax(s, axis=-1)
    return jnp.einsum('bhqk,bkhd->bqhd', p.astype(v_all.dtype), v_all)


def gmlp_mix(u, vn, w_s, b_s):
    B, S, _ = u.shape
    C = min(S, GMLP_CHUNK)
    n = S // C
    tri = jnp.tril(jnp.ones((C, C), dtype=w_s.dtype))
    w = w_s[:, :C, :C] * tri
    vc = vn.reshape(B, n, C, G_B, D_GB)
    mixed = jnp.einsum('gts,bnsgc->bntgc', w, vc) + b_s[:, :C].T[:, :, None]
    return u * mixed.reshape(B, S, D_B)


def mixer_layer(x, norm_g, w_in, b_f, ln_g, ln_b, w_s, b_s, w_out, caches=None):
    B, S, _ = x.shape
    h = rmsnorm(x, norm_g)
    z = h @ w_in
    q = z[..., OFF_Q:OFF_K].reshape(B, S, H_A, HEAD_DIM)
    k = z[..., OFF_K:OFF_V].reshape(B, S, H_A, HEAD_DIM)
    v = z[..., OFF_V:OFF_F].reshape(B, S, H_A, HEAD_DIM)
    logf = jax.nn.log_sigmoid(z[..., OFF_F:OFF_GA].astype(jnp.float32) + b_f.astype(jnp.float32))
    g_a = z[..., OFF_GA:OFF_U]
    u = jax.nn.gelu(z[..., OFF_U:OFF_VB])
    v_b = layernorm(jax.nn.gelu(z[..., OFF_VB:OFF_GB]), ln_g, ln_b)
    g_b = z[..., OFF_GB:D_IN]
    if caches is None:
        att = fox_prompt(q, k, v, logf)
    else:
        att = fox_sample(q, k, v, logf, *caches)
    out_a = att.reshape(B, S, D_A) * jax.nn.silu(g_a)
    out_b = gmlp_mix(u, v_b, w_s, b_s) * jax.nn.silu(g_b)
    y = x + jnp.concatenate([out_a, out_b], axis=-1) @ w_out
    return y, k, v, logf, v_b


def setup_inputs(seed: int = 0) -> dict:
    key = jax.random.key(seed)
    ks = jax.random.split(key, 16)
    f32 = jnp.float32
    x_prompt = jax.random.normal(ks[0], (BATCH, SEQ, D_MODEL), f32)
    x_sample = jax.random.normal(ks[1], (DEC_BATCH, DEC_SEQ, D_MODEL), f32)
    cache_k = jax.random.normal(ks[2], (DEPTH, DEC_BATCH, PAST_LEN, H_A, HEAD_DIM), f32)
    cache_v = jax.random.normal(ks[3], (DEPTH, DEC_BATCH, PAST_LEN, H_A, HEAD_DIM), f32)
    cache_logf = jax.nn.log_sigmoid(FORGET_BIAS + 0.5 * jax.random.normal(ks[4], (DEPTH, DEC_BATCH, PAST_LEN, H_A), f32))
    norm_g = 1.0 + 0.05 * jax.random.normal(ks[5], (DEPTH, D_MODEL), f32)
    w_in = jax.random.normal(ks[6], (DEPTH, D_MODEL, D_IN), f32) * D_MODEL ** -0.5
    b_f = FORGET_BIAS + 0.5 * jax.random.normal(ks[7], (DEPTH, H_A), f32)
    ln_g = 1.0 + 0.05 * jax.random.normal(ks[8], (DEPTH, D_B), f32)
    ln_b = 0.02 * jax.random.normal(ks[9], (DEPTH, D_B), f32)
    w_s = jax.random.normal(ks[10], (DEPTH, G_B, GMLP_CHUNK, GMLP_CHUNK), f32) * GMLP_CHUNK ** -0.5
    b_s = 1.0 + 0.1 * jax.random.normal(ks[11], (DEPTH, G_B, GMLP_CHUNK), f32)
    w_out = jax.random.normal(ks[12], (DEPTH, D_MIX, D_MODEL), f32) * D_MIX ** -0.5
    final_g = 1.0 + 0.05 * jax.random.normal(ks[13], (D_MODEL,), f32)
    return {"x_prompt": x_prompt, "x_sample": x_sample, "cache_k": cache_k, "cache_v": cache_v,
            "cache_logf": cache_logf, "norm_g": norm_g, "w_in": w_in, "b_f": b_f, "ln_g": ln_g,
            "ln_b": ln_b, "w_s": w_s, "b_s": b_s, "w_out": w_out, "final_g": final_g}


def reference(x_prompt, x_sample, cache_k, cache_v, cache_logf, norm_g, w_in, b_f, ln_g, ln_b,
              w_s, b_s, w_out, final_g):
    hp, hs = x_prompt, x_sample
    kp, vp, fp, ksm, vsm, fsm, gsm = [], [], [], [], [], [], []
    for l in range(DEPTH):
        params = (norm_g[l], w_in[l], b_f[l], ln_g[l], ln_b[l], w_s[l], b_s[l], w_out[l])
        hp, k1, v1, f1, _ = mixer_layer(hp, *params)
        hs, k2, v2, f2, g2 = mixer_layer(hs, *params, caches=(cache_k[l], cache_v[l], cache_logf[l]))
        kp.append(k1); vp.append(v1); fp.append(f1)
        ksm.append(k2); vsm.append(v2); fsm.append(f2); gsm.append(g2)
    y_prompt = rmsnorm(hp, final_g)
    y_sample = rmsnorm(hs, final_g)
    return (y_prompt, y_sample, jnp.stack(kp), jnp.stack(vp), jnp.stack(fp),
            jnp.stack(ksm), jnp.stack(vsm), jnp.stack(fsm), jnp.stack(gsm))
```

```python
import functools
import math

import jax
import jax.numpy as jnp
from jax import lax
from jax.experimental import pallas as pl
from jax.experimental.pallas import tpu as pltpu

F32 = jnp.float32
BF16 = jnp.bfloat16

HEAD_DIM = 128
GMLP_CHUNK = 128
RMS_EPS = 1e-6
LN_EPS = 1e-5
LOG2E = math.log2(math.e)
LANES = 128
VMEM_LIMIT_BYTES = 48 * 1024 * 1024

_NT = (((1,), (1,)), ((), ()))


def _compiler_params(semantics):
    return pltpu.CompilerParams(dimension_semantics=semantics,
                                vmem_limit_bytes=VMEM_LIMIT_BYTES)


def _rmsnorm_rows(x, g):
    return x * lax.rsqrt(jnp.mean(x * x, axis=-1, keepdims=True) + RMS_EPS) * g


def _gelu_tanh(x):
    c = math.sqrt(2.0 / math.pi)
    return 0.5 * x * (1.0 + jnp.tanh(c * (x + 0.044715 * (x * x * x))))


def _silu(x):
    return x * jax.nn.sigmoid(x)


def _proj_attn_kernel(x_ref, g_ref, w_ref, wft_ref, bf_ref,
                      q_ref, k_ref, v_ref, sga_ref, logft_ref, h_ref, *, q_scale, n_heads):
    j = pl.program_id(1)

    @pl.when(j == 0)
    def _():
        hb = _rmsnorm_rows(x_ref[...], g_ref[...]).astype(BF16)
        h_ref[...] = hb
        zf = lax.dot_general(wft_ref[...], hb, _NT, preferred_element_type=F32)
        t = zf[:n_heads] + bf_ref[...]
        logft_ref[...] = jnp.minimum(t, 0.0) - jnp.log1p(jnp.exp(-jnp.abs(t)))

    z = jnp.dot(h_ref[...], w_ref[...], preferred_element_type=F32)

    @pl.when(j == 0)
    def _():
        q_ref[...] = (z * q_scale).astype(BF16)

    @pl.when(j == 1)
    def _():
        k_ref[...] = z

    @pl.when(j == 2)
    def _():
        v_ref[...] = z

    @pl.when(j == 3)
    def _():
        sga_ref[...] = _silu(z).astype(BF16)


def _proj_attn(x2, norm_g, w_a, w_ft, b_f, *, tm):
    n, d = x2.shape
    d_a = w_a.shape[1] // 4
    n_heads = b_f.shape[0]
    q_scale = HEAD_DIM ** -0.5 * LOG2E
    row = lambda i, j: (i, 0)
    return pl.pallas_call(
        functools.partial(_proj_attn_kernel, q_scale=q_scale, n_heads=n_heads),
        grid=(n // tm, 4),
        in_specs=[
            pl.BlockSpec((tm, d), row),
            pl.BlockSpec((1, d), lambda i, j: (0, 0)),
            pl.BlockSpec((d, d_a), lambda i, j: (0, j)),
            pl.BlockSpec(w_ft.shape, lambda i, j: (0, 0)),
            pl.BlockSpec((n_heads, 1), lambda i, j: (0, 0)),
        ],
        out_specs=[
            pl.BlockSpec((tm, d_a), row),
            pl.BlockSpec((tm, d_a), row),
            pl.BlockSpec((tm, d_a), row),
            pl.BlockSpec((tm, d_a), row),
            pl.BlockSpec((n_heads, tm), lambda i, j: (0, i)),
        ],
        out_shape=[
            jax.ShapeDtypeStruct((n, d_a), BF16),
            jax.ShapeDtypeStruct((n, d_a), F32),
            jax.ShapeDtypeStruct((n, d_a), F32),
            jax.ShapeDtypeStruct((n, d_a), BF16),
            jax.ShapeDtypeStruct((n_heads, n), F32),
        ],
        scratch_shapes=[pltpu.VMEM((tm, d), BF16)],
        compiler_params=_compiler_params(("parallel", "arbitrary")),
        name="proj_attn",
    )(x2, norm_g, w_a, w_ft, b_f)


def _proj_gmlp_kernel(x_ref, g_ref, w_ref, lng_ref, lnb_ref, wmix_ref, bmix_ref, *rest,
                      n_groups, emit_vn):
    if emit_vn:
        ob_ref, vn_ref, h_ref, gu_ref, mix_ref = rest
    else:
        ob_ref, h_ref, gu_ref, mix_ref = rest
        vn_ref = None
    j = pl.program_id(1)
    tm = x_ref.shape[0]
    n_chunks = tm // GMLP_CHUNK
    d_g = w_ref.shape[1] // n_groups

    @pl.when(j == 0)
    def _():
        h_ref[...] = _rmsnorm_rows(x_ref[...], g_ref[...]).astype(BF16)

    z = jnp.dot(h_ref[...], w_ref[...], preferred_element_type=F32)

    @pl.when(j == 0)
    def _():
        gu_ref[...] = _gelu_tanh(z)

    @pl.when(j == 1)
    def _():
        a = _gelu_tanh(z)
        mu = jnp.mean(a, axis=-1, keepdims=True)
        ac = a - mu
        var = jnp.mean(ac * ac, axis=-1, keepdims=True)
        vn = ac * lax.rsqrt(var + LN_EPS) * lng_ref[...] + lnb_ref[...]
        if emit_vn:
            vn_ref[...] = vn
        vb = vn.astype(BF16)
        for g in range(n_groups):
            cols = slice(g * d_g, (g + 1) * d_g)
            rhs = jnp.concatenate(
                [vb[r * GMLP_CHUNK:(r + 1) * GMLP_CHUNK, cols] for r in range(n_chunks)], axis=1)
            mixed = jnp.dot(wmix_ref[g], rhs, preferred_element_type=F32)
            for r in range(n_chunks):
                mix_ref[r * GMLP_CHUNK:(r + 1) * GMLP_CHUNK, cols] = (
                    mixed[:, r * d_g:(r + 1) * d_g] + bmix_ref[g])

    @pl.when(j == 2)
    def _():
        ob_ref[...] = (gu_ref[...] * mix_ref[...] * _silu(z)).astype(BF16)


def _proj_gmlp(x2, norm_g, w_b, ln_g, ln_b, w_mix, b_mix, *, tm, emit_vn):
    n, d = x2.shape
    d_b = w_b.shape[1] // 3
    n_groups = w_mix.shape[0]
    row = lambda i, j: (i, 0)
    const2 = lambda i, j: (0, 0)
    const3 = lambda i, j: (0, 0, 0)
    out_specs = [pl.BlockSpec((tm, d_b), row)]
    out_shape = [jax.ShapeDtypeStruct((n, d_b), BF16)]
    if emit_vn:
        out_specs.append(pl.BlockSpec((tm, d_b), row))
        out_shape.append(jax.ShapeDtypeStruct((n, d_b), F32))
    return pl.pallas_call(
        functools.partial(_proj_gmlp_kernel, n_groups=n_groups, emit_vn=emit_vn),
        grid=(n // tm, 3),
        in_specs=[
            pl.BlockSpec((tm, d), row),
            pl.BlockSpec((1, d), const2),
            pl.BlockSpec((d, d_b), lambda i, j: (0, j)),
            pl.BlockSpec((1, d_b), const2),
            pl.BlockSpec((1, d_b), const2),
            pl.BlockSpec(w_mix.shape, const3),
            pl.BlockSpec(b_mix.shape, const3),
        ],
        out_specs=out_specs,
        out_shape=out_shape,
        scratch_shapes=[pltpu.VMEM((tm, d), BF16),
                        pltpu.VMEM((tm, d_b), F32),
                        pltpu.VMEM((tm, d_b), F32)],
        compiler_params=_compiler_params(("parallel", "arbitrary")),
        name="proj_gmlp",
    )(x2, norm_g, w_b, ln_g, ln_b, w_mix, b_mix)


def _cumsum_kernel(x_ref, o_ref, *, group):
    x = x_ref[...]
    lane = lax.broadcasted_iota(jnp.int32, x.shape, 1)
    sh = 1
    while sh < LANES:
        x = x + jnp.where(lane >= sh, pltpu.roll(x, sh, axis=1), 0.0)
        sh *= 2
    sub = lax.broadcasted_iota(jnp.int32, x.shape, 0) % group
    tot = jnp.broadcast_to(x[:, LANES - 1:LANES], x.shape)
    carry = jnp.where(sub >= 1, pltpu.roll(tot, 1, axis=0), 0.0)
    sh = 1
    while sh < group:
        carry = carry + jnp.where(sub >= sh, pltpu.roll(carry, sh, axis=0), 0.0)
        sh *= 2
    o_ref[...] = x + carry


def _cumsum_rows(x2, group):
    return pl.pallas_call(
        functools.partial(_cumsum_kernel, group=group),
        out_shape=jax.ShapeDtypeStruct(x2.shape, F32),
        name="cumsum_rows",
    )(x2)


def _online_softmax_step(s, v_blk, m, l, acc):
    m_new = jnp.maximum(m, jnp.max(s, axis=1, keepdims=True))
    alpha = jnp.exp2(m - m_new)
    p = jnp.exp2(s - m_new)
    l = alpha * l + jnp.sum(p, axis=1, keepdims=True)
    acc = alpha * acc + jnp.dot(p.astype(BF16), v_blk, preferred_element_type=F32)
    return m_new, l, acc


def _attn_prompt_kernel(q_ref, k_ref, v_ref, c_ref, sga_ref, o_ref, kb_ref, vb_ref, nc_ref, *, blk):
    qi = pl.program_id(2)

    @pl.when(qi == 0)
    def _():
        kb_ref[...] = k_ref[...].astype(BF16)
        vb_ref[...] = v_ref[...].astype(BF16)
        nc_ref[...] = c_ref[...] * (-LOG2E)

    q = q_ref[...]

    def scores(j):
        off = pl.multiple_of(j * blk, blk)
        k_blk = kb_ref[pl.ds(off, blk), :]
        s = lax.dot_general(q, k_blk, _NT, preferred_element_type=F32)
        return s + nc_ref[j], vb_ref[pl.ds(off, blk), :]

    def body(j, carry):
        s, v_blk = scores(j)
        return _online_softmax_step(s, v_blk, *carry)

    init = (jnp.full((blk, 1), -jnp.inf, F32), jnp.zeros((blk, 1), F32),
            jnp.zeros((blk, HEAD_DIM), F32))
    carry = lax.fori_loop(0, qi, body, init)

    s, v_blk = scores(qi)
    rows = lax.broadcasted_iota(jnp.int32, s.shape, 0)
    cols = lax.broadcasted_iota(jnp.int32, s.shape, 1)
    s = jnp.where(cols <= rows, s, -jnp.inf)
    _, l, acc = _online_softmax_step(s, v_blk, *carry)
    o_ref[...] = (acc / l * sga_ref[...].astype(F32)).astype(BF16)


def _attn_prompt(q, k, v, cum, sga, *, batch, seq, blk):
    n, d_a = q.shape
    n_heads = d_a // HEAD_DIM
    nq = seq // blk
    qmap = lambda b, h, i: (b * nq + i, h)
    kvmap = lambda b, h, i: (b, h)
    return pl.pallas_call(
        functools.partial(_attn_prompt_kernel, blk=blk),
        grid=(batch, n_heads, nq),
        in_specs=[
            pl.BlockSpec((blk, HEAD_DIM), qmap),
            pl.BlockSpec((seq, HEAD_DIM), kvmap),
            pl.BlockSpec((seq, HEAD_DIM), kvmap),
            pl.BlockSpec((None, None, nq, 1, blk), lambda b, h, i: (h, b, 0, 0, 0)),
            pl.BlockSpec((blk, HEAD_DIM), qmap),
        ],
        out_specs=pl.BlockSpec((blk, HEAD_DIM), qmap),
        out_shape=jax.ShapeDtypeStruct((n, d_a), BF16),
        scratch_shapes=[pltpu.VMEM((seq, HEAD_DIM), BF16),
                        pltpu.VMEM((seq, HEAD_DIM), BF16),
                        pltpu.VMEM((nq, 1, blk), F32)],
        compiler_params=_compiler_params(("parallel", "parallel", "arbitrary")),
        name="attn_prompt",
    )(q, k, v, cum, sga)


def _attn_sample_kernel(q_ref, kn_ref, vn_ref, ck_ref, cv_ref, cp_ref, cn_ref, sga_ref, o_ref,
                        m_ref, l_ref, acc_ref, *, n_heads):
    c = pl.program_id(1)

    @pl.when(c == 0)
    def _():
        m_ref[...] = jnp.full(m_ref.shape, -jnp.inf, F32)
        l_ref[...] = jnp.zeros(l_ref.shape, F32)
        acc_ref[...] = jnp.zeros(acc_ref.shape, F32)

    def head_cols(h):
        return slice(h * HEAD_DIM, (h + 1) * HEAD_DIM)

    for h in range(n_heads):
        q = q_ref[:, head_cols(h)]
        s = lax.dot_general(q, ck_ref[:, head_cols(h)].astype(BF16), _NT,
                            preferred_element_type=F32)
        s = s + cp_ref[h:h + 1, :] * (-LOG2E)
        m, l, acc = _online_softmax_step(s, cv_ref[:, head_cols(h)].astype(BF16),
                                         m_ref[h], l_ref[h], acc_ref[h])
        m_ref[h] = m
        l_ref[h] = l
        acc_ref[h] = acc

    @pl.when(c == pl.num_programs(1) - 1)
    def _():
        for h in range(n_heads):
            q = q_ref[:, head_cols(h)]
            s = lax.dot_general(q, kn_ref[:, head_cols(h)].astype(BF16), _NT,
                                preferred_element_type=F32)
            s = s + cn_ref[h:h + 1, :] * (-LOG2E)
            rows = lax.broadcasted_iota(jnp.int32, s.shape, 0)
            cols = lax.broadcasted_iota(jnp.int32, s.shape, 1)
            s = jnp.where(cols <= rows, s, -jnp.inf)
            _, l, acc = _online_softmax_step(s, vn_ref[:, head_cols(h)].astype(BF16),
                                             m_ref[h], l_ref[h], acc_ref[h])
            o_ref[:, head_cols(h)] = (
                acc / l * sga_ref[:, head_cols(h)].astype(F32)).astype(BF16)


def _attn_sample(q, k_new, v_new, cache_k2, cache_v2, cum_past, cum_new, sga, *,
                 batch, t_new, past, chunk):
    n, d_a = q.shape
    n_heads = d_a // HEAD_DIM
    n_chunks = past // chunk
    new_map = lambda b, c: (b, 0)
    cache_map = lambda b, c: (b * n_chunks + c, 0)
    return pl.pallas_call(
        functools.partial(_attn_sample_kernel, n_heads=n_heads),
        grid=(batch, n_chunks),
        in_specs=[
            pl.BlockSpec((t_new, d_a), new_map),
            pl.BlockSpec((t_new, d_a), new_map),
            pl.BlockSpec((t_new, d_a), new_map),
            pl.BlockSpec((chunk, d_a), cache_map),
            pl.BlockSpec((chunk, d_a), cache_map),
            pl.BlockSpec((None, None, n_heads, chunk), lambda b, c: (b, c, 0, 0)),
            pl.BlockSpec((None, n_heads, t_new), lambda b, c: (b, 0, 0)),
            pl.BlockSpec((t_new, d_a), new_map),
        ],
        out_specs=pl.BlockSpec((t_new, d_a), new_map),
        out_shape=jax.ShapeDtypeStruct((n, d_a), BF16),
        scratch_shapes=[pltpu.VMEM((n_heads, t_new, 1), F32),
                        pltpu.VMEM((n_heads, t_new, 1), F32),
                        pltpu.VMEM((n_heads, t_new, HEAD_DIM), F32)],
        compiler_params=_compiler_params(("parallel", "arbitrary")),
        name="attn_sample",
    )(q, k_new, v_new, cache_k2, cache_v2, cum_past, cum_new, sga)


def _out_proj_kernel(oa_ref, ob_ref, w_ref, x_ref, fg_ref, y_ref, *, final_norm):
    o = jnp.concatenate([oa_ref[...], ob_ref[...]], axis=1)
    y = x_ref[...] + jnp.dot(o, w_ref[...], preferred_element_type=F32)
    if final_norm:
        y = _rmsnorm_rows(y, fg_ref[...])
    y_ref[...] = y


def _out_proj(out_a, out_b, w_out, x2, final_g, *, tm, final_norm):
    n, d = x2.shape
    d_a = out_a.shape[1]
    d_b = out_b.shape[1]
    row = lambda i: (i, 0)
    const = lambda i: (0, 0)
    return pl.pallas_call(
        functools.partial(_out_proj_kernel, final_norm=final_norm),
        grid=(n // tm,),
        in_specs=[
            pl.BlockSpec((tm, d_a), row),
            pl.BlockSpec((tm, d_b), row),
            pl.BlockSpec(w_out.shape, const),
            pl.BlockSpec((tm, d), row),
            pl.BlockSpec((1, d), const),
        ],
        out_specs=pl.BlockSpec((tm, d), row),
        out_shape=jax.ShapeDtypeStruct((n, d), F32),
        compiler_params=_compiler_params(("parallel",)),
        name="out_proj",
    )(out_a, out_b, w_out, x2, final_g)


def _pad_rows(a, rows):
    return jnp.pad(a, ((0, rows - a.shape[0]), (0, 0)))


def _mixing_weights(w_s, b_s, seq):
    c = min(seq, GMLP_CHUNK)
    reps = GMLP_CHUNK // c
    w = w_s[:, :c, :c] * jnp.tril(jnp.ones((c, c), w_s.dtype))
    if reps > 1:
        w = jnp.einsum('rs,gab->grasb', jnp.eye(reps, dtype=w.dtype), w)
        w = w.reshape(w_s.shape[0], GMLP_CHUNK, GMLP_CHUNK)
    b = jnp.tile(b_s[:, :c], (1, reps))
    b_full = jnp.broadcast_to(b[:, :, None], (b.shape[0], GMLP_CHUNK, LANES))
    return w.astype(BF16), b_full.astype(F32)


def _layer(x, params, caches, final_g, *, final_norm, tm):
    norm_g, w_in, b_f, ln_g, ln_b, w_s, b_s, w_out = params
    batch, seq, d = x.shape
    n = batch * seq
    n_heads = b_f.shape[0]
    d_a = n_heads * HEAD_DIM
    d_b = ln_g.shape[0]
    off_f = 3 * d_a
    off_ga = off_f + n_heads
    off_u = off_ga + d_a

    x2 = x.reshape(n, d)
    w_a = jnp.concatenate([w_in[:, :off_f], w_in[:, off_ga:off_u]], axis=1).astype(BF16)
    w_ft = _pad_rows(w_in[:, off_f:off_ga].T, 16).astype(BF16)
    w_b = w_in[:, off_u:].astype(BF16)
    w_mix, b_mix = _mixing_weights(w_s, b_s, seq)

    q, k, v, sga, logft = _proj_attn(x2, norm_g[None], w_a, w_ft, b_f[:, None], tm=tm)
    gm = _proj_gmlp(x2, norm_g[None], w_b, ln_g[None], ln_b[None], w_mix, b_mix,
                    tm=tm, emit_vn=caches is not None)

    logf = logft.T.reshape(batch, seq, n_heads)
    if caches is None:
        out_b = gm[0]
        vn = None
        blk = 512
        group = seq // LANES
        cum = _cumsum_rows(logft.reshape(n_heads * batch * group, LANES), group)
        cum = cum.reshape(n_heads, batch, seq // blk, 1, blk)
        out_a = _attn_prompt(q, k, v, cum, sga, batch=batch, seq=seq, blk=blk)
    else:
        out_b, vn = gm
        cache_k, cache_v, cache_logf = caches
        past = cache_k.shape[1]
        chunk = 1024
        total = past + seq
        group = -(-total // LANES)
        lf_all = jnp.concatenate(
            [cache_logf.astype(F32).transpose(0, 2, 1), logf.transpose(0, 2, 1)], axis=2)
        lf_all = jnp.pad(lf_all, ((0, 0), (0, 0), (0, group * LANES - total)))
        cum = _cumsum_rows(lf_all.reshape(batch * n_heads * group, LANES), group)
        cum = cum.reshape(batch, n_heads, group * LANES)
        cum_past = cum[:, :, :past].reshape(batch, n_heads, past // chunk, chunk)
        cum_past = cum_past.transpose(0, 2, 1, 3)
        cum_new = cum[:, :, past:total]
        out_a = _attn_sample(q, k, v, cache_k.reshape(batch * past, d_a),
                             cache_v.reshape(batch * past, d_a), cum_past, cum_new, sga,
                             batch=batch, t_new=seq, past=past, chunk=chunk)

    y2 = _out_proj(out_a, out_b, w_out.astype(BF16), x2, final_g[None], tm=tm,
                   final_norm=final_norm)
    y = y2.reshape(batch, seq, d)
    k4 = k.reshape(batch, seq, n_heads, HEAD_DIM)
    v4 = v.reshape(batch, seq, n_heads, HEAD_DIM)
    vn3 = None if vn is None else vn.reshape(batch, seq, d_b)
    return y, k4, v4, logf, vn3


def kernel(x_prompt, x_sample, cache_k, cache_v, cache_logf, norm_g, w_in, b_f, ln_g, ln_b,
           w_s, b_s, w_out, final_g):
    depth = norm_g.shape[0]
    hp, hs = x_prompt, x_sample
    kp, vp, fp, ksm, vsm, fsm, gsm = [], [], [], [], [], [], []
    n_sample = x_sample.shape[0] * x_sample.shape[1]
    for l in range(depth):
        params = (norm_g[l], w_in[l], b_f[l], ln_g[l], ln_b[l], w_s[l], b_s[l], w_out[l])
        last = l == depth - 1
        hp, k1, v1, f1, _ = _layer(hp, params, None, final_g, final_norm=last, tm=512)
        hs, k2, v2, f2, g2 = _layer(hs, params, (cache_k[l], cache_v[l], cache_logf[l]), final_g,
                                    final_norm=last, tm=n_sample)
        kp.append(k1); vp.append(v1); fp.append(f1)
        ksm.append(k2); vsm.append(v2); fsm.append(f2); gsm.append(g2)
    return (hp, hs, jnp.stack(kp), jnp.stack(vp), jnp.stack(fp),
            jnp.stack(ksm), jnp.stack(vsm), jnp.stack(fsm), jnp.stack(gsm))
```

```python
import functools
import math

import jax
import jax.numpy as jnp
from jax import lax
from jax.experimental import pallas as pl
from jax.experimental.pallas import tpu as pltpu

F32 = jnp.float32
BF16 = jnp.bfloat16

HEAD_DIM = 128
GMLP_CHUNK = 128
RMS_EPS = 1e-6
LN_EPS = 1e-5
LOG2E = math.log2(math.e)
LANES = 128
VMEM_LIMIT_BYTES = 48 * 1024 * 1024

_NT = (((1,), (1,)), ((), ()))


def _compiler_params(semantics):
    return pltpu.CompilerParams(dimension_semantics=semantics,
                                vmem_limit_bytes=VMEM_LIMIT_BYTES)


def _rmsnorm_rows(x, g):
    return x * lax.rsqrt(jnp.mean(x * x, axis=-1, keepdims=True) + RMS_EPS) * g


def _gelu_tanh(x):
    c = math.sqrt(2.0 / math.pi)
    return 0.5 * x * (1.0 + jnp.tanh(c * (x + 0.044715 * (x * x * x))))


def _silu(x):
    return x * jax.nn.sigmoid(x)


def _store_heads_on_sublanes(ref, z, n_heads):
    rows = z.shape[0]
    for h in range(n_heads):
        ref[pl.ds(h, rows, stride=n_heads), :] = z[:, h * HEAD_DIM:(h + 1) * HEAD_DIM]


def _proj_attn_kernel(x_ref, g_ref, wqkv_ref, wga_ref, wft_ref, bf_ref,
                      q_ref, k_ref, kb_ref, v_ref, vb_ref, sga_ref, logft_ref, h_ref,
                      *, q_scale, n_heads, v_sub):
    j = pl.program_id(1)

    @pl.when(j == 0)
    def _():
        hb = _rmsnorm_rows(x_ref[...], g_ref[...]).astype(BF16)
        h_ref[...] = hb
        zf = lax.dot_general(wft_ref[...], hb, _NT, preferred_element_type=F32)
        t = zf[:n_heads] + bf_ref[...]
        logft_ref[...] = jnp.minimum(t, 0.0) - jnp.log1p(jnp.exp(-jnp.abs(t)))

    def project(w_ref):
        return jnp.dot(h_ref[...], w_ref[...], preferred_element_type=F32)

    @pl.when(j == 0)
    def _():
        q_ref[...] = (project(wqkv_ref) * q_scale).astype(BF16)

    @pl.when(j == 1)
    def _():
        z = project(wqkv_ref)
        _store_heads_on_sublanes(k_ref, z, n_heads)
        kb_ref[...] = z.astype(BF16)

    @pl.when(j == 2)
    def _():
        z = project(wqkv_ref)
        _store_heads_on_sublanes(v_ref, z, n_heads)
        if v_sub is None:
            vb_ref[...] = z.astype(BF16)
        else:
            for c in range(z.shape[0] // v_sub):
                vb_ref[c] = z[c * v_sub:(c + 1) * v_sub, :].T.astype(BF16)

    @pl.when(j == 3)
    def _():
        sga_ref[...] = _silu(project(wga_ref)).astype(BF16)


def _proj_attn(x2, norm_g, w_all, w_rest, w_ft, b_f, *, tm, v_sub=None):
    n, d = x2.shape
    n_heads = b_f.shape[0]
    d_a = n_heads * HEAD_DIM
    q_scale = HEAD_DIM ** -0.5 * LOG2E
    row = lambda i, j: (i, 0)
    const = lambda i, j: (0, 0)
    if v_sub is None:
        vb_spec = pl.BlockSpec((tm, d_a), row)
        vb_shape = jax.ShapeDtypeStruct((n, d_a), BF16)
    else:
        vb_spec = pl.BlockSpec((tm // v_sub, d_a, v_sub), lambda i, j: (i, 0, 0))
        vb_shape = jax.ShapeDtypeStruct((n // v_sub, d_a, v_sub), BF16)
    return pl.pallas_call(
        functools.partial(_proj_attn_kernel, q_scale=q_scale, n_heads=n_heads, v_sub=v_sub),
        grid=(n // tm, 4),
        in_specs=[
            pl.BlockSpec((tm, d), row),
            pl.BlockSpec((1, d), const),
            pl.BlockSpec((d, d_a), lambda i, j: (0, jnp.minimum(j, 2))),
            pl.BlockSpec((d, d_a), const),
            pl.BlockSpec(w_ft.shape, const),
            pl.BlockSpec((n_heads, 1), const),
        ],
        out_specs=[
            pl.BlockSpec((tm, d_a), row),
            pl.BlockSpec((tm * n_heads, HEAD_DIM), row),
            pl.BlockSpec((tm, d_a), row),
            pl.BlockSpec((tm * n_heads, HEAD_DIM), row),
            vb_spec,
            pl.BlockSpec((tm, d_a), row),
            pl.BlockSpec((n_heads, tm), lambda i, j: (0, i)),
        ],
        out_shape=[
            jax.ShapeDtypeStruct((n, d_a), BF16),
            jax.ShapeDtypeStruct((n * n_heads, HEAD_DIM), F32),
            jax.ShapeDtypeStruct((n, d_a), BF16),
            jax.ShapeDtypeStruct((n * n_heads, HEAD_DIM), F32),
            vb_shape,
            jax.ShapeDtypeStruct((n, d_a), BF16),
            jax.ShapeDtypeStruct((n_heads, n), F32),
        ],
        scratch_shapes=[pltpu.VMEM((tm, d), BF16)],
        compiler_params=_compiler_params(("parallel", "arbitrary")),
        name="proj_attn",
    )(x2, norm_g, w_all, w_rest, w_ft, b_f)


def _proj_gmlp_kernel(x_ref, g_ref, w_ref, lng_ref, lnb_ref, wmix_ref, bmix_ref, *rest,
                      n_groups, emit_vn):
    if emit_vn:
        ob_ref, vn_ref, h_ref, gu_ref, mix_ref = rest
    else:
        ob_ref, h_ref, gu_ref, mix_ref = rest
        vn_ref = None
    j = pl.program_id(1)
    tm = x_ref.shape[0]
    n_chunks = tm // GMLP_CHUNK
    d_g = w_ref.shape[1] // n_groups

    @pl.when(j == 0)
    def _():
        h_ref[...] = _rmsnorm_rows(x_ref[...], g_ref[...]).astype(BF16)

    z = jnp.dot(h_ref[...], w_ref[...], preferred_element_type=F32)

    @pl.when(j == 0)
    def _():
        gu_ref[...] = _gelu_tanh(z)

    @pl.when(j == 1)
    def _():
        a = _gelu_tanh(z)
        mu = jnp.mean(a, axis=-1, keepdims=True)
        ac = a - mu
        var = jnp.mean(ac * ac, axis=-1, keepdims=True)
        vn = ac * lax.rsqrt(var + LN_EPS) * lng_ref[...] + lnb_ref[...]
        if emit_vn:
            vn_ref[...] = vn
        vb = vn.astype(BF16)
        for g in range(n_groups):
            cols = slice(g * d_g, (g + 1) * d_g)
            rhs = jnp.concatenate(
                [vb[r * GMLP_CHUNK:(r + 1) * GMLP_CHUNK, cols] for r in range(n_chunks)], axis=1)
            mixed = jnp.dot(wmix_ref[g], rhs, preferred_element_type=F32)
            for r in range(n_chunks):
                mix_ref[r * GMLP_CHUNK:(r + 1) * GMLP_CHUNK, cols] = (
                    mixed[:, r * d_g:(r + 1) * d_g] + bmix_ref[g])

    @pl.when(j == 2)
    def _():
        ob_ref[...] = (gu_ref[...] * mix_ref[...] * _silu(z)).astype(BF16)


def _proj_gmlp(x2, norm_g, w_rest, ln_g, ln_b, w_mix, b_mix, *, tm, emit_vn):
    n, d = x2.shape
    d_b = ln_g.shape[1]
    n_groups = w_mix.shape[0]
    row = lambda i, j: (i, 0)
    const2 = lambda i, j: (0, 0)
    const3 = lambda i, j: (0, 0, 0)
    out_specs = [pl.BlockSpec((tm, d_b), row)]
    out_shape = [jax.ShapeDtypeStruct((n, d_b), BF16)]
    if emit_vn:
        out_specs.append(pl.BlockSpec((tm, d_b), row))
        out_shape.append(jax.ShapeDtypeStruct((n, d_b), F32))
    return pl.pallas_call(
        functools.partial(_proj_gmlp_kernel, n_groups=n_groups, emit_vn=emit_vn),
        grid=(n // tm, 3),
        in_specs=[
            pl.BlockSpec((tm, d), row),
            pl.BlockSpec((1, d), const2),
            pl.BlockSpec((d, d_b), lambda i, j: (0, j + 1)),
            pl.BlockSpec((1, d_b), const2),
            pl.BlockSpec((1, d_b), const2),
            pl.BlockSpec(w_mix.shape, const3),
            pl.BlockSpec(b_mix.shape, const3),
        ],
        out_specs=out_specs,
        out_shape=out_shape,
        scratch_shapes=[pltpu.VMEM((tm, d), BF16),
                        pltpu.VMEM((tm, d_b), F32),
                        pltpu.VMEM((tm, d_b), F32)],
        compiler_params=_compiler_params(("parallel", "arbitrary")),
        name="proj_gmlp",
    )(x2, norm_g, w_rest, ln_g, ln_b, w_mix, b_mix)


def _cumsum_kernel(x_ref, o_ref, *, group):
    x = x_ref[...]
    lane = lax.broadcasted_iota(jnp.int32, x.shape, 1)
    sh = 1
    while sh < LANES:
        x = x + jnp.where(lane >= sh, pltpu.roll(x, sh, axis=1), 0.0)
        sh *= 2
    sub = lax.broadcasted_iota(jnp.int32, x.shape, 0) % group
    tot = jnp.broadcast_to(x[:, LANES - 1:LANES], x.shape)
    carry = jnp.where(sub >= 1, pltpu.roll(tot, 1, axis=0), 0.0)
    sh = 1
    while sh < group:
        carry = carry + jnp.where(sub >= sh, pltpu.roll(carry, sh, axis=0), 0.0)
        sh *= 2
    o_ref[...] = x + carry


def _cumsum_rows(x2, group):
    return pl.pallas_call(
        functools.partial(_cumsum_kernel, group=group),
        out_shape=jax.ShapeDtypeStruct(x2.shape, F32),
        name="cumsum_rows",
    )(x2)


def _online_softmax_step(s, v_blk, m, l, acc):
    m_new = jnp.maximum(m, jnp.max(s, axis=1, keepdims=True))
    alpha = jnp.exp2(m - m_new)
    p = jnp.exp2(s - m_new)
    l = alpha * l + jnp.sum(p, axis=1, keepdims=True)
    acc = alpha * acc + jnp.dot(p.astype(BF16), v_blk, preferred_element_type=F32)
    return m_new, l, acc


QUERY_GROUP = 256
N_BIAS_PARTS = 3
SUM_ROWS = 16


def _attn_prompt_kernel(q_ref, kb_ref, vt_ref, ct_ref, sga_ref, o_ref,
                        ka_ref, sa_ref, sb_ref, pa_ref, pb_ref, acc_ref, qa_ref):
    qi = pl.program_id(2)
    kv_sub, blk = sa_ref.shape
    lane = lax.broadcasted_iota(jnp.int32, (LANES, LANES), 1)

    @pl.when(qi == 0)
    def _():
        for sb in range(ct_ref.shape[1]):
            rows = slice(sb * LANES, (sb + 1) * LANES)
            rest = jnp.broadcast_to(ct_ref[:, sb:sb + 1] * (-LOG2E), (LANES, LANES))
            extra = jnp.zeros((LANES, LANES), F32)
            for part in range(N_BIAS_PARTS):
                piece = rest.astype(BF16).astype(F32)
                extra = jnp.where(lane == part, piece, extra)
                rest = rest - piece
            ka_ref[rows, :] = jnp.concatenate([kb_ref[rows, :], extra.astype(BF16)], axis=1)

    ones_feat = jnp.where(lax.broadcasted_iota(jnp.int32, (blk, LANES), 1) < N_BIAS_PARTS,
                          1.0, 0.0).astype(BF16)
    qa_ref[...] = jnp.concatenate([q_ref[...], ones_feat], axis=1)
    ones_rows = jnp.ones((SUM_ROWS, kv_sub), BF16)
    n_groups = blk // QUERY_GROUP
    group = [slice(g * QUERY_GROUP, (g + 1) * QUERY_GROUP) for g in range(n_groups)]

    def scores(c, g, n_keys=kv_sub):
        off = pl.multiple_of(c * kv_sub, kv_sub)
        return lax.dot_general(ka_ref[pl.ds(off, n_keys), :], qa_ref[group[g], :], _NT,
                               preferred_element_type=F32)

    def causal(s, first_key, first_query):
        key = first_key + lax.broadcasted_iota(jnp.int32, s.shape, 0)
        qry = first_query + lax.broadcasted_iota(jnp.int32, s.shape, 1)
        return jnp.where(key <= qry, s, -jnp.inf)

    def softmax(s, m):
        m_new = jnp.maximum(m, jnp.max(s, axis=0, keepdims=True))
        return m_new, jnp.exp2(m - m_new), jnp.exp2(s - m_new).astype(BF16)

    def accumulate(c, g, p, alpha):
        n_keys = p.shape[0]
        vt_sum = jnp.concatenate([vt_ref[c][:, :n_keys], ones_rows[:, :n_keys]], axis=0)
        acc_ref[:, group[g]] = (alpha * acc_ref[:, group[g]]
                                + jnp.dot(vt_sum, p, preferred_element_type=F32))

    for g in range(n_groups):
        sa_ref[:, group[g]] = scores(0, g)
    pb_ref[...] = jnp.zeros(pb_ref.shape, BF16)
    acc_ref[...] = jnp.zeros(acc_ref.shape, F32)

    def body(j, carry):
        m, alpha_b = list(carry[0]), list(carry[1])
        alpha_a = [None] * n_groups
        c = 2 * j
        for g in range(n_groups):
            sb_ref[:, group[g]] = scores(c + 1, g)
            accumulate(jnp.maximum(c - 1, 0), g, pb_ref[:, group[g]], alpha_b[g])
            m[g], alpha_a[g], pa_ref[:, group[g]] = softmax(sa_ref[:, group[g]], m[g])
        for g in range(n_groups):
            sa_ref[:, group[g]] = scores(c + 2, g)
            accumulate(c, g, pa_ref[:, group[g]], alpha_a[g])
            m[g], alpha_b[g], pb_ref[:, group[g]] = softmax(sb_ref[:, group[g]], m[g])
        return tuple(m), tuple(alpha_b)

    init = (tuple(jnp.full((1, QUERY_GROUP), -jnp.inf, F32) for _ in range(n_groups)),
            tuple(jnp.ones((1, QUERY_GROUP), F32) for _ in range(n_groups)))
    m, alpha_b = lax.fori_loop(0, qi, body, init)

    c = 2 * qi
    m, alpha_a, alpha_hi = list(m), [None] * n_groups, [None] * n_groups
    first_query = [g * QUERY_GROUP for g in range(n_groups)]
    keys_a = [min(fq + QUERY_GROUP, kv_sub) for fq in first_query]
    keys_b = [max(fq + QUERY_GROUP - kv_sub, 0) for fq in first_query]
    for g in range(n_groups):
        if keys_b[g]:
            sb_ref[:keys_b[g], group[g]] = scores(c + 1, g, keys_b[g])
        accumulate(jnp.maximum(c - 1, 0), g, pb_ref[:, group[g]], alpha_b[g])
        s = causal(sa_ref[:keys_a[g], group[g]], 0, first_query[g])
        m[g], alpha_a[g], pa_ref[:keys_a[g], group[g]] = softmax(s, m[g])
    for g in range(n_groups):
        accumulate(c, g, pa_ref[:keys_a[g], group[g]], alpha_a[g])
        if keys_b[g]:
            s = causal(sb_ref[:keys_b[g], group[g]], kv_sub, first_query[g])
            _, alpha_hi[g], pb_ref[:keys_b[g], group[g]] = softmax(s, m[g])
    for g in range(n_groups):
        if keys_b[g]:
            accumulate(c + 1, g, pb_ref[:keys_b[g], group[g]], alpha_hi[g])
    out = (acc_ref[:HEAD_DIM, :] / acc_ref[HEAD_DIM:HEAD_DIM + 1, :]).T
    o_ref[...] = (out * sga_ref[...].astype(F32)).astype(BF16)


def _attn_prompt(q, kb, vt, cum_t, sga, *, batch, seq, blk, kv_sub):
    n, d_a = q.shape
    n_heads = d_a // HEAD_DIM
    assert blk == 2 * kv_sub and seq % blk == 0
    nq = seq // blk
    n_chunks = seq // kv_sub
    qmap = lambda b, h, i: (b * nq + i, h)
    return pl.pallas_call(
        _attn_prompt_kernel,
        grid=(batch, n_heads, nq),
        in_specs=[
            pl.BlockSpec((blk, HEAD_DIM), qmap),
            pl.BlockSpec((seq, HEAD_DIM), lambda b, h, i: (b, h)),
            pl.BlockSpec((n_chunks, HEAD_DIM, kv_sub), lambda b, h, i: (b, h, 0)),
            pl.BlockSpec((None, None, LANES, seq // LANES), lambda b, h, i: (h, b, 0, 0)),
            pl.BlockSpec((blk, HEAD_DIM), qmap),
        ],
        out_specs=pl.BlockSpec((blk, HEAD_DIM), qmap),
        out_shape=jax.ShapeDtypeStruct((n, d_a), BF16),
        scratch_shapes=[pltpu.VMEM((seq, HEAD_DIM + LANES), BF16),
                        pltpu.VMEM((kv_sub, blk), F32),
                        pltpu.VMEM((kv_sub, blk), F32),
                        pltpu.VMEM((kv_sub, blk), BF16),
                        pltpu.VMEM((kv_sub, blk), BF16),
                        pltpu.VMEM((HEAD_DIM + SUM_ROWS, blk), F32),
                        pltpu.VMEM((blk, HEAD_DIM + LANES), BF16)],
        compiler_params=_compiler_params(("parallel", "parallel", "arbitrary")),
        name="attn_prompt",
    )(q, kb, vt, cum_t, sga)


def _attn_sample_kernel(q_ref, kn_ref, vn_ref, ck_ref, cv_ref, cp_ref, cn_ref, sga_ref, o_ref,
                        m_ref, l_ref, acc_ref, *, n_heads):
    c = pl.program_id(1)

    @pl.when(c == 0)
    def _():
        m_ref[...] = jnp.full(m_ref.shape, -jnp.inf, F32)
        l_ref[...] = jnp.zeros(l_ref.shape, F32)
        acc_ref[...] = jnp.zeros(acc_ref.shape, F32)

    def head_cols(h):
        return slice(h * HEAD_DIM, (h + 1) * HEAD_DIM)

    chunk = ck_ref.shape[0] // n_heads

    def cached_head(ref, h):
        return ref[pl.ds(h, chunk, stride=n_heads), :].astype(BF16)

    for h in range(n_heads):
        q = q_ref[:, head_cols(h)]
        s = lax.dot_general(q, cached_head(ck_ref, h), _NT, preferred_element_type=F32)
        s = s + cp_ref[h:h + 1, :] * (-LOG2E)
        m, l, acc = _online_softmax_step(s, cached_head(cv_ref, h),
                                         m_ref[h], l_ref[h], acc_ref[h])
        m_ref[h] = m
        l_ref[h] = l
        acc_ref[h] = acc

    @pl.when(c == pl.num_programs(1) - 1)
    def _():
        for h in range(n_heads):
            q = q_ref[:, head_cols(h)]
            s = lax.dot_general(q, kn_ref[:, head_cols(h)], _NT, preferred_element_type=F32)
            s = s + cn_ref[h:h + 1, :] * (-LOG2E)
            rows = lax.broadcasted_iota(jnp.int32, s.shape, 0)
            cols = lax.broadcasted_iota(jnp.int32, s.shape, 1)
            s = jnp.where(cols <= rows, s, -jnp.inf)
            _, l, acc = _online_softmax_step(s, vn_ref[:, head_cols(h)],
                                             m_ref[h], l_ref[h], acc_ref[h])
            o_ref[:, head_cols(h)] = (
                acc / l * sga_ref[:, head_cols(h)].astype(F32)).astype(BF16)


def _attn_sample(q, k_new, v_new, cache_k2, cache_v2, cum_past, cum_new, sga, *,
                 batch, t_new, past, chunk):
    n, d_a = q.shape
    n_heads = d_a // HEAD_DIM
    n_chunks = past // chunk
    new_map = lambda b, c: (b, 0)
    cache_map = lambda b, c: (b * n_chunks + c, 0)
    return pl.pallas_call(
        functools.partial(_attn_sample_kernel, n_heads=n_heads),
        grid=(batch, n_chunks),
        in_specs=[
            pl.BlockSpec((t_new, d_a), new_map),
            pl.BlockSpec((t_new, d_a), new_map),
            pl.BlockSpec((t_new, d_a), new_map),
            pl.BlockSpec((chunk * n_heads, HEAD_DIM), cache_map),
            pl.BlockSpec((chunk * n_heads, HEAD_DIM), cache_map),
            pl.BlockSpec((None, None, n_heads, chunk), lambda b, c: (b, c, 0, 0)),
            pl.BlockSpec((None, n_heads, t_new), lambda b, c: (b, 0, 0)),
            pl.BlockSpec((t_new, d_a), new_map),
        ],
        out_specs=pl.BlockSpec((t_new, d_a), new_map),
        out_shape=jax.ShapeDtypeStruct((n, d_a), BF16),
        scratch_shapes=[pltpu.VMEM((n_heads, t_new, 1), F32),
                        pltpu.VMEM((n_heads, t_new, 1), F32),
                        pltpu.VMEM((n_heads, t_new, HEAD_DIM), F32)],
        compiler_params=_compiler_params(("parallel", "arbitrary")),
        name="attn_sample",
    )(q, k_new, v_new, cache_k2, cache_v2, cum_past, cum_new, sga)


def _out_proj_kernel(oa_ref, ob_ref, w_ref, x_ref, fg_ref, y_ref, *, final_norm):
    o = jnp.concatenate([oa_ref[...], ob_ref[...]], axis=1)
    y = x_ref[...] + jnp.dot(o, w_ref[...], preferred_element_type=F32)
    if final_norm:
        y = _rmsnorm_rows(y, fg_ref[...])
    y_ref[...] = y


def _out_proj(out_a, out_b, w_out, x2, final_g, *, tm, final_norm):
    n, d = x2.shape
    d_a = out_a.shape[1]
    d_b = out_b.shape[1]
    row = lambda i: (i, 0)
    const = lambda i: (0, 0)
    return pl.pallas_call(
        functools.partial(_out_proj_kernel, final_norm=final_norm),
        grid=(n // tm,),
        in_specs=[
            pl.BlockSpec((tm, d_a), row),
            pl.BlockSpec((tm, d_b), row),
            pl.BlockSpec(w_out.shape, const),
            pl.BlockSpec((tm, d), row),
            pl.BlockSpec((1, d), const),
        ],
        out_specs=pl.BlockSpec((tm, d), row),
        out_shape=jax.ShapeDtypeStruct((n, d), F32),
        compiler_params=_compiler_params(("parallel",)),
        name="out_proj",
    )(out_a, out_b, w_out, x2, final_g)


def _pad_rows(a, rows):
    return jnp.pad(a, ((0, rows - a.shape[0]), (0, 0)))


def _mixing_weights(w_s, b_s, seq):
    c = min(seq, GMLP_CHUNK)
    reps = GMLP_CHUNK // c
    w = w_s[:, :c, :c] * jnp.tril(jnp.ones((c, c), w_s.dtype))
    if reps > 1:
        w = jnp.einsum('rs,gab->grasb', jnp.eye(reps, dtype=w.dtype), w)
        w = w.reshape(w_s.shape[0], GMLP_CHUNK, GMLP_CHUNK)
    b = jnp.tile(b_s[:, :c], (1, reps))
    b_full = jnp.broadcast_to(b[:, :, None], (b.shape[0], GMLP_CHUNK, LANES))
    return w.astype(BF16), b_full.astype(F32)


def _prepare_weights(params):
    norm_g, w_in, b_f, ln_g, ln_b, w_s, b_s, w_out = params
    n_heads = b_f.shape[0]
    off_f = 3 * n_heads * HEAD_DIM
    off_ga = off_f + n_heads
    return dict(
        norm_g=norm_g[None], b_f=b_f[:, None], ln_g=ln_g[None], ln_b=ln_b[None],
        w_all=w_in.astype(BF16),
        w_rest=w_in[:, off_ga:].astype(BF16),
        w_ft=_pad_rows(w_in[:, off_f:off_ga].T, 16).astype(BF16),
        w_out=w_out.astype(BF16), w_s=w_s, b_s=b_s)


def _layer(x, w, caches, final_g, *, final_norm, tm):
    batch, seq, d = x.shape
    n = batch * seq
    n_heads = w['b_f'].shape[0]
    d_b = w['ln_g'].shape[1]

    x2 = x.reshape(n, d)
    w_mix, b_mix = _mixing_weights(w['w_s'], w['b_s'], seq)

    blk, kv_sub = 1024, 512
    q, k, kb, v, vb, sga, logft = _proj_attn(
        x2, w['norm_g'], w['w_all'], w['w_rest'], w['w_ft'], w['b_f'], tm=tm,
        v_sub=kv_sub if caches is None else None)
    gm = _proj_gmlp(x2, w['norm_g'], w['w_rest'], w['ln_g'], w['ln_b'], w_mix, b_mix,
                    tm=tm, emit_vn=caches is not None)

    logf = logft.T.reshape(batch, seq, n_heads)
    if caches is None:
        out_b = gm[0]
        vn = None
        group = seq // LANES
        cum = _cumsum_rows(logft.reshape(n_heads * batch * group, LANES), group)
        cum_t = cum.reshape(n_heads, batch, group, LANES).transpose(0, 1, 3, 2)
        out_a = _attn_prompt(q, kb, vb, cum_t, sga, batch=batch, seq=seq, blk=blk, kv_sub=kv_sub)
    else:
        out_b, vn = gm
        cache_k, cache_v, cache_logf = caches
        past = cache_k.shape[1]
        chunk = 1024
        total = past + seq
        group = -(-total // LANES)
        lf_all = jnp.concatenate(
            [cache_logf.astype(F32).transpose(0, 2, 1), logf.transpose(0, 2, 1)], axis=2)
        lf_all = jnp.pad(lf_all, ((0, 0), (0, 0), (0, group * LANES - total)))
        cum = _cumsum_rows(lf_all.reshape(batch * n_heads * group, LANES), group)
        cum = cum.reshape(batch, n_heads, group * LANES)
        cum_past = cum[:, :, :past].reshape(batch, n_heads, past // chunk, chunk)
        cum_past = cum_past.transpose(0, 2, 1, 3)
        cum_new = cum[:, :, past:total]
        out_a = _attn_sample(q, kb, vb, cache_k.reshape(batch * past * n_heads, HEAD_DIM),
                             cache_v.reshape(batch * past * n_heads, HEAD_DIM),
                             cum_past, cum_new, sga,
                             batch=batch, t_new=seq, past=past, chunk=chunk)

    y2 = _out_proj(out_a, out_b, w['w_out'], x2, final_g[None], tm=tm, final_norm=final_norm)
    y = y2.reshape(batch, seq, d)
    k4 = k.reshape(batch, seq, n_heads, HEAD_DIM)
    v4 = v.reshape(batch, seq, n_heads, HEAD_DIM)
    vn3 = None if vn is None else vn.reshape(batch, seq, d_b)
    return y, k4, v4, logf, vn3


def kernel(x_prompt, x_sample, cache_k, cache_v, cache_logf, norm_g, w_in, b_f, ln_g, ln_b,
           w_s, b_s, w_out, final_g):
    depth = norm_g.shape[0]
    hp, hs = x_prompt, x_sample
    kp, vp, fp, ksm, vsm, fsm, gsm = [], [], [], [], [], [], []
    n_sample = x_sample.shape[0] * x_sample.shape[1]
    for l in range(depth):
        w = _prepare_weights(
            (norm_g[l], w_in[l], b_f[l], ln_g[l], ln_b[l], w_s[l], b_s[l], w_out[l]))
        last = l == depth - 1
        hp, k1, v1, f1, _ = _layer(hp, w, None, final_g, final_norm=last, tm=512)
        hs, k2, v2, f2, g2 = _layer(hs, w, (cache_k[l], cache_v[l], cache_logf[l]), final_g,
                                    final_norm=last, tm=n_sample)
        kp.append(k1); vp.append(v1); fp.append(f1)
        ksm.append(k2); vsm.append(v2); fsm.append(f2); gsm.append(g2)
    return (hp, hs, jnp.stack(kp), jnp.stack(vp), jnp.stack(fp),
            jnp.stack(ksm), jnp.stack(vsm), jnp.stack(fsm), jnp.stack(gsm))
```

```python
import functools
import math

import jax
import jax.numpy as jnp
from jax import lax
from jax.experimental import pallas as pl
from jax.experimental.pallas import tpu as pltpu

F32 = jnp.float32
BF16 = jnp.bfloat16

HEAD_DIM = 128
GMLP_CHUNK = 128
RMS_EPS = 1e-6
LN_EPS = 1e-5
LOG2E = math.log2(math.e)
LANES = 128
VMEM_LIMIT_BYTES = 48 * 1024 * 1024

_NT = (((1,), (1,)), ((), ()))


def _compiler_params(semantics):
    return pltpu.CompilerParams(dimension_semantics=semantics,
                                vmem_limit_bytes=VMEM_LIMIT_BYTES)


def _rmsnorm_rows(x, g):
    return x * lax.rsqrt(jnp.mean(x * x, axis=-1, keepdims=True) + RMS_EPS) * g


def _gelu_tanh(x):
    c = math.sqrt(2.0 / math.pi)
    return 0.5 * x * (1.0 + jnp.tanh(c * (x + 0.044715 * (x * x * x))))


def _silu(x):
    return x * jax.nn.sigmoid(x)


def _store_heads_on_sublanes(ref, z, n_heads):
    rows = z.shape[0]
    for h in range(n_heads):
        ref[pl.ds(h, rows, stride=n_heads), :] = z[:, h * HEAD_DIM:(h + 1) * HEAD_DIM]


def _resident(shape, block_index):
    return pl.BlockSpec(shape, lambda i: block_index, pipeline_mode=pl.Buffered(1))


def _norm_q_kernel(x_ref, g_ref, wq_ref, wft_ref, bf_ref, h_ref, q_ref, logft_ref,
                   *, q_scale, n_heads):
    hb = _rmsnorm_rows(x_ref[...], g_ref[...]).astype(BF16)
    h_ref[...] = hb
    q_ref[...] = (jnp.dot(hb, wq_ref[...], preferred_element_type=F32) * q_scale).astype(BF16)
    zf = lax.dot_general(wft_ref[...], hb, _NT, preferred_element_type=F32)
    t = zf[:n_heads] + bf_ref[...]
    logft_ref[...] = jnp.minimum(t, 0.0) - jnp.log1p(jnp.exp(-jnp.abs(t)))


def _norm_q(x2, norm_g, w_qkv, w_ft, b_f, *, tm):
    n, d = x2.shape
    n_heads = b_f.shape[0]
    d_a = n_heads * HEAD_DIM
    row = lambda i: (i, 0)
    return pl.pallas_call(
        functools.partial(_norm_q_kernel, q_scale=HEAD_DIM ** -0.5 * LOG2E, n_heads=n_heads),
        grid=(n // tm,),
        in_specs=[
            pl.BlockSpec((tm, d), row),
            _resident((1, d), (0, 0)),
            _resident((d, d_a), (0, 0)),
            _resident(w_ft.shape, (0, 0)),
            _resident((n_heads, 1), (0, 0)),
        ],
        out_specs=[
            pl.BlockSpec((tm, d), row),
            pl.BlockSpec((tm, d_a), row),
            pl.BlockSpec((n_heads, tm), lambda i: (0, i)),
        ],
        out_shape=[
            jax.ShapeDtypeStruct((n, d), BF16),
            jax.ShapeDtypeStruct((n, d_a), BF16),
            jax.ShapeDtypeStruct((n_heads, n), F32),
        ],
        compiler_params=_compiler_params(("parallel",)),
        name="norm_q",
    )(x2, norm_g, w_qkv, w_ft, b_f)


def _proj_kv_kernel(h_ref, wk_ref, wv_ref, wga_ref, k_ref, kb_ref, v_ref, vb_ref, sga_ref,
                    *, n_heads, v_sub):
    h = h_ref[...]
    z = jnp.dot(h, wk_ref[...], preferred_element_type=F32)
    _store_heads_on_sublanes(k_ref, z, n_heads)
    kb_ref[...] = z.astype(BF16)
    z = jnp.dot(h, wv_ref[...], preferred_element_type=F32)
    _store_heads_on_sublanes(v_ref, z, n_heads)
    if v_sub is None:
        vb_ref[...] = z.astype(BF16)
    else:
        for c in range(z.shape[0] // v_sub):
            vb_ref[c] = z[c * v_sub:(c + 1) * v_sub, :].T.astype(BF16)
    sga_ref[...] = _silu(jnp.dot(h, wga_ref[...], preferred_element_type=F32)).astype(BF16)


def _proj_kv(h, w_qkv, w_rest, *, n_heads, tm, v_sub=None):
    n, d = h.shape
    d_a = n_heads * HEAD_DIM
    row = lambda i: (i, 0)
    if v_sub is None:
        vb_spec = pl.BlockSpec((tm, d_a), row)
        vb_shape = jax.ShapeDtypeStruct((n, d_a), BF16)
    else:
        vb_spec = pl.BlockSpec((tm // v_sub, d_a, v_sub), lambda i: (i, 0, 0))
        vb_shape = jax.ShapeDtypeStruct((n // v_sub, d_a, v_sub), BF16)
    return pl.pallas_call(
        functools.partial(_proj_kv_kernel, n_heads=n_heads, v_sub=v_sub),
        grid=(n // tm,),
        in_specs=[
            pl.BlockSpec((tm, d), row),
            _resident((d, d_a), (0, 1)),
            _resident((d, d_a), (0, 2)),
            _resident((d, d_a), (0, 0)),
        ],
        out_specs=[
            pl.BlockSpec((tm * n_heads, HEAD_DIM), row),
            pl.BlockSpec((tm, d_a), row),
            pl.BlockSpec((tm * n_heads, HEAD_DIM), row),
            vb_spec,
            pl.BlockSpec((tm, d_a), row),
        ],
        out_shape=[
            jax.ShapeDtypeStruct((n * n_heads, HEAD_DIM), F32),
            jax.ShapeDtypeStruct((n, d_a), BF16),
            jax.ShapeDtypeStruct((n * n_heads, HEAD_DIM), F32),
            vb_shape,
            jax.ShapeDtypeStruct((n, d_a), BF16),
        ],
        compiler_params=_compiler_params(("parallel",)),
        name="proj_kv",
    )(h, w_qkv, w_qkv, w_rest)


def _proj_gmlp_kernel(h_ref, wu_ref, wv_ref, wg_ref, lng_ref, lnb_ref, wmix_ref, bmix_ref,
                      ob_ref, *rest, n_groups, emit_vn):
    if emit_vn:
        vn_ref, mix_ref = rest
    else:
        (mix_ref,) = rest
    h = h_ref[...]
    tm = h.shape[0]
    n_chunks = tm // GMLP_CHUNK
    d_g = wv_ref.shape[1] // n_groups

    a = _gelu_tanh(jnp.dot(h, wv_ref[...], preferred_element_type=F32))
    mu = jnp.mean(a, axis=-1, keepdims=True)
    ac = a - mu
    var = jnp.mean(ac * ac, axis=-1, keepdims=True)
    vn = ac * lax.rsqrt(var + LN_EPS) * lng_ref[...] + lnb_ref[...]
    if emit_vn:
        vn_ref[...] = vn
    vb = vn.astype(BF16)
    for g in range(n_groups):
        cols = slice(g * d_g, (g + 1) * d_g)
        rhs = jnp.concatenate(
            [vb[r * GMLP_CHUNK:(r + 1) * GMLP_CHUNK, cols] for r in range(n_chunks)], axis=1)
        mixed = jnp.dot(wmix_ref[g], rhs, preferred_element_type=F32)
        for r in range(n_chunks):
            mix_ref[r * GMLP_CHUNK:(r + 1) * GMLP_CHUNK, cols] = (
                mixed[:, r * d_g:(r + 1) * d_g] + bmix_ref[g])

    gated = _gelu_tanh(jnp.dot(h, wu_ref[...], preferred_element_type=F32)) * mix_ref[...]
    gate = _silu(jnp.dot(h, wg_ref[...], preferred_element_type=F32))
    ob_ref[...] = (gated * gate).astype(BF16)


def _proj_gmlp(h, w_rest, ln_g, ln_b, w_mix, b_mix, *, tm, emit_vn):
    n, d = h.shape
    d_b = ln_g.shape[1]
    n_groups = w_mix.shape[0]
    row = lambda i: (i, 0)
    out_specs = [pl.BlockSpec((tm, d_b), row)]
    out_shape = [jax.ShapeDtypeStruct((n, d_b), BF16)]
    if emit_vn:
        out_specs.append(pl.BlockSpec((tm, d_b), row))
        out_shape.append(jax.ShapeDtypeStruct((n, d_b), F32))
    return pl.pallas_call(
        functools.partial(_proj_gmlp_kernel, n_groups=n_groups, emit_vn=emit_vn),
        grid=(n // tm,),
        in_specs=[
            pl.BlockSpec((tm, d), row),
            _resident((d, d_b), (0, 1)),
            _resident((d, d_b), (0, 2)),
            _resident((d, d_b), (0, 3)),
            _resident((1, d_b), (0, 0)),
            _resident((1, d_b), (0, 0)),
            _resident(w_mix.shape, (0, 0, 0)),
            _resident(b_mix.shape, (0, 0, 0)),
        ],
        out_specs=out_specs,
        out_shape=out_shape,
        scratch_shapes=[pltpu.VMEM((tm, d_b), F32)],
        compiler_params=_compiler_params(("parallel",)),
        name="proj_gmlp",
    )(h, w_rest, w_rest, w_rest, ln_g, ln_b, w_mix, b_mix)


def _cumsum_kernel(x_ref, o_ref, *, group):
    x = x_ref[...]
    lane = lax.broadcasted_iota(jnp.int32, x.shape, 1)
    sh = 1
    while sh < LANES:
        x = x + jnp.where(lane >= sh, pltpu.roll(x, sh, axis=1), 0.0)
        sh *= 2
    sub = lax.broadcasted_iota(jnp.int32, x.shape, 0) % group
    tot = jnp.broadcast_to(x[:, LANES - 1:LANES], x.shape)
    carry = jnp.where(sub >= 1, pltpu.roll(tot, 1, axis=0), 0.0)
    sh = 1
    while sh < group:
        carry = carry + jnp.where(sub >= sh, pltpu.roll(carry, sh, axis=0), 0.0)
        sh *= 2
    o_ref[...] = x + carry


def _cumsum_rows(x2, group):
    return pl.pallas_call(
        functools.partial(_cumsum_kernel, group=group),
        out_shape=jax.ShapeDtypeStruct(x2.shape, F32),
        name="cumsum_rows",
    )(x2)


def _online_softmax_step(s, v_blk, m, l, acc):
    m_new = jnp.maximum(m, jnp.max(s, axis=1, keepdims=True))
    alpha = jnp.exp2(m - m_new)
    p = jnp.exp2(s - m_new)
    l = alpha * l + jnp.sum(p, axis=1, keepdims=True)
    acc = alpha * acc + jnp.dot(p.astype(BF16), v_blk, preferred_element_type=F32)
    return m_new, l, acc


QUERY_GROUP = 256
N_BIAS_PARTS = 3
SUM_ROWS = 16


def _attn_prompt_kernel(q_ref, kb_ref, vt_ref, ct_ref, sga_ref, o_ref,
                        ka_ref, sa_ref, sb_ref, pa_ref, pb_ref, acc_ref, qa_ref):
    qi = pl.program_id(2)
    kv_sub, blk = sa_ref.shape
    lane = lax.broadcasted_iota(jnp.int32, (LANES, LANES), 1)

    @pl.when(qi == 0)
    def _():
        for sb in range(ct_ref.shape[1]):
            rows = slice(sb * LANES, (sb + 1) * LANES)
            rest = jnp.broadcast_to(ct_ref[:, sb:sb + 1] * (-LOG2E), (LANES, LANES))
            extra = jnp.zeros((LANES, LANES), F32)
            for part in range(N_BIAS_PARTS):
                piece = rest.astype(BF16).astype(F32)
                extra = jnp.where(lane == part, piece, extra)
                rest = rest - piece
            ka_ref[rows, :] = jnp.concatenate([kb_ref[rows, :], extra.astype(BF16)], axis=1)

    ones_feat = jnp.where(lax.broadcasted_iota(jnp.int32, (blk, LANES), 1) < N_BIAS_PARTS,
                          1.0, 0.0).astype(BF16)
    qa_ref[...] = jnp.concatenate([q_ref[...], ones_feat], axis=1)
    ones_rows = jnp.ones((SUM_ROWS, kv_sub), BF16)
    n_groups = blk // QUERY_GROUP
    group = [slice(g * QUERY_GROUP, (g + 1) * QUERY_GROUP) for g in range(n_groups)]

    def scores(c, g, n_keys=kv_sub):
        off = pl.multiple_of(c * kv_sub, kv_sub)
        return lax.dot_general(ka_ref[pl.ds(off, n_keys), :], qa_ref[group[g], :], _NT,
                               preferred_element_type=F32)

    def causal(s, first_key, first_query):
        key = first_key + lax.broadcasted_iota(jnp.int32, s.shape, 0)
        qry = first_query + lax.broadcasted_iota(jnp.int32, s.shape, 1)
        return jnp.where(key <= qry, s, -jnp.inf)

    def softmax(s, m):
        m_new = jnp.maximum(m, jnp.max(s, axis=0, keepdims=True))
        return m_new, jnp.exp2(m - m_new), jnp.exp2(s - m_new).astype(BF16)

    def accumulate(c, g, p, alpha):
        n_keys = p.shape[0]
        vt_sum = jnp.concatenate([vt_ref[c][:, :n_keys], ones_rows[:, :n_keys]], axis=0)
        acc_ref[:, group[g]] = (alpha * acc_ref[:, group[g]]
                                + jnp.dot(vt_sum, p, preferred_element_type=F32))

    for g in range(n_groups):
        sa_ref[:, group[g]] = scores(0, g)
    pb_ref[...] = jnp.zeros(pb_ref.shape, BF16)
    acc_ref[...] = jnp.zeros(acc_ref.shape, F32)

    def body(j, carry):
        m, alpha_b = list(carry[0]), list(carry[1])
        alpha_a = [None] * n_groups
        c = 2 * j
        for g in range(n_groups):
            sb_ref[:, group[g]] = scores(c + 1, g)
            accumulate(jnp.maximum(c - 1, 0), g, pb_ref[:, group[g]], alpha_b[g])
            m[g], alpha_a[g], pa_ref[:, group[g]] = softmax(sa_ref[:, group[g]], m[g])
        for g in range(n_groups):
            sa_ref[:, group[g]] = scores(c + 2, g)
            accumulate(c, g, pa_ref[:, group[g]], alpha_a[g])
            m[g], alpha_b[g], pb_ref[:, group[g]] = softmax(sb_ref[:, group[g]], m[g])
        return tuple(m), tuple(alpha_b)

    init = (tuple(jnp.full((1, QUERY_GROUP), -jnp.inf, F32) for _ in range(n_groups)),
            tuple(jnp.ones((1, QUERY_GROUP), F32) for _ in range(n_groups)))
    m, alpha_b = lax.fori_loop(0, qi, body, init)

    c = 2 * qi
    m, alpha_a, alpha_hi = list(m), [None] * n_groups, [None] * n_groups
    first_query = [g * QUERY_GROUP for g in range(n_groups)]
    keys_a = [min(fq + QUERY_GROUP, kv_sub) for fq in first_query]
    keys_b = [max(fq + QUERY_GROUP - kv_sub, 0) for fq in first_query]
    for g in range(n_groups):
        if keys_b[g]:
            sb_ref[:keys_b[g], group[g]] = scores(c + 1, g, keys_b[g])
        accumulate(jnp.maximum(c - 1, 0), g, pb_ref[:, group[g]], alpha_b[g])
        s = causal(sa_ref[:keys_a[g], group[g]], 0, first_query[g])
        m[g], alpha_a[g], pa_ref[:keys_a[g], group[g]] = softmax(s, m[g])
    for g in range(n_groups):
        accumulate(c, g, pa_ref[:keys_a[g], group[g]], alpha_a[g])
        if keys_b[g]:
            s = causal(sb_ref[:keys_b[g], group[g]], kv_sub, first_query[g])
            _, alpha_hi[g], pb_ref[:keys_b[g], group[g]] = softmax(s, m[g])
    for g in range(n_groups):
        if keys_b[g]:
            accumulate(c + 1, g, pb_ref[:keys_b[g], group[g]], alpha_hi[g])
    out = (acc_ref[:HEAD_DIM, :] / acc_ref[HEAD_DIM:HEAD_DIM + 1, :]).T
    o_ref[...] = (out * sga_ref[...].astype(F32)).astype(BF16)


def _attn_prompt(q, kb, vt, cum_t, sga, *, batch, seq, blk, kv_sub):
    n, d_a = q.shape
    n_heads = d_a // HEAD_DIM
    assert blk == 2 * kv_sub and seq % blk == 0
    nq = seq // blk
    n_chunks = seq // kv_sub
    qmap = lambda b, h, i: (b * nq + i, h)
    return pl.pallas_call(
        _attn_prompt_kernel,
        grid=(batch, n_heads, nq),
        in_specs=[
            pl.BlockSpec((blk, HEAD_DIM), qmap),
            pl.BlockSpec((seq, HEAD_DIM), lambda b, h, i: (b, h)),
            pl.BlockSpec((n_chunks, HEAD_DIM, kv_sub), lambda b, h, i: (b, h, 0)),
            pl.BlockSpec((None, None, LANES, seq // LANES), lambda b, h, i: (h, b, 0, 0)),
            pl.BlockSpec((blk, HEAD_DIM), qmap),
        ],
        out_specs=pl.BlockSpec((blk, HEAD_DIM), qmap),
        out_shape=jax.ShapeDtypeStruct((n, d_a), BF16),
        scratch_shapes=[pltpu.VMEM((seq, HEAD_DIM + LANES), BF16),
                        pltpu.VMEM((kv_sub, blk), F32),
                        pltpu.VMEM((kv_sub, blk), F32),
                        pltpu.VMEM((kv_sub, blk), BF16),
                        pltpu.VMEM((kv_sub, blk), BF16),
                        pltpu.VMEM((HEAD_DIM + SUM_ROWS, blk), F32),
                        pltpu.VMEM((blk, HEAD_DIM + LANES), BF16)],
        compiler_params=_compiler_params(("parallel", "parallel", "arbitrary")),
        name="attn_prompt",
    )(q, kb, vt, cum_t, sga)


def _attn_sample_kernel(q_ref, kn_ref, vn_ref, ck_ref, cv_ref, cp_ref, cn_ref, sga_ref, o_ref,
                        m_ref, l_ref, acc_ref, *, n_heads):
    c = pl.program_id(1)

    @pl.when(c == 0)
    def _():
        m_ref[...] = jnp.full(m_ref.shape, -jnp.inf, F32)
        l_ref[...] = jnp.zeros(l_ref.shape, F32)
        acc_ref[...] = jnp.zeros(acc_ref.shape, F32)

    def head_cols(h):
        return slice(h * HEAD_DIM, (h + 1) * HEAD_DIM)

    chunk = ck_ref.shape[0] // n_heads

    def cached_head(ref, h):
        return ref[pl.ds(h, chunk, stride=n_heads), :].astype(BF16)

    for h in range(n_heads):
        q = q_ref[:, head_cols(h)]
        s = lax.dot_general(q, cached_head(ck_ref, h), _NT, preferred_element_type=F32)
        s = s + cp_ref[h:h + 1, :] * (-LOG2E)
        m, l, acc = _online_softmax_step(s, cached_head(cv_ref, h),
                                         m_ref[h], l_ref[h], acc_ref[h])
        m_ref[h] = m
        l_ref[h] = l
        acc_ref[h] = acc

    @pl.when(c == pl.num_programs(1) - 1)
    def _():
        for h in range(n_heads):
            q = q_ref[:, head_cols(h)]
            s = lax.dot_general(q, kn_ref[:, head_cols(h)], _NT, preferred_element_type=F32)
            s = s + cn_ref[h:h + 1, :] * (-LOG2E)
            rows = lax.broadcasted_iota(jnp.int32, s.shape, 0)
            cols = lax.broadcasted_iota(jnp.int32, s.shape, 1)
            s = jnp.where(cols <= rows, s, -jnp.inf)
            _, l, acc = _online_softmax_step(s, vn_ref[:, head_cols(h)],
                                             m_ref[h], l_ref[h], acc_ref[h])
            o_ref[:, head_cols(h)] = (
                acc / l * sga_ref[:, head_cols(h)].astype(F32)).astype(BF16)


def _attn_sample(q, k_new, v_new, cache_k2, cache_v2, cum_past, cum_new, sga, *,
                 batch, t_new, past, chunk):
    n, d_a = q.shape
    n_heads = d_a // HEAD_DIM
    n_chunks = past // chunk
    new_map = lambda b, c: (b, 0)
    cache_map = lambda b, c: (b * n_chunks + c, 0)
    return pl.pallas_call(
        functools.partial(_attn_sample_kernel, n_heads=n_heads),
        grid=(batch, n_chunks),
        in_specs=[
            pl.BlockSpec((t_new, d_a), new_map),
            pl.BlockSpec((t_new, d_a), new_map),
            pl.BlockSpec((t_new, d_a), new_map),
            pl.BlockSpec((chunk * n_heads, HEAD_DIM), cache_map),
            pl.BlockSpec((chunk * n_heads, HEAD_DIM), cache_map),
            pl.BlockSpec((None, None, n_heads, chunk), lambda b, c: (b, c, 0, 0)),
            pl.BlockSpec((None, n_heads, t_new), lambda b, c: (b, 0, 0)),
            pl.BlockSpec((t_new, d_a), new_map),
        ],
        out_specs=pl.BlockSpec((t_new, d_a), new_map),
        out_shape=jax.ShapeDtypeStruct((n, d_a), BF16),
        scratch_shapes=[pltpu.VMEM((n_heads, t_new, 1), F32),
                        pltpu.VMEM((n_heads, t_new, 1), F32),
                        pltpu.VMEM((n_heads, t_new, HEAD_DIM), F32)],
        compiler_params=_compiler_params(("parallel", "arbitrary")),
        name="attn_sample",
    )(q, k_new, v_new, cache_k2, cache_v2, cum_past, cum_new, sga)


def _out_proj_kernel(oa_ref, ob_ref, w_ref, x_ref, fg_ref, y_ref, *, final_norm):
    o = jnp.concatenate([oa_ref[...], ob_ref[...]], axis=1)
    y = x_ref[...] + jnp.dot(o, w_ref[...], preferred_element_type=F32)
    if final_norm:
        y = _rmsnorm_rows(y, fg_ref[...])
    y_ref[...] = y


def _out_proj(out_a, out_b, w_out, x2, final_g, *, tm, final_norm):
    n, d = x2.shape
    d_a = out_a.shape[1]
    d_b = out_b.shape[1]
    row = lambda i: (i, 0)
    const = lambda i: (0, 0)
    return pl.pallas_call(
        functools.partial(_out_proj_kernel, final_norm=final_norm),
        grid=(n // tm,),
        in_specs=[
            pl.BlockSpec((tm, d_a), row),
            pl.BlockSpec((tm, d_b), row),
            pl.BlockSpec(w_out.shape, const),
            pl.BlockSpec((tm, d), row),
            pl.BlockSpec((1, d), const),
        ],
        out_specs=pl.BlockSpec((tm, d), row),
        out_shape=jax.ShapeDtypeStruct((n, d), F32),
        compiler_params=_compiler_params(("parallel",)),
        name="out_proj",
    )(out_a, out_b, w_out, x2, final_g)


def _pad_rows(a, rows):
    return jnp.pad(a, ((0, rows - a.shape[0]), (0, 0)))


def _mixing_weights(w_s, b_s, seq):
    c = min(seq, GMLP_CHUNK)
    reps = GMLP_CHUNK // c
    w = w_s[:, :c, :c] * jnp.tril(jnp.ones((c, c), w_s.dtype))
    if reps > 1:
        w = jnp.einsum('rs,gab->grasb', jnp.eye(reps, dtype=w.dtype), w)
        w = w.reshape(w_s.shape[0], GMLP_CHUNK, GMLP_CHUNK)
    b = jnp.tile(b_s[:, :c], (1, reps))
    b_full = jnp.broadcast_to(b[:, :, None], (b.shape[0], GMLP_CHUNK, LANES))
    return w.astype(BF16), b_full.astype(F32)


def _prepare_weights(params):
    norm_g, w_in, b_f, ln_g, ln_b, w_s, b_s, w_out = params
    n_heads = b_f.shape[0]
    off_f = 3 * n_heads * HEAD_DIM
    off_ga = off_f + n_heads
    return dict(
        norm_g=norm_g[None], b_f=b_f[:, None], ln_g=ln_g[None], ln_b=ln_b[None],
        w_qkv=w_in[:, :off_f].astype(BF16),
        w_rest=w_in[:, off_ga:].astype(BF16),
        w_ft=_pad_rows(w_in[:, off_f:off_ga].T, 16).astype(BF16),
        w_out=w_out.astype(BF16), w_s=w_s, b_s=b_s)


def _layer(x, w, caches, final_g, *, final_norm, tm):
    batch, seq, d = x.shape
    n = batch * seq
    n_heads = w['b_f'].shape[0]
    d_b = w['ln_g'].shape[1]

    x2 = x.reshape(n, d)
    w_mix, b_mix = _mixing_weights(w['w_s'], w['b_s'], seq)

    blk, kv_sub = 1024, 512
    h, q, logft = _norm_q(x2, w['norm_g'], w['w_qkv'], w['w_ft'], w['b_f'], tm=tm)
    k, kb, v, vb, sga = _proj_kv(h, w['w_qkv'], w['w_rest'], n_heads=n_heads, tm=tm,
                                 v_sub=kv_sub if caches is None else None)
    gm = _proj_gmlp(h, w['w_rest'], w['ln_g'], w['ln_b'], w_mix, b_mix,
                    tm=tm, emit_vn=caches is not None)

    logf = logft.T.reshape(batch, seq, n_heads)
    if caches is None:
        out_b = gm[0]
        vn = None
        group = seq // LANES
        cum = _cumsum_rows(logft.reshape(n_heads * batch * group, LANES), group)
        cum_t = cum.reshape(n_heads, batch, group, LANES).transpose(0, 1, 3, 2)
        out_a = _attn_prompt(q, kb, vb, cum_t, sga, batch=batch, seq=seq, blk=blk, kv_sub=kv_sub)
    else:
        out_b, vn = gm
        cache_k, cache_v, cache_logf = caches
        past = cache_k.shape[1]
        chunk = 1024
        total = past + seq
        group = -(-total // LANES)
        lf_all = jnp.concatenate(
            [cache_logf.astype(F32).transpose(0, 2, 1), logf.transpose(0, 2, 1)], axis=2)
        lf_all = jnp.pad(lf_all, ((0, 0), (0, 0), (0, group * LANES - total)))
        cum = _cumsum_rows(lf_all.reshape(batch * n_heads * group, LANES), group)
        cum = cum.reshape(batch, n_heads, group * LANES)
        cum_past = cum[:, :, :past].reshape(batch, n_heads, past // chunk, chunk)
        cum_past = cum_past.transpose(0, 2, 1, 3)
        cum_new = cum[:, :, past:total]
        out_a = _attn_sample(q, kb, vb, cache_k.reshape(batch * past * n_heads, HEAD_DIM),
                             cache_v.reshape(batch * past * n_heads, HEAD_DIM),
                             cum_past, cum_new, sga,
                             batch=batch, t_new=seq, past=past, chunk=chunk)

    y2 = _out_proj(out_a, out_b, w['w_out'], x2, final_g[None], tm=tm, final_norm=final_norm)
    y = y2.reshape(batch, seq, d)
    k4 = k.reshape(batch, seq, n_heads, HEAD_DIM)
    v4 = v.reshape(batch, seq, n_heads, HEAD_DIM)
    vn3 = None if vn is None else vn.reshape(batch, seq, d_b)
    return y, k4, v4, logf, vn3


def kernel(x_prompt, x_sample, cache_k, cache_v, cache_logf, norm_g, w_in, b_f, ln_g, ln_b,
           w_s, b_s, w_out, final_g):
    depth = norm_g.shape[0]
    hp, hs = x_prompt, x_sample
    kp, vp, fp, ksm, vsm, fsm, gsm = [], [], [], [], [], [], []
    n_sample = x_sample.shape[0] * x_sample.shape[1]
    for l in range(depth):
        w = _prepare_weights(
            (norm_g[l], w_in[l], b_f[l], ln_g[l], ln_b[l], w_s[l], b_s[l], w_out[l]))
        last = l == depth - 1
        hp, k1, v1, f1, _ = _layer(hp, w, None, final_g, final_norm=last, tm=512)
        hs, k2, v2, f2, g2 = _layer(hs, w, (cache_k[l], cache_v[l], cache_logf[l]), final_g,
                                    final_norm=last, tm=n_sample)
        kp.append(k1); vp.append(v1); fp.append(f1)
        ksm.append(k2); vsm.append(v2); fsm.append(f2); gsm.append(g2)
    return (hp, hs, jnp.stack(kp), jnp.stack(vp), jnp.stack(fp),
            jnp.stack(ksm), jnp.stack(vsm), jnp.stack(fsm), jnp.stack(gsm))
```

```python
import functools
import math

import jax
import jax.numpy as jnp
from jax import lax
from jax.experimental import pallas as pl
from jax.experimental.pallas import tpu as pltpu

F32 = jnp.float32
BF16 = jnp.bfloat16

HEAD_DIM = 128
GMLP_CHUNK = 128
RMS_EPS = 1e-6
LN_EPS = 1e-5
LOG2E = math.log2(math.e)
LANES = 128
VMEM_LIMIT_BYTES = 48 * 1024 * 1024

_NT = (((1,), (1,)), ((), ()))


def _compiler_params(semantics):
    return pltpu.CompilerParams(dimension_semantics=semantics,
                                vmem_limit_bytes=VMEM_LIMIT_BYTES)


def _rmsnorm_rows(x, g):
    return x * lax.rsqrt(jnp.mean(x * x, axis=-1, keepdims=True) + RMS_EPS) * g


def _gelu_tanh(x):
    c = math.sqrt(2.0 / math.pi)
    return 0.5 * x * (1.0 + jnp.tanh(c * (x + 0.044715 * (x * x * x))))


def _silu(x):
    return x * jax.nn.sigmoid(x)


def _store_heads_on_sublanes(ref, z, n_heads):
    rows = z.shape[0]
    for h in range(n_heads):
        ref[pl.ds(h, rows, stride=n_heads), :] = z[:, h * HEAD_DIM:(h + 1) * HEAD_DIM]


def _resident(shape, block_index):
    return pl.BlockSpec(shape, lambda i: block_index, pipeline_mode=pl.Buffered(1))


def _norm_q_kernel(x_ref, g_ref, wq_ref, wft_ref, bf_ref, h_ref, q_ref, logft_ref,
                   *, q_scale, n_heads):
    hb = _rmsnorm_rows(x_ref[...], g_ref[...]).astype(BF16)
    h_ref[...] = hb
    q_ref[...] = (jnp.dot(hb, wq_ref[...], preferred_element_type=F32) * q_scale).astype(BF16)
    zf = lax.dot_general(wft_ref[...], hb, _NT, preferred_element_type=F32)
    t = zf[:n_heads] + bf_ref[...]
    logft_ref[...] = jnp.minimum(t, 0.0) - jnp.log1p(jnp.exp(-jnp.abs(t)))


def _norm_q(x2, norm_g, w_qkv, w_ft, b_f, *, tm):
    n, d = x2.shape
    n_heads = b_f.shape[0]
    d_a = n_heads * HEAD_DIM
    row = lambda i: (i, 0)
    return pl.pallas_call(
        functools.partial(_norm_q_kernel, q_scale=HEAD_DIM ** -0.5 * LOG2E, n_heads=n_heads),
        grid=(n // tm,),
        in_specs=[
            pl.BlockSpec((tm, d), row),
            _resident((1, d), (0, 0)),
            _resident((d, d_a), (0, 0)),
            _resident(w_ft.shape, (0, 0)),
            _resident((n_heads, 1), (0, 0)),
        ],
        out_specs=[
            pl.BlockSpec((tm, d), row),
            pl.BlockSpec((tm, d_a), row),
            pl.BlockSpec((n_heads, tm), lambda i: (0, i)),
        ],
        out_shape=[
            jax.ShapeDtypeStruct((n, d), BF16),
            jax.ShapeDtypeStruct((n, d_a), BF16),
            jax.ShapeDtypeStruct((n_heads, n), F32),
        ],
        compiler_params=_compiler_params(("parallel",)),
        name="norm_q",
    )(x2, norm_g, w_qkv, w_ft, b_f)


def _proj_kv_kernel(h_ref, wk_ref, wv_ref, wga_ref, k_ref, kb_ref, v_ref, vb_ref, sga_ref,
                    *, n_heads, v_sub):
    h = h_ref[...]
    z = jnp.dot(h, wk_ref[...], preferred_element_type=F32)
    _store_heads_on_sublanes(k_ref, z, n_heads)
    kb_ref[...] = z.astype(BF16)
    z = jnp.dot(h, wv_ref[...], preferred_element_type=F32)
    _store_heads_on_sublanes(v_ref, z, n_heads)
    if v_sub is None:
        vb_ref[...] = z.astype(BF16)
    else:
        for c in range(z.shape[0] // v_sub):
            vb_ref[c] = z[c * v_sub:(c + 1) * v_sub, :].T.astype(BF16)
    sga_ref[...] = _silu(jnp.dot(h, wga_ref[...], preferred_element_type=F32)).astype(BF16)


def _proj_kv(h, w_qkv, w_rest, *, n_heads, tm, v_sub=None):
    n, d = h.shape
    d_a = n_heads * HEAD_DIM
    row = lambda i: (i, 0)
    if v_sub is None:
        vb_spec = pl.BlockSpec((tm, d_a), row)
        vb_shape = jax.ShapeDtypeStruct((n, d_a), BF16)
    else:
        vb_spec = pl.BlockSpec((tm // v_sub, d_a, v_sub), lambda i: (i, 0, 0))
        vb_shape = jax.ShapeDtypeStruct((n // v_sub, d_a, v_sub), BF16)
    return pl.pallas_call(
        functools.partial(_proj_kv_kernel, n_heads=n_heads, v_sub=v_sub),
        grid=(n // tm,),
        in_specs=[
            pl.BlockSpec((tm, d), row),
            _resident((d, d_a), (0, 1)),
            _resident((d, d_a), (0, 2)),
            _resident((d, d_a), (0, 0)),
        ],
        out_specs=[
            pl.BlockSpec((tm * n_heads, HEAD_DIM), row),
            pl.BlockSpec((tm, d_a), row),
            pl.BlockSpec((tm * n_heads, HEAD_DIM), row),
            vb_spec,
            pl.BlockSpec((tm, d_a), row),
        ],
        out_shape=[
            jax.ShapeDtypeStruct((n * n_heads, HEAD_DIM), F32),
            jax.ShapeDtypeStruct((n, d_a), BF16),
            jax.ShapeDtypeStruct((n * n_heads, HEAD_DIM), F32),
            vb_shape,
            jax.ShapeDtypeStruct((n, d_a), BF16),
        ],
        compiler_params=_compiler_params(("parallel",)),
        name="proj_kv",
    )(h, w_qkv, w_qkv, w_rest)


def _proj_gmlp_kernel(h_ref, wu_ref, wv_ref, wg_ref, lng_ref, lnb_ref, wmix_ref, bmix_ref,
                      ob_ref, *rest, n_groups, emit_vn):
    if emit_vn:
        vn_ref, mix_ref = rest
    else:
        (mix_ref,) = rest
    h = h_ref[...]
    tm = h.shape[0]
    n_chunks = tm // GMLP_CHUNK
    d_g = wv_ref.shape[1] // n_groups

    a = _gelu_tanh(jnp.dot(h, wv_ref[...], preferred_element_type=F32))
    mu = jnp.mean(a, axis=-1, keepdims=True)
    ac = a - mu
    var = jnp.mean(ac * ac, axis=-1, keepdims=True)
    vn = ac * lax.rsqrt(var + LN_EPS) * lng_ref[...] + lnb_ref[...]
    if emit_vn:
        vn_ref[...] = vn
    vb = vn.astype(BF16)
    for g in range(n_groups):
        cols = slice(g * d_g, (g + 1) * d_g)
        rhs = jnp.concatenate(
            [vb[r * GMLP_CHUNK:(r + 1) * GMLP_CHUNK, cols] for r in range(n_chunks)], axis=1)
        mixed = jnp.dot(wmix_ref[g], rhs, preferred_element_type=F32)
        for r in range(n_chunks):
            mix_ref[r * GMLP_CHUNK:(r + 1) * GMLP_CHUNK, cols] = (
                mixed[:, r * d_g:(r + 1) * d_g] + bmix_ref[g])

    gated = _gelu_tanh(jnp.dot(h, wu_ref[...], preferred_element_type=F32)) * mix_ref[...]
    gate = _silu(jnp.dot(h, wg_ref[...], preferred_element_type=F32))
    ob_ref[...] = (gated * gate).astype(BF16)


def _proj_gmlp(h, w_rest, ln_g, ln_b, w_mix, b_mix, *, tm, emit_vn):
    n, d = h.shape
    d_b = ln_g.shape[1]
    n_groups = w_mix.shape[0]
    row = lambda i: (i, 0)
    out_specs = [pl.BlockSpec((tm, d_b), row)]
    out_shape = [jax.ShapeDtypeStruct((n, d_b), BF16)]
    if emit_vn:
        out_specs.append(pl.BlockSpec((tm, d_b), row))
        out_shape.append(jax.ShapeDtypeStruct((n, d_b), F32))
    return pl.pallas_call(
        functools.partial(_proj_gmlp_kernel, n_groups=n_groups, emit_vn=emit_vn),
        grid=(n // tm,),
        in_specs=[
            pl.BlockSpec((tm, d), row),
            _resident((d, d_b), (0, 1)),
            _resident((d, d_b), (0, 2)),
            _resident((d, d_b), (0, 3)),
            _resident((1, d_b), (0, 0)),
            _resident((1, d_b), (0, 0)),
            _resident(w_mix.shape, (0, 0, 0)),
            _resident(b_mix.shape, (0, 0, 0)),
        ],
        out_specs=out_specs,
        out_shape=out_shape,
        scratch_shapes=[pltpu.VMEM((tm, d_b), F32)],
        compiler_params=_compiler_params(("parallel",)),
        name="proj_gmlp",
    )(h, w_rest, w_rest, w_rest, ln_g, ln_b, w_mix, b_mix)


def _cumsum_kernel(x_ref, o_ref, *, group):
    x = x_ref[...]
    lane = lax.broadcasted_iota(jnp.int32, x.shape, 1)
    sh = 1
    while sh < LANES:
        x = x + jnp.where(lane >= sh, pltpu.roll(x, sh, axis=1), 0.0)
        sh *= 2
    sub = lax.broadcasted_iota(jnp.int32, x.shape, 0) % group
    tot = jnp.broadcast_to(x[:, LANES - 1:LANES], x.shape)
    carry = jnp.where(sub >= 1, pltpu.roll(tot, 1, axis=0), 0.0)
    sh = 1
    while sh < group:
        carry = carry + jnp.where(sub >= sh, pltpu.roll(carry, sh, axis=0), 0.0)
        sh *= 2
    o_ref[...] = x + carry


def _cumsum_rows(x2, group):
    return pl.pallas_call(
        functools.partial(_cumsum_kernel, group=group),
        out_shape=jax.ShapeDtypeStruct(x2.shape, F32),
        name="cumsum_rows",
    )(x2)


def _online_softmax_step(s, v_blk, m, l, acc):
    m_new = jnp.maximum(m, jnp.max(s, axis=1, keepdims=True))
    alpha = jnp.exp2(m - m_new)
    p = jnp.exp2(s - m_new)
    l = alpha * l + jnp.sum(p, axis=1, keepdims=True)
    acc = alpha * acc + jnp.dot(p.astype(BF16), v_blk, preferred_element_type=F32)
    return m_new, l, acc


QUERY_GROUP = 256
N_BIAS_PARTS = 3
SUM_ROWS = 16


def _attn_prompt_kernel(q_ref, kb_ref, vt_ref, ct_ref, sga_ref, o_ref,
                        ka_ref, sa_ref, sb_ref, pa_ref, pb_ref, acc_ref, qa_ref):
    qi = pl.program_id(2)
    kv_sub = sa_ref.shape[0]
    blk = q_ref.shape[0]
    n_heads = q_ref.shape[1] // HEAD_DIM
    lane = lax.broadcasted_iota(jnp.int32, (LANES, LANES), 1)

    def head_cols(hh):
        return slice(hh * HEAD_DIM, (hh + 1) * HEAD_DIM)

    @pl.when(qi == 0)
    def _():
        for hh in range(n_heads):
            for sb in range(ct_ref.shape[2]):
                rows = slice(sb * LANES, (sb + 1) * LANES)
                rest = jnp.broadcast_to(ct_ref[hh, :, sb:sb + 1] * (-LOG2E), (LANES, LANES))
                extra = jnp.zeros((LANES, LANES), F32)
                for part in range(N_BIAS_PARTS):
                    piece = rest.astype(BF16).astype(F32)
                    extra = jnp.where(lane == part, piece, extra)
                    rest = rest - piece
                ka_ref[hh, rows, :] = jnp.concatenate(
                    [kb_ref[rows, head_cols(hh)], extra.astype(BF16)], axis=1)

    ones_feat = jnp.where(lax.broadcasted_iota(jnp.int32, (blk, LANES), 1) < N_BIAS_PARTS,
                          1.0, 0.0).astype(BF16)
    for hh in range(n_heads):
        qa_ref[hh * blk:(hh + 1) * blk, :] = jnp.concatenate(
            [q_ref[:, head_cols(hh)], ones_feat], axis=1)
    ones_rows = jnp.ones((SUM_ROWS, kv_sub), BF16)
    groups_per_head = blk // QUERY_GROUP
    n_groups = n_heads * groups_per_head
    group = [slice(t * QUERY_GROUP, (t + 1) * QUERY_GROUP) for t in range(n_groups)]
    head_of = [t // groups_per_head for t in range(n_groups)]
    order = [hh * groups_per_head + g for g in range(groups_per_head) for hh in range(n_heads)]

    def scores(c, g, n_keys=kv_sub):
        off = pl.multiple_of(c * kv_sub, kv_sub)
        return lax.dot_general(ka_ref[head_of[g], pl.ds(off, n_keys), :], qa_ref[group[g], :],
                               _NT, preferred_element_type=F32)

    def causal(s, first_key, first_query):
        key = first_key + lax.broadcasted_iota(jnp.int32, s.shape, 0)
        qry = first_query + lax.broadcasted_iota(jnp.int32, s.shape, 1)
        return jnp.where(key <= qry, s, -jnp.inf)

    def softmax(s, m):
        m_new = jnp.maximum(m, jnp.max(s, axis=0, keepdims=True))
        return m_new, jnp.exp2(m - m_new), jnp.exp2(s - m_new).astype(BF16)

    def accumulate(c, g, p, alpha):
        n_keys = p.shape[0]
        vt_sum = jnp.concatenate([vt_ref[c, head_cols(head_of[g]), :][:, :n_keys],
                                  ones_rows[:, :n_keys]], axis=0)
        acc_ref[:, group[g]] = (alpha * acc_ref[:, group[g]]
                                + jnp.dot(vt_sum, p, preferred_element_type=F32))

    for g in order:
        sa_ref[:, group[g]] = scores(0, g)
    pb_ref[...] = jnp.zeros(pb_ref.shape, BF16)
    acc_ref[...] = jnp.zeros(acc_ref.shape, F32)

    def body(j, carry):
        m, alpha_b = list(carry[0]), list(carry[1])
        alpha_a = [None] * n_groups
        c = 2 * j
        for g in order:
            sb_ref[:, group[g]] = scores(c + 1, g)
            accumulate(jnp.maximum(c - 1, 0), g, pb_ref[:, group[g]], alpha_b[g])
            m[g], alpha_a[g], pa_ref[:, group[g]] = softmax(sa_ref[:, group[g]], m[g])
        for g in order:
            sa_ref[:, group[g]] = scores(c + 2, g)
            accumulate(c, g, pa_ref[:, group[g]], alpha_a[g])
            m[g], alpha_b[g], pb_ref[:, group[g]] = softmax(sb_ref[:, group[g]], m[g])
        return tuple(m), tuple(alpha_b)

    init = (tuple(jnp.full((1, QUERY_GROUP), -jnp.inf, F32) for _ in range(n_groups)),
            tuple(jnp.ones((1, QUERY_GROUP), F32) for _ in range(n_groups)))
    m, alpha_b = lax.fori_loop(0, qi, body, init)

    c = 2 * qi
    m, alpha_a, alpha_hi = list(m), [None] * n_groups, [None] * n_groups
    first_query = [(g % groups_per_head) * QUERY_GROUP for g in range(n_groups)]
    keys_a = [min(fq + QUERY_GROUP, kv_sub) for fq in first_query]
    keys_b = [max(fq + QUERY_GROUP - kv_sub, 0) for fq in first_query]
    for g in order:
        if keys_b[g]:
            sb_ref[:keys_b[g], group[g]] = scores(c + 1, g, keys_b[g])
        accumulate(jnp.maximum(c - 1, 0), g, pb_ref[:, group[g]], alpha_b[g])
        s = causal(sa_ref[:keys_a[g], group[g]], 0, first_query[g])
        m[g], alpha_a[g], pa_ref[:keys_a[g], group[g]] = softmax(s, m[g])
    for g in order:
        accumulate(c, g, pa_ref[:keys_a[g], group[g]], alpha_a[g])
        if keys_b[g]:
            s = causal(sb_ref[:keys_b[g], group[g]], kv_sub, first_query[g])
            _, alpha_hi[g], pb_ref[:keys_b[g], group[g]] = softmax(s, m[g])
    for g in order:
        if keys_b[g]:
            accumulate(c + 1, g, pb_ref[:keys_b[g], group[g]], alpha_hi[g])
    for hh in range(n_heads):
        cols = slice(hh * blk, (hh + 1) * blk)
        out = (acc_ref[:HEAD_DIM, cols] / acc_ref[HEAD_DIM:HEAD_DIM + 1, cols]).T
        o_ref[:, head_cols(hh)] = (out * sga_ref[:, head_cols(hh)].astype(F32)).astype(BF16)


def _attn_prompt(q, kb, vt, cum_t, sga, *, batch, seq, blk, kv_sub, heads_per_step):
    n, d_a = q.shape
    n_heads = d_a // HEAD_DIM
    assert blk == 2 * kv_sub and seq % blk == 0 and n_heads % heads_per_step == 0
    nq = seq // blk
    n_chunks = seq // kv_sub
    width = heads_per_step * HEAD_DIM
    wide = heads_per_step * blk
    qmap = lambda b, h, i: (b * nq + i, h)
    return pl.pallas_call(
        _attn_prompt_kernel,
        grid=(batch, n_heads // heads_per_step, nq),
        in_specs=[
            pl.BlockSpec((blk, width), qmap),
            pl.BlockSpec((seq, width), lambda b, h, i: (b, h)),
            pl.BlockSpec((n_chunks, width, kv_sub), lambda b, h, i: (b, h, 0)),
            pl.BlockSpec((heads_per_step, None, LANES, seq // LANES),
                         lambda b, h, i: (h, b, 0, 0)),
            pl.BlockSpec((blk, width), qmap),
        ],
        out_specs=pl.BlockSpec((blk, width), qmap),
        out_shape=jax.ShapeDtypeStruct((n, d_a), BF16),
        scratch_shapes=[pltpu.VMEM((heads_per_step, seq, HEAD_DIM + LANES), BF16),
                        pltpu.VMEM((kv_sub, wide), F32),
                        pltpu.VMEM((kv_sub, wide), F32),
                        pltpu.VMEM((kv_sub, wide), BF16),
                        pltpu.VMEM((kv_sub, wide), BF16),
                        pltpu.VMEM((HEAD_DIM + SUM_ROWS, wide), F32),
                        pltpu.VMEM((wide, HEAD_DIM + LANES), BF16)],
        compiler_params=_compiler_params(("parallel", "parallel", "arbitrary")),
        name="attn_prompt",
    )(q, kb, vt, cum_t, sga)


def _attn_sample_kernel(q_ref, kn_ref, vn_ref, ck_ref, cv_ref, cp_ref, cn_ref, sga_ref, o_ref,
                        m_ref, l_ref, acc_ref, *, n_heads):
    c = pl.program_id(1)

    @pl.when(c == 0)
    def _():
        m_ref[...] = jnp.full(m_ref.shape, -jnp.inf, F32)
        l_ref[...] = jnp.zeros(l_ref.shape, F32)
        acc_ref[...] = jnp.zeros(acc_ref.shape, F32)

    def head_cols(h):
        return slice(h * HEAD_DIM, (h + 1) * HEAD_DIM)

    chunk = ck_ref.shape[0] // n_heads

    def cached_head(ref, h):
        return ref[pl.ds(h, chunk, stride=n_heads), :].astype(BF16)

    for h in range(n_heads):
        q = q_ref[:, head_cols(h)]
        s = lax.dot_general(q, cached_head(ck_ref, h), _NT, preferred_element_type=F32)
        s = s + cp_ref[h:h + 1, :] * (-LOG2E)
        m, l, acc = _online_softmax_step(s, cached_head(cv_ref, h),
                                         m_ref[h], l_ref[h], acc_ref[h])
        m_ref[h] = m
        l_ref[h] = l
        acc_ref[h] = acc

    @pl.when(c == pl.num_programs(1) - 1)
    def _():
        for h in range(n_heads):
            q = q_ref[:, head_cols(h)]
            s = lax.dot_general(q, kn_ref[:, head_cols(h)], _NT, preferred_element_type=F32)
            s = s + cn_ref[h:h + 1, :] * (-LOG2E)
            rows = lax.broadcasted_iota(jnp.int32, s.shape, 0)
            cols = lax.broadcasted_iota(jnp.int32, s.shape, 1)
            s = jnp.where(cols <= rows, s, -jnp.inf)
            _, l, acc = _online_softmax_step(s, vn_ref[:, head_cols(h)],
                                             m_ref[h], l_ref[h], acc_ref[h])
            o_ref[:, head_cols(h)] = (
                acc / l * sga_ref[:, head_cols(h)].astype(F32)).astype(BF16)


def _attn_sample(q, k_new, v_new, cache_k2, cache_v2, cum_past, cum_new, sga, *,
                 batch, t_new, past, chunk):
    n, d_a = q.shape
    n_heads = d_a // HEAD_DIM
    n_chunks = past // chunk
    new_map = lambda b, c: (b, 0)
    cache_map = lambda b, c: (b * n_chunks + c, 0)
    return pl.pallas_call(
        functools.partial(_attn_sample_kernel, n_heads=n_heads),
        grid=(batch, n_chunks),
        in_specs=[
            pl.BlockSpec((t_new, d_a), new_map),
            pl.BlockSpec((t_new, d_a), new_map),
            pl.BlockSpec((t_new, d_a), new_map),
            pl.BlockSpec((chunk * n_heads, HEAD_DIM), cache_map),
            pl.BlockSpec((chunk * n_heads, HEAD_DIM), cache_map),
            pl.BlockSpec((None, None, n_heads, chunk), lambda b, c: (b, c, 0, 0)),
            pl.BlockSpec((None, n_heads, t_new), lambda b, c: (b, 0, 0)),
            pl.BlockSpec((t_new, d_a), new_map),
        ],
        out_specs=pl.BlockSpec((t_new, d_a), new_map),
        out_shape=jax.ShapeDtypeStruct((n, d_a), BF16),
        scratch_shapes=[pltpu.VMEM((n_heads, t_new, 1), F32),
                        pltpu.VMEM((n_heads, t_new, 1), F32),
                        pltpu.VMEM((n_heads, t_new, HEAD_DIM), F32)],
        compiler_params=_compiler_params(("parallel", "arbitrary")),
        name="attn_sample",
    )(q, k_new, v_new, cache_k2, cache_v2, cum_past, cum_new, sga)


def _out_proj_kernel(oa_ref, ob_ref, w_ref, x_ref, fg_ref, y_ref, *, final_norm):
    o = jnp.concatenate([oa_ref[...], ob_ref[...]], axis=1)
    y = x_ref[...] + jnp.dot(o, w_ref[...], preferred_element_type=F32)
    if final_norm:
        y = _rmsnorm_rows(y, fg_ref[...])
    y_ref[...] = y


def _out_proj(out_a, out_b, w_out, x2, final_g, *, tm, final_norm):
    n, d = x2.shape
    d_a = out_a.shape[1]
    d_b = out_b.shape[1]
    row = lambda i: (i, 0)
    const = lambda i: (0, 0)
    return pl.pallas_call(
        functools.partial(_out_proj_kernel, final_norm=final_norm),
        grid=(n // tm,),
        in_specs=[
            pl.BlockSpec((tm, d_a), row),
            pl.BlockSpec((tm, d_b), row),
            pl.BlockSpec(w_out.shape, const),
            pl.BlockSpec((tm, d), row),
            pl.BlockSpec((1, d), const),
        ],
        out_specs=pl.BlockSpec((tm, d), row),
        out_shape=jax.ShapeDtypeStruct((n, d), F32),
        compiler_params=_compiler_params(("parallel",)),
        name="out_proj",
    )(out_a, out_b, w_out, x2, final_g)


def _pad_rows(a, rows):
    return jnp.pad(a, ((0, rows - a.shape[0]), (0, 0)))


def _mixing_weights(w_s, b_s, seq):
    c = min(seq, GMLP_CHUNK)
    reps = GMLP_CHUNK // c
    w = w_s[:, :c, :c] * jnp.tril(jnp.ones((c, c), w_s.dtype))
    if reps > 1:
        w = jnp.einsum('rs,gab->grasb', jnp.eye(reps, dtype=w.dtype), w)
        w = w.reshape(w_s.shape[0], GMLP_CHUNK, GMLP_CHUNK)
    b = jnp.tile(b_s[:, :c], (1, reps))
    b_full = jnp.broadcast_to(b[:, :, None], (b.shape[0], GMLP_CHUNK, LANES))
    return w.astype(BF16), b_full.astype(F32)


def _split_w_in_kernel(w_ref, qkv_ref, rest_ref, *, off_rest):
    w = w_ref[...]
    qkv_ref[...] = w[:, :qkv_ref.shape[1]].astype(BF16)
    rest_ref[...] = w[:, off_rest:].astype(BF16)


def _split_w_in(w_in, off_f, off_rest, *, tr=256):
    d, d_in = w_in.shape
    row = lambda i: (i, 0)
    return pl.pallas_call(
        functools.partial(_split_w_in_kernel, off_rest=off_rest),
        grid=(d // tr,),
        in_specs=[pl.BlockSpec((tr, d_in), row)],
        out_specs=[pl.BlockSpec((tr, off_f), row), pl.BlockSpec((tr, d_in - off_rest), row)],
        out_shape=[jax.ShapeDtypeStruct((d, off_f), BF16),
                   jax.ShapeDtypeStruct((d, d_in - off_rest), BF16)],
        compiler_params=_compiler_params(("parallel",)),
        name="split_w_in",
    )(w_in)


def _prepare_weights(params):
    norm_g, w_in, b_f, ln_g, ln_b, w_s, b_s, w_out = params
    n_heads = b_f.shape[0]
    off_f = 3 * n_heads * HEAD_DIM
    off_ga = off_f + n_heads
    w_qkv, w_rest = _split_w_in(w_in, off_f, off_ga)
    return dict(
        norm_g=norm_g[None], b_f=b_f[:, None], ln_g=ln_g[None], ln_b=ln_b[None],
        w_qkv=w_qkv, w_rest=w_rest,
        w_ft=_pad_rows(w_in[:, off_f:off_ga].T, 16).astype(BF16),
        w_out=w_out.astype(BF16), w_s=w_s, b_s=b_s)


def _layer(x, w, caches, final_g, *, final_norm, tm):
    batch, seq, d = x.shape
    n = batch * seq
    n_heads = w['b_f'].shape[0]
    d_b = w['ln_g'].shape[1]

    x2 = x.reshape(n, d)
    w_mix, b_mix = _mixing_weights(w['w_s'], w['b_s'], seq)

    blk, kv_sub = 1024, 512
    h, q, logft = _norm_q(x2, w['norm_g'], w['w_qkv'], w['w_ft'], w['b_f'], tm=tm)
    k, kb, v, vb, sga = _proj_kv(h, w['w_qkv'], w['w_rest'], n_heads=n_heads, tm=tm,
                                 v_sub=kv_sub if caches is None else None)
    gm = _proj_gmlp(h, w['w_rest'], w['ln_g'], w['ln_b'], w_mix, b_mix,
                    tm=tm, emit_vn=caches is not None)

    logf = logft.T.reshape(batch, seq, n_heads)
    if caches is None:
        out_b = gm[0]
        vn = None
        group = seq // LANES
        cum = _cumsum_rows(logft.reshape(n_heads * batch * group, LANES), group)
        cum_t = cum.reshape(n_heads, batch, group, LANES).transpose(0, 1, 3, 2)
        out_a = _attn_prompt(q, kb, vb, cum_t, sga, batch=batch, seq=seq, blk=blk, kv_sub=kv_sub,
                             heads_per_step=2)
    else:
        out_b, vn = gm
        cache_k, cache_v, cache_logf = caches
        past = cache_k.shape[1]
        chunk = 1024
        total = past + seq
        group = -(-total // LANES)
        lf_all = jnp.concatenate(
            [cache_logf.astype(F32).transpose(0, 2, 1), logf.transpose(0, 2, 1)], axis=2)
        lf_all = jnp.pad(lf_all, ((0, 0), (0, 0), (0, group * LANES - total)))
        cum = _cumsum_rows(lf_all.reshape(batch * n_heads * group, LANES), group)
        cum = cum.reshape(batch, n_heads, group * LANES)
        cum_past = cum[:, :, :past].reshape(batch, n_heads, past // chunk, chunk)
        cum_past = cum_past.transpose(0, 2, 1, 3)
        cum_new = cum[:, :, past:total]
        out_a = _attn_sample(q, kb, vb, cache_k.reshape(batch * past * n_heads, HEAD_DIM),
                             cache_v.reshape(batch * past * n_heads, HEAD_DIM),
                             cum_past, cum_new, sga,
                             batch=batch, t_new=seq, past=past, chunk=chunk)

    y2 = _out_proj(out_a, out_b, w['w_out'], x2, final_g[None], tm=tm, final_norm=final_norm)
    y = y2.reshape(batch, seq, d)
    k4 = k.reshape(batch, seq, n_heads, HEAD_DIM)
    v4 = v.reshape(batch, seq, n_heads, HEAD_DIM)
    vn3 = None if vn is None else vn.reshape(batch, seq, d_b)
    return y, k4, v4, logf, vn3


def kernel(x_prompt, x_sample, cache_k, cache_v, cache_logf, norm_g, w_in, b_f, ln_g, ln_b,
           w_s, b_s, w_out, final_g):
    depth = norm_g.shape[0]
    hp, hs = x_prompt, x_sample
    kp, vp, fp, ksm, vsm, fsm, gsm = [], [], [], [], [], [], []
    n_sample = x_sample.shape[0] * x_sample.shape[1]
    for l in range(depth):
        w = _prepare_weights(
            (norm_g[l], w_in[l], b_f[l], ln_g[l], ln_b[l], w_s[l], b_s[l], w_out[l]))
        last = l == depth - 1
        hp, k1, v1, f1, _ = _layer(hp, w, None, final_g, final_norm=last, tm=512)
        hs, k2, v2, f2, g2 = _layer(hs, w, (cache_k[l], cache_v[l], cache_logf[l]), final_g,
                                    final_norm=last, tm=n_sample)
        kp.append(k1); vp.append(v1); fp.append(f1)
        ksm.append(k2); vsm.append(v2); fsm.append(f2); gsm.append(g2)
    return (hp, hs, jnp.stack(kp), jnp.stack(vp), jnp.stack(fp),
            jnp.stack(ksm), jnp.stack(vsm), jnp.stack(fsm), jnp.stack(gsm))
```

```python
import functools
import math

import jax
import jax.numpy as jnp
from jax import lax
from jax.experimental import pallas as pl
from jax.experimental.pallas import tpu as pltpu

F32 = jnp.float32
BF16 = jnp.bfloat16

HEAD_DIM = 128
GMLP_CHUNK = 128
RMS_EPS = 1e-6
LN_EPS = 1e-5
LOG2E = math.log2(math.e)
LANES = 128
VMEM_LIMIT_BYTES = 48 * 1024 * 1024

_NT = (((1,), (1,)), ((), ()))


def _compiler_params(semantics):
    return pltpu.CompilerParams(dimension_semantics=semantics,
                                vmem_limit_bytes=VMEM_LIMIT_BYTES)


def _rmsnorm_rows(x, g):
    return x * lax.rsqrt(jnp.mean(x * x, axis=-1, keepdims=True) + RMS_EPS) * g


def _gelu_tanh(x):
    c = math.sqrt(2.0 / math.pi)
    return 0.5 * x * (1.0 + jnp.tanh(c * (x + 0.044715 * (x * x * x))))


def _silu(x):
    return x * jax.nn.sigmoid(x)


def _store_heads_on_sublanes(ref, z, n_heads):
    rows = z.shape[0]
    for h in range(n_heads):
        ref[pl.ds(h, rows, stride=n_heads), :] = z[:, h * HEAD_DIM:(h + 1) * HEAD_DIM]


def _project(h, wt_ref):
    return lax.dot_general(h, wt_ref[...], _NT, preferred_element_type=F32)


def _resident(shape, block_index):
    return pl.BlockSpec(shape, lambda i: block_index, pipeline_mode=pl.Buffered(1))


def _norm_q_kernel(x_ref, g_ref, wq_ref, wft_ref, bf_ref, h_ref, q_ref, logft_ref,
                   *, q_scale, n_heads):
    hb = _rmsnorm_rows(x_ref[...], g_ref[...]).astype(BF16)
    h_ref[...] = hb
    q_ref[...] = (_project(hb, wq_ref) * q_scale).astype(BF16)
    zf = lax.dot_general(wft_ref[...], hb, _NT, preferred_element_type=F32)
    t = zf[:n_heads] + bf_ref[...]
    logft_ref[...] = jnp.minimum(t, 0.0) - jnp.log1p(jnp.exp(-jnp.abs(t)))


def _norm_q(x2, norm_g, w_qkv, w_ft, b_f, *, tm):
    n, d = x2.shape
    n_heads = b_f.shape[0]
    d_a = n_heads * HEAD_DIM
    row = lambda i: (i, 0)
    return pl.pallas_call(
        functools.partial(_norm_q_kernel, q_scale=HEAD_DIM ** -0.5 * LOG2E, n_heads=n_heads),
        grid=(n // tm,),
        in_specs=[
            pl.BlockSpec((tm, d), row),
            _resident((1, d), (0, 0)),
            _resident((d_a, d), (0, 0)),
            _resident(w_ft.shape, (0, 0)),
            _resident((n_heads, 1), (0, 0)),
        ],
        out_specs=[
            pl.BlockSpec((tm, d), row),
            pl.BlockSpec((tm, d_a), row),
            pl.BlockSpec((n_heads, tm), lambda i: (0, i)),
        ],
        out_shape=[
            jax.ShapeDtypeStruct((n, d), BF16),
            jax.ShapeDtypeStruct((n, d_a), BF16),
            jax.ShapeDtypeStruct((n_heads, n), F32),
        ],
        compiler_params=_compiler_params(("parallel",)),
        name="norm_q",
    )(x2, norm_g, w_qkv, w_ft, b_f)


def _proj_kv_kernel(h_ref, wk_ref, wv_ref, wga_ref, k_ref, kb_ref, v_ref, vb_ref, sga_ref,
                    *, n_heads, v_sub):
    h = h_ref[...]
    z = _project(h, wk_ref)
    _store_heads_on_sublanes(k_ref, z, n_heads)
    kb_ref[...] = z.astype(BF16)
    z = _project(h, wv_ref)
    _store_heads_on_sublanes(v_ref, z, n_heads)
    if v_sub is None:
        vb_ref[...] = z.astype(BF16)
    else:
        for c in range(z.shape[0] // v_sub):
            vb_ref[c] = z[c * v_sub:(c + 1) * v_sub, :].T.astype(BF16)
    sga_ref[...] = _silu(_project(h, wga_ref)).astype(BF16)


def _proj_kv(h, w_qkv, w_rest, *, n_heads, tm, v_sub=None):
    n, d = h.shape
    d_a = n_heads * HEAD_DIM
    row = lambda i: (i, 0)
    if v_sub is None:
        vb_spec = pl.BlockSpec((tm, d_a), row)
        vb_shape = jax.ShapeDtypeStruct((n, d_a), BF16)
    else:
        vb_spec = pl.BlockSpec((tm // v_sub, d_a, v_sub), lambda i: (i, 0, 0))
        vb_shape = jax.ShapeDtypeStruct((n // v_sub, d_a, v_sub), BF16)
    return pl.pallas_call(
        functools.partial(_proj_kv_kernel, n_heads=n_heads, v_sub=v_sub),
        grid=(n // tm,),
        in_specs=[
            pl.BlockSpec((tm, d), row),
            _resident((d_a, d), (1, 0)),
            _resident((d_a, d), (2, 0)),
            _resident((d_a, d), (0, 0)),
        ],
        out_specs=[
            pl.BlockSpec((tm * n_heads, HEAD_DIM), row),
            pl.BlockSpec((tm, d_a), row),
            pl.BlockSpec((tm * n_heads, HEAD_DIM), row),
            vb_spec,
            pl.BlockSpec((tm, d_a), row),
        ],
        out_shape=[
            jax.ShapeDtypeStruct((n * n_heads, HEAD_DIM), F32),
            jax.ShapeDtypeStruct((n, d_a), BF16),
            jax.ShapeDtypeStruct((n * n_heads, HEAD_DIM), F32),
            vb_shape,
            jax.ShapeDtypeStruct((n, d_a), BF16),
        ],
        compiler_params=_compiler_params(("parallel",)),
        name="proj_kv",
    )(h, w_qkv, w_qkv, w_rest)


def _proj_gmlp_kernel(h_ref, wu_ref, wv_ref, wg_ref, lng_ref, lnb_ref, wmix_ref, bmix_ref,
                      ob_ref, *rest, n_groups, emit_vn):
    if emit_vn:
        vn_ref, mix_ref = rest
    else:
        (mix_ref,) = rest
    h = h_ref[...]
    tm = h.shape[0]
    n_chunks = tm // GMLP_CHUNK
    d_g = wv_ref.shape[0] // n_groups

    a = _gelu_tanh(_project(h, wv_ref))
    mu = jnp.mean(a, axis=-1, keepdims=True)
    ac = a - mu
    var = jnp.mean(ac * ac, axis=-1, keepdims=True)
    vn = ac * lax.rsqrt(var + LN_EPS) * lng_ref[...] + lnb_ref[...]
    if emit_vn:
        vn_ref[...] = vn
    vb = vn.astype(BF16)
    for g in range(n_groups):
        cols = slice(g * d_g, (g + 1) * d_g)
        rhs = jnp.concatenate(
            [vb[r * GMLP_CHUNK:(r + 1) * GMLP_CHUNK, cols] for r in range(n_chunks)], axis=1)
        mixed = jnp.dot(wmix_ref[g], rhs, preferred_element_type=F32)
        for r in range(n_chunks):
            mix_ref[r * GMLP_CHUNK:(r + 1) * GMLP_CHUNK, cols] = (
                mixed[:, r * d_g:(r + 1) * d_g] + bmix_ref[g])

    gated = _gelu_tanh(_project(h, wu_ref)) * mix_ref[...]
    gate = _silu(_project(h, wg_ref))
    ob_ref[...] = (gated * gate).astype(BF16)


def _proj_gmlp(h, w_rest, ln_g, ln_b, w_mix, b_mix, *, tm, emit_vn):
    n, d = h.shape
    d_b = ln_g.shape[1]
    n_groups = w_mix.shape[0]
    row = lambda i: (i, 0)
    out_specs = [pl.BlockSpec((tm, d_b), row)]
    out_shape = [jax.ShapeDtypeStruct((n, d_b), BF16)]
    if emit_vn:
        out_specs.append(pl.BlockSpec((tm, d_b), row))
        out_shape.append(jax.ShapeDtypeStruct((n, d_b), F32))
    return pl.pallas_call(
        functools.partial(_proj_gmlp_kernel, n_groups=n_groups, emit_vn=emit_vn),
        grid=(n // tm,),
        in_specs=[
            pl.BlockSpec((tm, d), row),
            _resident((d_b, d), (1, 0)),
            _resident((d_b, d), (2, 0)),
            _resident((d_b, d), (3, 0)),
            _resident((1, d_b), (0, 0)),
            _resident((1, d_b), (0, 0)),
            _resident(w_mix.shape, (0, 0, 0)),
            _resident(b_mix.shape, (0, 0, 0)),
        ],
        out_specs=out_specs,
        out_shape=out_shape,
        scratch_shapes=[pltpu.VMEM((tm, d_b), F32)],
        compiler_params=_compiler_params(("parallel",)),
        name="proj_gmlp",
    )(h, w_rest, w_rest, w_rest, ln_g, ln_b, w_mix, b_mix)


def _cumsum_kernel(x_ref, o_ref, *, group):
    x = x_ref[...]
    lane = lax.broadcasted_iota(jnp.int32, x.shape, 1)
    sh = 1
    while sh < LANES:
        x = x + jnp.where(lane >= sh, pltpu.roll(x, sh, axis=1), 0.0)
        sh *= 2
    sub = lax.broadcasted_iota(jnp.int32, x.shape, 0) % group
    tot = jnp.broadcast_to(x[:, LANES - 1:LANES], x.shape)
    carry = jnp.where(sub >= 1, pltpu.roll(tot, 1, axis=0), 0.0)
    sh = 1
    while sh < group:
        carry = carry + jnp.where(sub >= sh, pltpu.roll(carry, sh, axis=0), 0.0)
        sh *= 2
    o_ref[...] = x + carry


def _cumsum_rows(x2, group):
    return pl.pallas_call(
        functools.partial(_cumsum_kernel, group=group),
        out_shape=jax.ShapeDtypeStruct(x2.shape, F32),
        name="cumsum_rows",
    )(x2)


def _online_softmax_step(s, v_blk, m, l, acc):
    m_new = jnp.maximum(m, jnp.max(s, axis=1, keepdims=True))
    alpha = jnp.exp2(m - m_new)
    p = jnp.exp2(s - m_new)
    l = alpha * l + jnp.sum(p, axis=1, keepdims=True)
    acc = alpha * acc + jnp.dot(p.astype(BF16), v_blk, preferred_element_type=F32)
    return m_new, l, acc


QUERY_GROUP = 256
N_BIAS_PARTS = 3
SUM_ROWS = 16


def _attn_prompt_kernel(q_ref, kb_ref, vt_ref, ct_ref, sga_ref, o_ref,
                        ka_ref, sa_ref, sb_ref, pa_ref, pb_ref, acc_ref, qa_ref):
    qi = pl.program_id(2)
    kv_sub = sa_ref.shape[0]
    blk = q_ref.shape[0]
    n_heads = q_ref.shape[1] // HEAD_DIM
    lane = lax.broadcasted_iota(jnp.int32, (LANES, LANES), 1)

    def head_cols(hh):
        return slice(hh * HEAD_DIM, (hh + 1) * HEAD_DIM)

    @pl.when(qi == 0)
    def _():
        for hh in range(n_heads):
            for sb in range(ct_ref.shape[2]):
                rows = slice(sb * LANES, (sb + 1) * LANES)
                rest = jnp.broadcast_to(ct_ref[hh, :, sb:sb + 1] * (-LOG2E), (LANES, LANES))
                extra = jnp.zeros((LANES, LANES), F32)
                for part in range(N_BIAS_PARTS):
                    piece = rest.astype(BF16).astype(F32)
                    extra = jnp.where(lane == part, piece, extra)
                    rest = rest - piece
                ka_ref[hh, rows, :] = jnp.concatenate(
                    [kb_ref[rows, head_cols(hh)], extra.astype(BF16)], axis=1)

    ones_feat = jnp.where(lax.broadcasted_iota(jnp.int32, (blk, LANES), 1) < N_BIAS_PARTS,
                          1.0, 0.0).astype(BF16)
    for hh in range(n_heads):
        qa_ref[hh * blk:(hh + 1) * blk, :] = jnp.concatenate(
            [q_ref[:, head_cols(hh)], ones_feat], axis=1)
    ones_rows = jnp.ones((SUM_ROWS, kv_sub), BF16)
    groups_per_head = blk // QUERY_GROUP
    n_groups = n_heads * groups_per_head
    group = [slice(t * QUERY_GROUP, (t + 1) * QUERY_GROUP) for t in range(n_groups)]
    head_of = [t // groups_per_head for t in range(n_groups)]
    order = [hh * groups_per_head + g for g in range(groups_per_head) for hh in range(n_heads)]

    def scores(c, g, n_keys=kv_sub):
        off = pl.multiple_of(c * kv_sub, kv_sub)
        return lax.dot_general(ka_ref[head_of[g], pl.ds(off, n_keys), :], qa_ref[group[g], :],
                               _NT, preferred_element_type=F32)

    own_tile = (QUERY_GROUP, QUERY_GROUP)
    not_after = (lax.broadcasted_iota(jnp.int32, own_tile, 0)
                 <= lax.broadcasted_iota(jnp.int32, own_tile, 1))

    def causal_tail(s):
        n_before = s.shape[0] - QUERY_GROUP
        tail = jnp.where(not_after, s[n_before:], -jnp.inf)
        return tail if n_before == 0 else jnp.concatenate([s[:n_before], tail], axis=0)

    def softmax(s, m):
        m_new = jnp.maximum(m, jnp.max(s, axis=0, keepdims=True))
        return m_new, jnp.exp2(m - m_new), jnp.exp2(s - m_new).astype(BF16)

    def accumulate(c, g, p, alpha):
        n_keys = p.shape[0]
        vt_sum = jnp.concatenate([vt_ref[c, head_cols(head_of[g]), :][:, :n_keys],
                                  ones_rows[:, :n_keys]], axis=0)
        acc_ref[:, group[g]] = (alpha * acc_ref[:, group[g]]
                                + jnp.dot(vt_sum, p, preferred_element_type=F32))

    for g in order:
        sa_ref[:, group[g]] = scores(0, g)
    pb_ref[...] = jnp.zeros(pb_ref.shape, BF16)
    acc_ref[...] = jnp.zeros(acc_ref.shape, F32)

    def body(j, carry):
        m, alpha_b = list(carry[0]), list(carry[1])
        alpha_a = [None] * n_groups
        c = 2 * j
        for g in order:
            sb_ref[:, group[g]] = scores(c + 1, g)
            accumulate(jnp.maximum(c - 1, 0), g, pb_ref[:, group[g]], alpha_b[g])
            m[g], alpha_a[g], pa_ref[:, group[g]] = softmax(sa_ref[:, group[g]], m[g])
        for g in order:
            sa_ref[:, group[g]] = scores(c + 2, g)
            accumulate(c, g, pa_ref[:, group[g]], alpha_a[g])
            m[g], alpha_b[g], pb_ref[:, group[g]] = softmax(sb_ref[:, group[g]], m[g])
        return tuple(m), tuple(alpha_b)

    init = (tuple(jnp.full((1, QUERY_GROUP), -jnp.inf, F32) for _ in range(n_groups)),
            tuple(jnp.ones((1, QUERY_GROUP), F32) for _ in range(n_groups)))
    m, alpha_b = lax.fori_loop(0, qi, body, init)

    c = 2 * qi
    m, alpha_a, alpha_hi = list(m), [None] * n_groups, [None] * n_groups
    first_query = [(g % groups_per_head) * QUERY_GROUP for g in range(n_groups)]
    keys_a = [min(fq + QUERY_GROUP, kv_sub) for fq in first_query]
    keys_b = [max(fq + QUERY_GROUP - kv_sub, 0) for fq in first_query]
    for g in order:
        if keys_b[g]:
            sb_ref[:keys_b[g], group[g]] = scores(c + 1, g, keys_b[g])
        accumulate(jnp.maximum(c - 1, 0), g, pb_ref[:, group[g]], alpha_b[g])
        s = sa_ref[:keys_a[g], group[g]]
        if not keys_b[g]:
            s = causal_tail(s)
        m[g], alpha_a[g], pa_ref[:keys_a[g], group[g]] = softmax(s, m[g])
    for g in order:
        accumulate(c, g, pa_ref[:keys_a[g], group[g]], alpha_a[g])
        if keys_b[g]:
            s = causal_tail(sb_ref[:keys_b[g], group[g]])
            _, alpha_hi[g], pb_ref[:keys_b[g], group[g]] = softmax(s, m[g])
    for g in order:
        if keys_b[g]:
            accumulate(c + 1, g, pb_ref[:keys_b[g], group[g]], alpha_hi[g])
    for hh in range(n_heads):
        cols = slice(hh * blk, (hh + 1) * blk)
        out = (acc_ref[:HEAD_DIM, cols] / acc_ref[HEAD_DIM:HEAD_DIM + 1, cols]).T
        o_ref[:, head_cols(hh)] = (out * sga_ref[:, head_cols(hh)].astype(F32)).astype(BF16)


def _attn_prompt(q, kb, vt, cum_t, sga, *, batch, seq, blk, kv_sub, heads_per_step):
    n, d_a = q.shape
    n_heads = d_a // HEAD_DIM
    assert blk == 2 * kv_sub and seq % blk == 0 and n_heads % heads_per_step == 0
    nq = seq // blk
    n_chunks = seq // kv_sub
    width = heads_per_step * HEAD_DIM
    wide = heads_per_step * blk
    qmap = lambda b, h, i: (b * nq + i, h)
    return pl.pallas_call(
        _attn_prompt_kernel,
        grid=(batch, n_heads // heads_per_step, nq),
        in_specs=[
            pl.BlockSpec((blk, width), qmap),
            pl.BlockSpec((seq, width), lambda b, h, i: (b, h)),
            pl.BlockSpec((n_chunks, width, kv_sub), lambda b, h, i: (b, h, 0)),
            pl.BlockSpec((heads_per_step, None, LANES, seq // LANES),
                         lambda b, h, i: (h, b, 0, 0)),
            pl.BlockSpec((blk, width), qmap),
        ],
        out_specs=pl.BlockSpec((blk, width), qmap),
        out_shape=jax.ShapeDtypeStruct((n, d_a), BF16),
        scratch_shapes=[pltpu.VMEM((heads_per_step, seq, HEAD_DIM + LANES), BF16),
                        pltpu.VMEM((kv_sub, wide), F32),
                        pltpu.VMEM((kv_sub, wide), F32),
                        pltpu.VMEM((kv_sub, wide), BF16),
                        pltpu.VMEM((kv_sub, wide), BF16),
                        pltpu.VMEM((HEAD_DIM + SUM_ROWS, wide), F32),
                        pltpu.VMEM((wide, HEAD_DIM + LANES), BF16)],
        compiler_params=_compiler_params(("parallel", "parallel", "arbitrary")),
        name="attn_prompt",
    )(q, kb, vt, cum_t, sga)


def _attn_sample_kernel(q_ref, kn_ref, vn_ref, ck_ref, cv_ref, cp_ref, cn_ref, sga_ref, o_ref,
                        m_ref, l_ref, acc_ref, *, n_heads):
    c = pl.program_id(1)

    @pl.when(c == 0)
    def _():
        m_ref[...] = jnp.full(m_ref.shape, -jnp.inf, F32)
        l_ref[...] = jnp.zeros(l_ref.shape, F32)
        acc_ref[...] = jnp.zeros(acc_ref.shape, F32)

    def head_cols(h):
        return slice(h * HEAD_DIM, (h + 1) * HEAD_DIM)

    chunk = ck_ref.shape[0] // n_heads

    def cached_head(ref, h):
        return ref[pl.ds(h, chunk, stride=n_heads), :].astype(BF16)

    heads = range(n_heads)
    s = [lax.dot_general(q_ref[:, head_cols(h)], cached_head(ck_ref, h), _NT,
                         preferred_element_type=F32) + cp_ref[h:h + 1, :] * (-LOG2E)
         for h in heads]
    m_new = [jnp.maximum(m_ref[h], jnp.max(s[h], axis=1, keepdims=True)) for h in heads]
    p = [jnp.exp2(s[h] - m_new[h]) for h in heads]
    pv = [jnp.dot(p[h].astype(BF16), cached_head(cv_ref, h), preferred_element_type=F32)
          for h in heads]
    for h in heads:
        alpha = jnp.exp2(m_ref[h] - m_new[h])
        l_ref[h] = alpha * l_ref[h] + jnp.sum(p[h], axis=1, keepdims=True)
        acc_ref[h] = alpha * acc_ref[h] + pv[h]
        m_ref[h] = m_new[h]

    @pl.when(c == pl.num_programs(1) - 1)
    def _():
        for h in range(n_heads):
            q = q_ref[:, head_cols(h)]
            s = lax.dot_general(q, kn_ref[:, head_cols(h)], _NT, preferred_element_type=F32)
            s = s + cn_ref[h:h + 1, :] * (-LOG2E)
            rows = lax.broadcasted_iota(jnp.int32, s.shape, 0)
            cols = lax.broadcasted_iota(jnp.int32, s.shape, 1)
            s = jnp.where(cols <= rows, s, -jnp.inf)
            _, l, acc = _online_softmax_step(s, vn_ref[:, head_cols(h)],
                                             m_ref[h], l_ref[h], acc_ref[h])
            o_ref[:, head_cols(h)] = (
                acc / l * sga_ref[:, head_cols(h)].astype(F32)).astype(BF16)


def _attn_sample(q, k_new, v_new, cache_k2, cache_v2, cum_past, cum_new, sga, *,
                 batch, t_new, past, chunk):
    n, d_a = q.shape
    n_heads = d_a // HEAD_DIM
    n_chunks = past // chunk
    new_map = lambda b, c: (b, 0)
    cache_map = lambda b, c: (b * n_chunks + c, 0)
    return pl.pallas_call(
        functools.partial(_attn_sample_kernel, n_heads=n_heads),
        grid=(batch, n_chunks),
        in_specs=[
            pl.BlockSpec((t_new, d_a), new_map),
            pl.BlockSpec((t_new, d_a), new_map),
            pl.BlockSpec((t_new, d_a), new_map),
            pl.BlockSpec((chunk * n_heads, HEAD_DIM), cache_map),
            pl.BlockSpec((chunk * n_heads, HEAD_DIM), cache_map),
            pl.BlockSpec((None, None, n_heads, chunk), lambda b, c: (b, c, 0, 0)),
            pl.BlockSpec((None, n_heads, t_new), lambda b, c: (b, 0, 0)),
            pl.BlockSpec((t_new, d_a), new_map),
        ],
        out_specs=pl.BlockSpec((t_new, d_a), new_map),
        out_shape=jax.ShapeDtypeStruct((n, d_a), BF16),
        scratch_shapes=[pltpu.VMEM((n_heads, t_new, 1), F32),
                        pltpu.VMEM((n_heads, t_new, 1), F32),
                        pltpu.VMEM((n_heads, t_new, HEAD_DIM), F32)],
        compiler_params=_compiler_params(("parallel", "arbitrary")),
        name="attn_sample",
    )(q, k_new, v_new, cache_k2, cache_v2, cum_past, cum_new, sga)


def _out_proj_kernel(oa_ref, ob_ref, w_ref, x_ref, fg_ref, y_ref, *, final_norm):
    o = jnp.concatenate([oa_ref[...], ob_ref[...]], axis=1)
    y = x_ref[...] + jnp.dot(o, w_ref[...], preferred_element_type=F32)
    if final_norm:
        y = _rmsnorm_rows(y, fg_ref[...])
    y_ref[...] = y


def _out_proj(out_a, out_b, w_out, x2, final_g, *, tm, final_norm):
    n, d = x2.shape
    d_a = out_a.shape[1]
    d_b = out_b.shape[1]
    row = lambda i: (i, 0)
    const = lambda i: (0, 0)
    return pl.pallas_call(
        functools.partial(_out_proj_kernel, final_norm=final_norm),
        grid=(n // tm,),
        in_specs=[
            pl.BlockSpec((tm, d_a), row),
            pl.BlockSpec((tm, d_b), row),
            pl.BlockSpec(w_out.shape, const),
            pl.BlockSpec((tm, d), row),
            pl.BlockSpec((1, d), const),
        ],
        out_specs=pl.BlockSpec((tm, d), row),
        out_shape=jax.ShapeDtypeStruct((n, d), F32),
        compiler_params=_compiler_params(("parallel",)),
        name="out_proj",
    )(out_a, out_b, w_out, x2, final_g)


def _pad_rows(a, rows):
    return jnp.pad(a, ((0, rows - a.shape[0]), (0, 0)))


def _mixing_weights(w_s, b_s, seq):
    c = min(seq, GMLP_CHUNK)
    reps = GMLP_CHUNK // c
    w = w_s[:, :c, :c] * jnp.tril(jnp.ones((c, c), w_s.dtype))
    if reps > 1:
        w = jnp.einsum('rs,gab->grasb', jnp.eye(reps, dtype=w.dtype), w)
        w = w.reshape(w_s.shape[0], GMLP_CHUNK, GMLP_CHUNK)
    b = jnp.tile(b_s[:, :c], (1, reps))
    b_full = jnp.broadcast_to(b[:, :, None], (b.shape[0], GMLP_CHUNK, LANES))
    return w.astype(BF16), b_full.astype(F32)


def _split_w_in_kernel(wt_ref, qkv_ref, rest_ref, *, off_rest):
    wt = wt_ref[...]
    qkv_ref[...] = wt[:qkv_ref.shape[0], :].astype(BF16)
    rest_ref[...] = wt[off_rest:, :].astype(BF16)


def _split_w_in(w_in_t, off_f, off_rest, *, tc=256):
    d_in, d = w_in_t.shape
    col = lambda i: (0, i)
    return pl.pallas_call(
        functools.partial(_split_w_in_kernel, off_rest=off_rest),
        grid=(d // tc,),
        in_specs=[pl.BlockSpec((d_in, tc), col)],
        out_specs=[pl.BlockSpec((off_f, tc), col), pl.BlockSpec((d_in - off_rest, tc), col)],
        out_shape=[jax.ShapeDtypeStruct((off_f, d), BF16),
                   jax.ShapeDtypeStruct((d_in - off_rest, d), BF16)],
        compiler_params=_compiler_params(("parallel",)),
        name="split_w_in",
    )(w_in_t)


def _prepare_weights(params):
    norm_g, w_in, b_f, ln_g, ln_b, w_s, b_s, w_out = params
    n_heads = b_f.shape[0]
    off_f = 3 * n_heads * HEAD_DIM
    off_ga = off_f + n_heads
    w_in_t = w_in.T
    w_qkv, w_rest = _split_w_in(w_in_t, off_f, off_ga)
    return dict(
        norm_g=norm_g[None], b_f=b_f[:, None], ln_g=ln_g[None], ln_b=ln_b[None],
        w_qkv=w_qkv, w_rest=w_rest,
        w_ft=_pad_rows(w_in_t[off_f:off_ga], 16).astype(BF16),
        w_out=w_out.astype(BF16), w_s=w_s, b_s=b_s)


def _layer(x, w, caches, final_g, *, final_norm, tm):
    batch, seq, d = x.shape
    n = batch * seq
    n_heads = w['b_f'].shape[0]
    d_b = w['ln_g'].shape[1]

    x2 = x.reshape(n, d)
    w_mix, b_mix = _mixing_weights(w['w_s'], w['b_s'], seq)

    blk, kv_sub = 1024, 512
    h, q, logft = _norm_q(x2, w['norm_g'], w['w_qkv'], w['w_ft'], w['b_f'], tm=tm)
    k, kb, v, vb, sga = _proj_kv(h, w['w_qkv'], w['w_rest'], n_heads=n_heads, tm=tm,
                                 v_sub=kv_sub if caches is None else None)
    gm = _proj_gmlp(h, w['w_rest'], w['ln_g'], w['ln_b'], w_mix, b_mix,
                    tm=tm, emit_vn=caches is not None)

    logf = logft.T.reshape(batch, seq, n_heads)
    if caches is None:
        out_b = gm[0]
        vn = None
        group = seq // LANES
        cum = _cumsum_rows(logft.reshape(n_heads * batch * group, LANES), group)
        cum_t = cum.reshape(n_heads, batch, group, LANES).transpose(0, 1, 3, 2)
        out_a = _attn_prompt(q, kb, vb, cum_t, sga, batch=batch, seq=seq, blk=blk, kv_sub=kv_sub,
                             heads_per_step=2)
    else:
        out_b, vn = gm
        cache_k, cache_v, cache_logf = caches
        past = cache_k.shape[1]
        chunk = 2048
        total = past + seq
        group = -(-total // LANES)
        lf_all = jnp.concatenate(
            [cache_logf.astype(F32).transpose(0, 2, 1), logf.transpose(0, 2, 1)], axis=2)
        lf_all = jnp.pad(lf_all, ((0, 0), (0, 0), (0, group * LANES - total)))
        cum = _cumsum_rows(lf_all.reshape(batch * n_heads * group, LANES), group)
        cum = cum.reshape(batch, n_heads, group * LANES)
        cum_past = cum[:, :, :past].reshape(batch, n_heads, past // chunk, chunk)
        cum_past = cum_past.transpose(0, 2, 1, 3)
        cum_new = cum[:, :, past:total]
        out_a = _attn_sample(q, kb, vb, cache_k.reshape(batch * past * n_heads, HEAD_DIM),
                             cache_v.reshape(batch * past * n_heads, HEAD_DIM),
                             cum_past, cum_new, sga,
                             batch=batch, t_new=seq, past=past, chunk=chunk)

    y2 = _out_proj(out_a, out_b, w['w_out'], x2, final_g[None], tm=tm, final_norm=final_norm)
    y = y2.reshape(batch, seq, d)
    k4 = k.reshape(batch, seq, n_heads, HEAD_DIM)
    v4 = v.reshape(batch, seq, n_heads, HEAD_DIM)
    vn3 = None if vn is None else vn.reshape(batch, seq, d_b)
    return y, k4, v4, logf, vn3


def kernel(x_prompt, x_sample, cache_k, cache_v, cache_logf, norm_g, w_in, b_f, ln_g, ln_b,
           w_s, b_s, w_out, final_g):
    depth = norm_g.shape[0]
    hp, hs = x_prompt, x_sample
    kp, vp, fp, ksm, vsm, fsm, gsm = [], [], [], [], [], [], []
    n_sample = x_sample.shape[0] * x_sample.shape[1]
    for l in range(depth):
        w = _prepare_weights(
            (norm_g[l], w_in[l], b_f[l], ln_g[l], ln_b[l], w_s[l], b_s[l], w_out[l]))
        last = l == depth - 1
        hp, k1, v1, f1, _ = _layer(hp, w, None, final_g, final_norm=last, tm=512)
        hs, k2, v2, f2, g2 = _layer(hs, w, (cache_k[l], cache_v[l], cache_logf[l]), final_g,
                                    final_norm=last, tm=n_sample)
        kp.append(k1); vp.append(v1); fp.append(f1)
        ksm.append(k2); vsm.append(v2); fsm.append(f2); gsm.append(g2)
    return (hp, hs, jnp.stack(kp), jnp.stack(vp), jnp.stack(fp),
            jnp.stack(ksm), jnp.stack(vsm), jnp.stack(fsm), jnp.stack(gsm))
```

```python
import functools
import math

import jax
import jax.numpy as jnp
from jax import lax
from jax.experimental import pallas as pl
from jax.experimental.pallas import tpu as pltpu

F32 = jnp.float32
BF16 = jnp.bfloat16

HEAD_DIM = 128
GMLP_CHUNK = 128
RMS_EPS = 1e-6
LN_EPS = 1e-5
LOG2E = math.log2(math.e)
LANES = 128
VMEM_LIMIT_BYTES = 48 * 1024 * 1024

_NT = (((1,), (1,)), ((), ()))


def _compiler_params(semantics):
    return pltpu.CompilerParams(dimension_semantics=semantics,
                                vmem_limit_bytes=VMEM_LIMIT_BYTES)


def _rmsnorm_rows(x, g):
    return x * lax.rsqrt(jnp.mean(x * x, axis=-1, keepdims=True) + RMS_EPS) * g


def _gelu_tanh(x):
    c = math.sqrt(2.0 / math.pi)
    return 0.5 * x * (1.0 + jnp.tanh(c * (x + 0.044715 * (x * x * x))))


def _silu(x):
    return x * jax.nn.sigmoid(x)


def _store_heads_on_sublanes(ref, z, n_heads):
    rows = z.shape[0]
    for h in range(n_heads):
        ref[pl.ds(h, rows, stride=n_heads), :] = z[:, h * HEAD_DIM:(h + 1) * HEAD_DIM]


def _project(h, w_ref):
    return jnp.dot(h, w_ref[...], preferred_element_type=F32)


def _resident(shape, block_index):
    return pl.BlockSpec(shape, lambda i: block_index, pipeline_mode=pl.Buffered(1))


def _norm_q_kernel(x_ref, g_ref, wq_ref, wft_ref, bf_ref, h_ref, q_ref, logft_ref,
                   *, q_scale, n_heads):
    hb = _rmsnorm_rows(x_ref[...], g_ref[...]).astype(BF16)
    h_ref[...] = hb
    q_ref[...] = (_project(hb, wq_ref) * q_scale).astype(BF16)
    zf = lax.dot_general(wft_ref[...], hb, _NT, preferred_element_type=F32)
    t = zf[:n_heads] + bf_ref[...]
    logft_ref[...] = jnp.minimum(t, 0.0) - jnp.log1p(jnp.exp(-jnp.abs(t)))


def _norm_q(x2, norm_g, w_qkv, w_ft, b_f, *, tm):
    n, d = x2.shape
    n_heads = b_f.shape[0]
    d_a = n_heads * HEAD_DIM
    row = lambda i: (i, 0)
    return pl.pallas_call(
        functools.partial(_norm_q_kernel, q_scale=HEAD_DIM ** -0.5 * LOG2E, n_heads=n_heads),
        grid=(n // tm,),
        in_specs=[
            pl.BlockSpec((tm, d), row),
            _resident((1, d), (0, 0)),
            _resident((d, d_a), (0, 0)),
            _resident(w_ft.shape, (0, 0)),
            _resident((n_heads, 1), (0, 0)),
        ],
        out_specs=[
            pl.BlockSpec((tm, d), row),
            pl.BlockSpec((tm, d_a), row),
            pl.BlockSpec((n_heads, tm), lambda i: (0, i)),
        ],
        out_shape=[
            jax.ShapeDtypeStruct((n, d), BF16),
            jax.ShapeDtypeStruct((n, d_a), BF16),
            jax.ShapeDtypeStruct((n_heads, n), F32),
        ],
        compiler_params=_compiler_params(("parallel",)),
        name="norm_q",
    )(x2, norm_g, w_qkv, w_ft, b_f)


def _proj_kv_kernel(h_ref, wk_ref, wv_ref, wga_ref, k_ref, kb_ref, v_ref, vb_ref, sga_ref,
                    *, n_heads, v_sub):
    h = h_ref[...]
    z = _project(h, wk_ref)
    _store_heads_on_sublanes(k_ref, z, n_heads)
    kb_ref[...] = z.astype(BF16)
    z = _project(h, wv_ref)
    _store_heads_on_sublanes(v_ref, z, n_heads)
    if v_sub is None:
        vb_ref[...] = z.astype(BF16)
    else:
        for c in range(z.shape[0] // v_sub):
            vb_ref[c] = z[c * v_sub:(c + 1) * v_sub, :].T.astype(BF16)
    sga_ref[...] = _silu(_project(h, wga_ref)).astype(BF16)


def _proj_kv(h, w_qkv, w_rest, *, n_heads, tm, v_sub=None):
    n, d = h.shape
    d_a = n_heads * HEAD_DIM
    row = lambda i: (i, 0)
    if v_sub is None:
        vb_spec = pl.BlockSpec((tm, d_a), row)
        vb_shape = jax.ShapeDtypeStruct((n, d_a), BF16)
    else:
        vb_spec = pl.BlockSpec((tm // v_sub, d_a, v_sub), lambda i: (i, 0, 0))
        vb_shape = jax.ShapeDtypeStruct((n // v_sub, d_a, v_sub), BF16)
    return pl.pallas_call(
        functools.partial(_proj_kv_kernel, n_heads=n_heads, v_sub=v_sub),
        grid=(n // tm,),
        in_specs=[
            pl.BlockSpec((tm, d), row),
            _resident((d, d_a), (0, 1)),
            _resident((d, d_a), (0, 2)),
            _resident((d, d_a), (0, 0)),
        ],
        out_specs=[
            pl.BlockSpec((tm * n_heads, HEAD_DIM), row),
            pl.BlockSpec((tm, d_a), row),
            pl.BlockSpec((tm * n_heads, HEAD_DIM), row),
            vb_spec,
            pl.BlockSpec((tm, d_a), row),
        ],
        out_shape=[
            jax.ShapeDtypeStruct((n * n_heads, HEAD_DIM), F32),
            jax.ShapeDtypeStruct((n, d_a), BF16),
            jax.ShapeDtypeStruct((n * n_heads, HEAD_DIM), F32),
            vb_shape,
            jax.ShapeDtypeStruct((n, d_a), BF16),
        ],
        compiler_params=_compiler_params(("parallel",)),
        name="proj_kv",
    )(h, w_qkv, w_qkv, w_rest)


def _proj_gmlp_kernel(h_ref, wu_ref, wv_ref, wg_ref, lng_ref, lnb_ref, wmix_ref, bmix_ref,
                      ob_ref, *rest, n_groups, emit_vn):
    if emit_vn:
        vn_ref, mix_ref, gu_ref = rest
    else:
        mix_ref, gu_ref = rest
    h = h_ref[...]
    tm = h.shape[0]
    n_chunks = tm // GMLP_CHUNK
    d_g = wv_ref.shape[1] // n_groups

    a = _gelu_tanh(_project(h, wv_ref))
    gu_ref[...] = _gelu_tanh(_project(h, wu_ref))
    mu = jnp.mean(a, axis=-1, keepdims=True)
    ac = a - mu
    var = jnp.mean(ac * ac, axis=-1, keepdims=True)
    vn = ac * lax.rsqrt(var + LN_EPS) * lng_ref[...] + lnb_ref[...]
    if emit_vn:
        vn_ref[...] = vn
    vb = vn.astype(BF16)
    for g in range(n_groups):
        cols = slice(g * d_g, (g + 1) * d_g)
        rhs = jnp.concatenate(
            [vb[r * GMLP_CHUNK:(r + 1) * GMLP_CHUNK, cols] for r in range(n_chunks)], axis=1)
        mixed = jnp.dot(wmix_ref[g], rhs, preferred_element_type=F32)
        for r in range(n_chunks):
            mix_ref[r * GMLP_CHUNK:(r + 1) * GMLP_CHUNK, cols] = (
                mixed[:, r * d_g:(r + 1) * d_g] + bmix_ref[g])

    gate = _silu(_project(h, wg_ref))
    ob_ref[...] = (gu_ref[...] * mix_ref[...] * gate).astype(BF16)


def _proj_gmlp(h, w_rest, ln_g, ln_b, w_mix, b_mix, *, tm, emit_vn):
    n, d = h.shape
    d_b = ln_g.shape[1]
    n_groups = w_mix.shape[0]
    row = lambda i: (i, 0)
    out_specs = [pl.BlockSpec((tm, d_b), row)]
    out_shape = [jax.ShapeDtypeStruct((n, d_b), BF16)]
    if emit_vn:
        out_specs.append(pl.BlockSpec((tm, d_b), row))
        out_shape.append(jax.ShapeDtypeStruct((n, d_b), F32))
    return pl.pallas_call(
        functools.partial(_proj_gmlp_kernel, n_groups=n_groups, emit_vn=emit_vn),
        grid=(n // tm,),
        in_specs=[
            pl.BlockSpec((tm, d), row),
            _resident((d, d_b), (0, 1)),
            _resident((d, d_b), (0, 2)),
            _resident((d, d_b), (0, 3)),
            _resident((1, d_b), (0, 0)),
            _resident((1, d_b), (0, 0)),
            _resident(w_mix.shape, (0, 0, 0)),
            _resident(b_mix.shape, (0, 0, 0)),
        ],
        out_specs=out_specs,
        out_shape=out_shape,
        scratch_shapes=[pltpu.VMEM((tm, d_b), F32), pltpu.VMEM((tm, d_b), F32)],
        compiler_params=_compiler_params(("parallel",)),
        name="proj_gmlp",
    )(h, w_rest, w_rest, w_rest, ln_g, ln_b, w_mix, b_mix)


def _cumsum_kernel(x_ref, o_ref, *, group):
    x = x_ref[...]
    lane = lax.broadcasted_iota(jnp.int32, x.shape, 1)
    sh = 1
    while sh < LANES:
        x = x + jnp.where(lane >= sh, pltpu.roll(x, sh, axis=1), 0.0)
        sh *= 2
    sub = lax.broadcasted_iota(jnp.int32, x.shape, 0) % group
    tot = jnp.broadcast_to(x[:, LANES - 1:LANES], x.shape)
    carry = jnp.where(sub >= 1, pltpu.roll(tot, 1, axis=0), 0.0)
    sh = 1
    while sh < group:
        carry = carry + jnp.where(sub >= sh, pltpu.roll(carry, sh, axis=0), 0.0)
        sh *= 2
    o_ref[...] = x + carry


def _cumsum_rows(x2, group):
    return pl.pallas_call(
        functools.partial(_cumsum_kernel, group=group),
        out_shape=jax.ShapeDtypeStruct(x2.shape, F32),
        name="cumsum_rows",
    )(x2)


def _online_softmax_step(s, v_blk, m, l, acc):
    m_new = jnp.maximum(m, jnp.max(s, axis=1, keepdims=True))
    alpha = jnp.exp2(m - m_new)
    p = jnp.exp2(s - m_new)
    l = alpha * l + jnp.sum(p, axis=1, keepdims=True)
    acc = alpha * acc + jnp.dot(p.astype(BF16), v_blk, preferred_element_type=F32)
    return m_new, l, acc


QUERY_GROUP = 256
N_BIAS_PARTS = 3
SUM_ROWS = 16


def _attn_prompt_kernel(q_ref, kb_ref, vt_ref, ct_ref, sga_ref, o_ref,
                        ka_ref, s_ref, p_ref, acc_ref, qa_ref, *, blk):
    kv_sub = s_ref.shape[1]
    seq = q_ref.shape[0]
    n_heads = q_ref.shape[1] // HEAD_DIM
    lane = lax.broadcasted_iota(jnp.int32, (LANES, LANES), 1)

    def head_cols(hh):
        return slice(hh * HEAD_DIM, (hh + 1) * HEAD_DIM)

    for hh in range(n_heads):
        for sb in range(ct_ref.shape[2]):
            rows = slice(sb * LANES, (sb + 1) * LANES)
            rest = jnp.broadcast_to(ct_ref[hh, :, sb:sb + 1] * (-LOG2E), (LANES, LANES))
            extra = jnp.zeros((LANES, LANES), F32)
            for part in range(N_BIAS_PARTS):
                piece = rest.astype(BF16).astype(F32)
                extra = jnp.where(lane == part, piece, extra)
                rest = rest - piece
            ka_ref[hh, rows, :] = jnp.concatenate(
                [kb_ref[rows, head_cols(hh)], extra.astype(BF16)], axis=1)

    ones_feat = jnp.where(lax.broadcasted_iota(jnp.int32, (blk, LANES), 1) < N_BIAS_PARTS,
                          1.0, 0.0).astype(BF16)
    ones_rows = jnp.ones((SUM_ROWS, kv_sub), BF16)

    def load_queries(qb):
        for hh in range(n_heads):
            qa_ref[hh * blk:(hh + 1) * blk, :] = jnp.concatenate(
                [q_ref[qb * blk:(qb + 1) * blk, head_cols(hh)], ones_feat], axis=1)

    groups_per_head = blk // QUERY_GROUP
    n_groups = n_heads * groups_per_head
    group = [slice(t * QUERY_GROUP, (t + 1) * QUERY_GROUP) for t in range(n_groups)]
    head_of = [t // groups_per_head for t in range(n_groups)]
    order = [hh * groups_per_head + g for g in range(groups_per_head) for hh in range(n_heads)]

    def scores(c, g, n_keys):
        keys = slice(c * kv_sub, c * kv_sub + n_keys)
        return lax.dot_general(ka_ref[head_of[g], keys, :], qa_ref[group[g], :],
                               _NT, preferred_element_type=F32)

    own_tile = (QUERY_GROUP, QUERY_GROUP)
    not_after = (lax.broadcasted_iota(jnp.int32, own_tile, 0)
                 <= lax.broadcasted_iota(jnp.int32, own_tile, 1))

    def causal_tail(s):
        n_before = s.shape[0] - QUERY_GROUP
        tail = jnp.where(not_after, s[n_before:], -jnp.inf)
        return tail if n_before == 0 else jnp.concatenate([s[:n_before], tail], axis=0)

    def softmax(s, m):
        m_new = jnp.maximum(m, jnp.max(s, axis=0, keepdims=True))
        return m_new, jnp.exp2(m - m_new), jnp.exp2(s - m_new).astype(BF16)

    def accumulate(c, g, p, alpha):
        n_keys = p.shape[0]
        vt_sum = jnp.concatenate([vt_ref[c, head_cols(head_of[g]), :][:, :n_keys],
                                  ones_rows[:, :n_keys]], axis=0)
        pv = jnp.dot(vt_sum, p, preferred_element_type=F32)
        acc_ref[:, group[g]] = pv if alpha is None else alpha * acc_ref[:, group[g]] + pv

    def finish(qb):
        rows = slice(qb * blk, (qb + 1) * blk)
        for hh in range(n_heads):
            cols = slice(hh * blk, (hh + 1) * blk)
            out = (acc_ref[:HEAD_DIM, cols] / acc_ref[HEAD_DIM:HEAD_DIM + 1, cols]).T
            o_ref[rows, head_cols(hh)] = (
                out * sga_ref[rows, head_cols(hh)].astype(F32)).astype(BF16)

    chunks_per_block = blk // kv_sub
    items = []
    for qb in range(seq // blk):
        for c in range((qb + 1) * chunks_per_block):
            first_key = c * kv_sub - qb * blk
            n_keys, own = [], []
            for g in range(n_groups):
                first_query = (g % groups_per_head) * QUERY_GROUP
                n_keys.append(min(max(first_query + QUERY_GROUP - first_key, 0), kv_sub))
                own.append(first_key <= first_query < first_key + kv_sub)
            items.append((qb, c, n_keys, own))

    m = [None] * n_groups
    alpha = {}
    for t in range(len(items) + 2):
        qk_item = items[t] if t < len(items) else None
        sm_item = items[t - 1] if 1 <= t <= len(items) else None
        pv_item = items[t - 2] if t >= 2 else None
        if qk_item is not None and qk_item[1] == 0:
            load_queries(qk_item[0])
        for g in order:
            if qk_item is not None and qk_item[2][g]:
                qb, c, n_keys, _ = qk_item
                s_ref[t % 2, :n_keys[g], group[g]] = scores(c, g, n_keys[g])
            if pv_item is not None and pv_item[2][g]:
                qb, c, n_keys, _ = pv_item
                accumulate(c, g, p_ref[t % 2, :n_keys[g], group[g]],
                           None if c == 0 else alpha[t - 2, g])
            if sm_item is not None and sm_item[2][g]:
                qb, c, n_keys, own = sm_item
                s = s_ref[(t - 1) % 2, :n_keys[g], group[g]]
                if own[g]:
                    s = causal_tail(s)
                m_old = jnp.full((1, QUERY_GROUP), -jnp.inf, F32) if c == 0 else m[g]
                m[g], alpha[t - 1, g], p_ref[(t - 1) % 2, :n_keys[g], group[g]] = softmax(s, m_old)
        if pv_item is not None and pv_item[1] == (pv_item[0] + 1) * chunks_per_block - 1:
            finish(pv_item[0])


def _attn_prompt(q, kb, vt, cum_t, sga, *, batch, seq, blk, kv_sub, heads_per_step):
    n, d_a = q.shape
    n_heads = d_a // HEAD_DIM
    assert blk % kv_sub == 0 and seq % blk == 0 and n_heads % heads_per_step == 0
    n_chunks = seq // kv_sub
    width = heads_per_step * HEAD_DIM
    wide = heads_per_step * blk
    per_seq = lambda b, h: (b, h)
    return pl.pallas_call(
        functools.partial(_attn_prompt_kernel, blk=blk),
        grid=(batch, n_heads // heads_per_step),
        in_specs=[
            pl.BlockSpec((seq, width), per_seq),
            pl.BlockSpec((seq, width), per_seq),
            pl.BlockSpec((n_chunks, width, kv_sub), lambda b, h: (b, h, 0)),
            pl.BlockSpec((heads_per_step, None, LANES, seq // LANES), lambda b, h: (h, b, 0, 0)),
            pl.BlockSpec((seq, width), per_seq),
        ],
        out_specs=pl.BlockSpec((seq, width), per_seq),
        out_shape=jax.ShapeDtypeStruct((n, d_a), BF16),
        scratch_shapes=[pltpu.VMEM((heads_per_step, seq, HEAD_DIM + LANES), BF16),
                        pltpu.VMEM((2, kv_sub, wide), F32),
                        pltpu.VMEM((2, kv_sub, wide), BF16),
                        pltpu.VMEM((HEAD_DIM + SUM_ROWS, wide), F32),
                        pltpu.VMEM((wide, HEAD_DIM + LANES), BF16)],
        compiler_params=_compiler_params(("parallel", "parallel")),
        name="attn_prompt",
    )(q, kb, vt, cum_t, sga)


def _attn_sample_kernel(q_ref, kn_ref, vn_ref, ck_ref, cv_ref, cp_ref, cn_ref, sga_ref, o_ref,
                        m_ref, l_ref, acc_ref, *, n_heads):
    c = pl.program_id(1)

    @pl.when(c == 0)
    def _():
        m_ref[...] = jnp.full(m_ref.shape, -jnp.inf, F32)
        l_ref[...] = jnp.zeros(l_ref.shape, F32)
        acc_ref[...] = jnp.zeros(acc_ref.shape, F32)

    def head_cols(h):
        return slice(h * HEAD_DIM, (h + 1) * HEAD_DIM)

    chunk = ck_ref.shape[0] // n_heads

    def cached_head(ref, h):
        return ref[pl.ds(h, chunk, stride=n_heads), :].astype(BF16)

    heads = range(n_heads)
    s = [lax.dot_general(q_ref[:, head_cols(h)], cached_head(ck_ref, h), _NT,
                         preferred_element_type=F32) + cp_ref[h:h + 1, :] * (-LOG2E)
         for h in heads]
    m_new = [jnp.maximum(m_ref[h], jnp.max(s[h], axis=1, keepdims=True)) for h in heads]
    p = [jnp.exp2(s[h] - m_new[h]) for h in heads]
    pv = [jnp.dot(p[h].astype(BF16), cached_head(cv_ref, h), preferred_element_type=F32)
          for h in heads]
    for h in heads:
        alpha = jnp.exp2(m_ref[h] - m_new[h])
        l_ref[h] = alpha * l_ref[h] + jnp.sum(p[h], axis=1, keepdims=True)
        acc_ref[h] = alpha * acc_ref[h] + pv[h]
        m_ref[h] = m_new[h]

    @pl.when(c == pl.num_programs(1) - 1)
    def _():
        for h in range(n_heads):
            q = q_ref[:, head_cols(h)]
            s = lax.dot_general(q, kn_ref[:, head_cols(h)], _NT, preferred_element_type=F32)
            s = s + cn_ref[h:h + 1, :] * (-LOG2E)
            rows = lax.broadcasted_iota(jnp.int32, s.shape, 0)
            cols = lax.broadcasted_iota(jnp.int32, s.shape, 1)
            s = jnp.where(cols <= rows, s, -jnp.inf)
            _, l, acc = _online_softmax_step(s, vn_ref[:, head_cols(h)],
                                             m_ref[h], l_ref[h], acc_ref[h])
            o_ref[:, head_cols(h)] = (
                acc / l * sga_ref[:, head_cols(h)].astype(F32)).astype(BF16)


def _attn_sample(q, k_new, v_new, cache_k2, cache_v2, cum_past, cum_new, sga, *,
                 batch, t_new, past, chunk):
    n, d_a = q.shape
    n_heads = d_a // HEAD_DIM
    n_chunks = past // chunk
    new_map = lambda b, c: (b, 0)
    cache_map = lambda b, c: (b * n_chunks + c, 0)
    return pl.pallas_call(
        functools.partial(_attn_sample_kernel, n_heads=n_heads),
        grid=(batch, n_chunks),
        in_specs=[
            pl.BlockSpec((t_new, d_a), new_map),
            pl.BlockSpec((t_new, d_a), new_map),
            pl.BlockSpec((t_new, d_a), new_map),
            pl.BlockSpec((chunk * n_heads, HEAD_DIM), cache_map),
            pl.BlockSpec((chunk * n_heads, HEAD_DIM), cache_map),
            pl.BlockSpec((None, None, n_heads, chunk), lambda b, c: (b, c, 0, 0)),
            pl.BlockSpec((None, n_heads, t_new), lambda b, c: (b, 0, 0)),
            pl.BlockSpec((t_new, d_a), new_map),
        ],
        out_specs=pl.BlockSpec((t_new, d_a), new_map),
        out_shape=jax.ShapeDtypeStruct((n, d_a), BF16),
        scratch_shapes=[pltpu.VMEM((n_heads, t_new, 1), F32),
                        pltpu.VMEM((n_heads, t_new, 1), F32),
                        pltpu.VMEM((n_heads, t_new, HEAD_DIM), F32)],
        compiler_params=_compiler_params(("parallel", "arbitrary")),
        name="attn_sample",
    )(q, k_new, v_new, cache_k2, cache_v2, cum_past, cum_new, sga)


def _out_proj_kernel(oa_ref, ob_ref, w_ref, x_ref, fg_ref, y_ref, *, final_norm):
    o = jnp.concatenate([oa_ref[...], ob_ref[...]], axis=1)
    y = x_ref[...] + jnp.dot(o, w_ref[...], preferred_element_type=F32)
    if final_norm:
        y = _rmsnorm_rows(y, fg_ref[...])
    y_ref[...] = y


def _out_proj(out_a, out_b, w_out, x2, final_g, *, tm, final_norm):
    n, d = x2.shape
    d_a = out_a.shape[1]
    d_b = out_b.shape[1]
    row = lambda i: (i, 0)
    const = lambda i: (0, 0)
    return pl.pallas_call(
        functools.partial(_out_proj_kernel, final_norm=final_norm),
        grid=(n // tm,),
        in_specs=[
            pl.BlockSpec((tm, d_a), row),
            pl.BlockSpec((tm, d_b), row),
            pl.BlockSpec(w_out.shape, const),
            pl.BlockSpec((tm, d), row),
            pl.BlockSpec((1, d), const),
        ],
        out_specs=pl.BlockSpec((tm, d), row),
        out_shape=jax.ShapeDtypeStruct((n, d), F32),
        compiler_params=_compiler_params(("parallel",)),
        name="out_proj",
    )(out_a, out_b, w_out, x2, final_g)


def _pad_rows(a, rows):
    return jnp.pad(a, ((0, rows - a.shape[0]), (0, 0)))


def _mixing_weights(w_s, b_s, seq):
    c = min(seq, GMLP_CHUNK)
    reps = GMLP_CHUNK // c
    w = w_s[:, :c, :c] * jnp.tril(jnp.ones((c, c), w_s.dtype))
    if reps > 1:
        w = jnp.einsum('rs,gab->grasb', jnp.eye(reps, dtype=w.dtype), w)
        w = w.reshape(w_s.shape[0], GMLP_CHUNK, GMLP_CHUNK)
    b = jnp.tile(b_s[:, :c], (1, reps))
    b_full = jnp.broadcast_to(b[:, :, None], (b.shape[0], GMLP_CHUNK, LANES))
    return w.astype(BF16), b_full.astype(F32)


def _split_w_in_kernel(wt_ref, qkv_ref, rest_ref, *, off_rest):
    wt = wt_ref[...]
    qkv_ref[...] = wt[:qkv_ref.shape[1], :].T.astype(BF16)
    rest_ref[...] = wt[off_rest:, :].T.astype(BF16)


def _split_w_in(w_in_t, off_f, off_rest, *, tc=256):
    d_in, d = w_in_t.shape
    return pl.pallas_call(
        functools.partial(_split_w_in_kernel, off_rest=off_rest),
        grid=(d // tc,),
        in_specs=[pl.BlockSpec((d_in, tc), lambda i: (0, i))],
        out_specs=[pl.BlockSpec((tc, off_f), lambda i: (i, 0)),
                   pl.BlockSpec((tc, d_in - off_rest), lambda i: (i, 0))],
        out_shape=[jax.ShapeDtypeStruct((d, off_f), BF16),
                   jax.ShapeDtypeStruct((d, d_in - off_rest), BF16)],
        compiler_params=_compiler_params(("parallel",)),
        name="split_w_in",
    )(w_in_t)


def _prepare_weights(params):
    norm_g, w_in, b_f, ln_g, ln_b, w_s, b_s, w_out = params
    n_heads = b_f.shape[0]
    off_f = 3 * n_heads * HEAD_DIM
    off_ga = off_f + n_heads
    w_in_t = w_in.T
    w_qkv, w_rest = _split_w_in(w_in_t, off_f, off_ga)
    return dict(
        norm_g=norm_g[None], b_f=b_f[:, None], ln_g=ln_g[None], ln_b=ln_b[None],
        w_qkv=w_qkv, w_rest=w_rest,
        w_ft=_pad_rows(w_in_t[off_f:off_ga], 16).astype(BF16),
        w_out=w_out.astype(BF16), w_s=w_s, b_s=b_s)


def _layer(x, w, caches, final_g, *, final_norm, tm):
    batch, seq, d = x.shape
    n = batch * seq
    n_heads = w['b_f'].shape[0]
    d_b = w['ln_g'].shape[1]

    x2 = x.reshape(n, d)
    w_mix, b_mix = _mixing_weights(w['w_s'], w['b_s'], seq)

    blk, kv_sub = 1024, 512
    h, q, logft = _norm_q(x2, w['norm_g'], w['w_qkv'], w['w_ft'], w['b_f'], tm=tm)
    k, kb, v, vb, sga = _proj_kv(h, w['w_qkv'], w['w_rest'], n_heads=n_heads, tm=tm,
                                 v_sub=kv_sub if caches is None else None)
    gm = _proj_gmlp(h, w['w_rest'], w['ln_g'], w['ln_b'], w_mix, b_mix,
                    tm=tm, emit_vn=caches is not None)

    logf = logft.T.reshape(batch, seq, n_heads)
    if caches is None:
        out_b = gm[0]
        vn = None
        group = seq // LANES
        cum = _cumsum_rows(logft.reshape(n_heads * batch * group, LANES), group)
        cum_t = cum.reshape(n_heads, batch, group, LANES).transpose(0, 1, 3, 2)
        out_a = _attn_prompt(q, kb, vb, cum_t, sga, batch=batch, seq=seq, blk=blk, kv_sub=kv_sub,
                             heads_per_step=2)
    else:
        out_b, vn = gm
        cache_k, cache_v, cache_logf = caches
        past = cache_k.shape[1]
        chunk = 2048
        total = past + seq
        group = -(-total // LANES)
        lf_all = jnp.concatenate(
            [cache_logf.astype(F32).transpose(0, 2, 1), logf.transpose(0, 2, 1)], axis=2)
        lf_all = jnp.pad(lf_all, ((0, 0), (0, 0), (0, group * LANES - total)))
        cum = _cumsum_rows(lf_all.reshape(batch * n_heads * group, LANES), group)
        cum = cum.reshape(batch, n_heads, group * LANES)
        cum_past = cum[:, :, :past].reshape(batch, n_heads, past // chunk, chunk)
        cum_past = cum_past.transpose(0, 2, 1, 3)
        cum_new = cum[:, :, past:total]
        out_a = _attn_sample(q, kb, vb, cache_k.reshape(batch * past * n_heads, HEAD_DIM),
                             cache_v.reshape(batch * past * n_heads, HEAD_DIM),
                             cum_past, cum_new, sga,
                             batch=batch, t_new=seq, past=past, chunk=chunk)

    y2 = _out_proj(out_a, out_b, w['w_out'], x2, final_g[None], tm=tm, final_norm=final_norm)
    y = y2.reshape(batch, seq, d)
    k4 = k.reshape(batch, seq, n_heads, HEAD_DIM)
    v4 = v.reshape(batch, seq, n_heads, HEAD_DIM)
    vn3 = None if vn is None else vn.reshape(batch, seq, d_b)
    return y, k4, v4, logf, vn3


def kernel(x_prompt, x_sample, cache_k, cache_v, cache_logf, norm_g, w_in, b_f, ln_g, ln_b,
           w_s, b_s, w_out, final_g):
    depth = norm_g.shape[0]
    hp, hs = x_prompt, x_sample
    kp, vp, fp, ksm, vsm, fsm, gsm = [], [], [], [], [], [], []
    n_sample = x_sample.shape[0] * x_sample.shape[1]
    for l in range(depth):
        w = _prepare_weights(
            (norm_g[l], w_in[l], b_f[l], ln_g[l], ln_b[l], w_s[l], b_s[l], w_out[l]))
        last = l == depth - 1
        hp, k1, v1, f1, _ = _layer(hp, w, None, final_g, final_norm=last, tm=512)
        hs, k2, v2, f2, g2 = _layer(hs, w, (cache_k[l], cache_v[l], cache_logf[l]), final_g,
                                    final_norm=last, tm=n_sample)
        kp.append(k1); vp.append(v1); fp.append(f1)
        ksm.append(k2); vsm.append(v2); fsm.append(f2); gsm.append(g2)
    return (hp, hs, jnp.stack(kp), jnp.stack(vp), jnp.stack(fp),
            jnp.stack(ksm), jnp.stack(vsm), jnp.stack(fsm), jnp.stack(gsm))
```

```python
import functools
import math

import jax
import jax.numpy as jnp
from jax import lax
from jax.experimental import pallas as pl
from jax.experimental.pallas import tpu as pltpu

F32 = jnp.float32
BF16 = jnp.bfloat16

HEAD_DIM = 128
GMLP_CHUNK = 128
RMS_EPS = 1e-6
LN_EPS = 1e-5
LOG2E = math.log2(math.e)
LANES = 128
VMEM_LIMIT_BYTES = 48 * 1024 * 1024

_NT = (((1,), (1,)), ((), ()))


def _compiler_params(semantics):
    return pltpu.CompilerParams(dimension_semantics=semantics,
                                vmem_limit_bytes=VMEM_LIMIT_BYTES)


def _rmsnorm_rows(x, g):
    return x * lax.rsqrt(jnp.mean(x * x, axis=-1, keepdims=True) + RMS_EPS) * g


def _gelu_tanh(x):
    c = math.sqrt(2.0 / math.pi)
    return 0.5 * x * (1.0 + jnp.tanh(c * (x + 0.044715 * (x * x * x))))


def _silu(x):
    return x * jax.nn.sigmoid(x)


def _store_heads_on_sublanes(ref, z, n_heads):
    rows = z.shape[0]
    for h in range(n_heads):
        ref[pl.ds(h, rows, stride=n_heads), :] = z[:, h * HEAD_DIM:(h + 1) * HEAD_DIM]


def _project(h, w_ref):
    return jnp.dot(h, w_ref[...], preferred_element_type=F32)


def _resident(shape, block_index):
    return pl.BlockSpec(shape, lambda i: block_index, pipeline_mode=pl.Buffered(1))


def _norm_q_kernel(x_ref, g_ref, wq_ref, wft_ref, bf_ref, h_ref, q_ref, logft_ref,
                   *, q_scale, n_heads):
    d = x_ref.shape[1]
    acc = None
    ss = None
    for k0 in range(0, d, NORM_K_SLICE):
        ks = slice(k0, k0 + NORM_K_SLICE)
        xk = x_ref[:, ks]
        sq = jnp.sum(xk * xk, axis=-1, keepdims=True)
        ss = sq if ss is None else ss + sq
        part = jnp.dot((xk * g_ref[:, ks]).astype(BF16), wq_ref[ks, :],
                       preferred_element_type=F32)
        acc = part if acc is None else acc + part
    r = lax.rsqrt(ss * (1.0 / d) + RMS_EPS)
    q_ref[...] = (acc * (r * q_scale)).astype(BF16)
    hb = (x_ref[...] * r * g_ref[...]).astype(BF16)
    h_ref[...] = hb
    zf = lax.dot_general(wft_ref[...], hb, _NT, preferred_element_type=F32)
    t = zf[:n_heads] + bf_ref[...]
    logft_ref[...] = jnp.minimum(t, 0.0) - jnp.log1p(jnp.exp(-jnp.abs(t)))


def _norm_q(x2, norm_g, w_qkv, w_ft, b_f, *, tm):
    n, d = x2.shape
    n_heads = b_f.shape[0]
    d_a = n_heads * HEAD_DIM
    row = lambda i: (i, 0)
    return pl.pallas_call(
        functools.partial(_norm_q_kernel, q_scale=HEAD_DIM ** -0.5 * LOG2E, n_heads=n_heads),
        grid=(n // tm,),
        in_specs=[
            pl.BlockSpec((tm, d), row),
            _resident((1, d), (0, 0)),
            _resident((d, d_a), (0, 0)),
            _resident(w_ft.shape, (0, 0)),
            _resident((n_heads, 1), (0, 0)),
        ],
        out_specs=[
            pl.BlockSpec((tm, d), row),
            pl.BlockSpec((tm, d_a), row),
            pl.BlockSpec((n_heads, tm), lambda i: (0, i)),
        ],
        out_shape=[
            jax.ShapeDtypeStruct((n, d), BF16),
            jax.ShapeDtypeStruct((n, d_a), BF16),
            jax.ShapeDtypeStruct((n_heads, n), F32),
        ],
        compiler_params=_compiler_params(("parallel",)),
        name="norm_q",
    )(x2, norm_g, w_qkv, w_ft, b_f)


def _proj_kv_kernel(h_ref, wk_ref, wv_ref, wga_ref, k_ref, kb_ref, v_ref, vb_ref, sga_ref,
                    *, n_heads, v_sub):
    h = h_ref[...]
    z = _project(h, wk_ref)
    _store_heads_on_sublanes(k_ref, z, n_heads)
    kb_ref[...] = z.astype(BF16)
    z = _project(h, wv_ref)
    _store_heads_on_sublanes(v_ref, z, n_heads)
    if v_sub is None:
        vb_ref[...] = z.astype(BF16)
    else:
        for c in range(z.shape[0] // v_sub):
            vb_ref[c] = z[c * v_sub:(c + 1) * v_sub, :].T.astype(BF16)
    sga_ref[...] = _silu(_project(h, wga_ref)).astype(BF16)


def _proj_kv(h, w_qkv, w_rest, *, n_heads, tm, v_sub=None):
    n, d = h.shape
    d_a = n_heads * HEAD_DIM
    row = lambda i: (i, 0)
    if v_sub is None:
        vb_spec = pl.BlockSpec((tm, d_a), row)
        vb_shape = jax.ShapeDtypeStruct((n, d_a), BF16)
    else:
        vb_spec = pl.BlockSpec((tm // v_sub, d_a, v_sub), lambda i: (i, 0, 0))
        vb_shape = jax.ShapeDtypeStruct((n // v_sub, d_a, v_sub), BF16)
    return pl.pallas_call(
        functools.partial(_proj_kv_kernel, n_heads=n_heads, v_sub=v_sub),
        grid=(n // tm,),
        in_specs=[
            pl.BlockSpec((tm, d), row),
            _resident((d, d_a), (0, 1)),
            _resident((d, d_a), (0, 2)),
            _resident((d, d_a), (0, 0)),
        ],
        out_specs=[
            pl.BlockSpec((tm * n_heads, HEAD_DIM), row),
            pl.BlockSpec((tm, d_a), row),
            pl.BlockSpec((tm * n_heads, HEAD_DIM), row),
            vb_spec,
            pl.BlockSpec((tm, d_a), row),
        ],
        out_shape=[
            jax.ShapeDtypeStruct((n * n_heads, HEAD_DIM), F32),
            jax.ShapeDtypeStruct((n, d_a), BF16),
            jax.ShapeDtypeStruct((n * n_heads, HEAD_DIM), F32),
            vb_shape,
            jax.ShapeDtypeStruct((n, d_a), BF16),
        ],
        compiler_params=_compiler_params(("parallel",)),
        name="proj_kv",
    )(h, w_qkv, w_qkv, w_rest)


def _proj_gmlp_kernel(h_ref, wu_ref, wv_ref, wg_ref, lng_ref, lnb_ref, wmix_ref, bmix_ref,
                      ob_ref, *rest, n_groups, emit_vn):
    if emit_vn:
        vn_ref, mix_ref, gu_ref = rest
    else:
        mix_ref, gu_ref = rest
    h = h_ref[...]
    tm = h.shape[0]
    n_chunks = tm // GMLP_CHUNK
    d_g = wv_ref.shape[1] // n_groups

    a = _gelu_tanh(_project(h, wv_ref))
    gu_ref[...] = _gelu_tanh(_project(h, wu_ref))
    mu = jnp.mean(a, axis=-1, keepdims=True)
    ac = a - mu
    var = jnp.mean(ac * ac, axis=-1, keepdims=True)
    vn = ac * lax.rsqrt(var + LN_EPS) * lng_ref[...] + lnb_ref[...]
    if emit_vn:
        vn_ref[...] = vn
    vb = vn.astype(BF16)
    for g in range(n_groups):
        cols = slice(g * d_g, (g + 1) * d_g)
        rhs = jnp.concatenate(
            [vb[r * GMLP_CHUNK:(r + 1) * GMLP_CHUNK, cols] for r in range(n_chunks)], axis=1)
        mixed = jnp.dot(wmix_ref[g], rhs, preferred_element_type=F32)
        for r in range(n_chunks):
            mix_ref[r * GMLP_CHUNK:(r + 1) * GMLP_CHUNK, cols] = (
                mixed[:, r * d_g:(r + 1) * d_g] + bmix_ref[g])

    gate = _silu(_project(h, wg_ref))
    ob_ref[...] = (gu_ref[...] * mix_ref[...] * gate).astype(BF16)


def _proj_gmlp(h, w_rest, ln_g, ln_b, w_mix, b_mix, *, tm, emit_vn):
    n, d = h.shape
    d_b = ln_g.shape[1]
    n_groups = w_mix.shape[0]
    row = lambda i: (i, 0)
    out_specs = [pl.BlockSpec((tm, d_b), row)]
    out_shape = [jax.ShapeDtypeStruct((n, d_b), BF16)]
    if emit_vn:
        out_specs.append(pl.BlockSpec((tm, d_b), row))
        out_shape.append(jax.ShapeDtypeStruct((n, d_b), F32))
    return pl.pallas_call(
        functools.partial(_proj_gmlp_kernel, n_groups=n_groups, emit_vn=emit_vn),
        grid=(n // tm,),
        in_specs=[
            pl.BlockSpec((tm, d), row),
            _resident((d, d_b), (0, 1)),
            _resident((d, d_b), (0, 2)),
            _resident((d, d_b), (0, 3)),
            _resident((1, d_b), (0, 0)),
            _resident((1, d_b), (0, 0)),
            _resident(w_mix.shape, (0, 0, 0)),
            _resident(b_mix.shape, (0, 0, 0)),
        ],
        out_specs=out_specs,
        out_shape=out_shape,
        scratch_shapes=[pltpu.VMEM((tm, d_b), F32), pltpu.VMEM((tm, d_b), F32)],
        compiler_params=_compiler_params(("parallel",)),
        name="proj_gmlp",
    )(h, w_rest, w_rest, w_rest, ln_g, ln_b, w_mix, b_mix)


def _cumsum_kernel(x_ref, o_ref, *, group):
    x = x_ref[...]
    lane = lax.broadcasted_iota(jnp.int32, x.shape, 1)
    sh = 1
    while sh < LANES:
        x = x + jnp.where(lane >= sh, pltpu.roll(x, sh, axis=1), 0.0)
        sh *= 2
    sub = lax.broadcasted_iota(jnp.int32, x.shape, 0) % group
    tot = jnp.broadcast_to(x[:, LANES - 1:LANES], x.shape)
    carry = jnp.where(sub >= 1, pltpu.roll(tot, 1, axis=0), 0.0)
    sh = 1
    while sh < group:
        carry = carry + jnp.where(sub >= sh, pltpu.roll(carry, sh, axis=0), 0.0)
        sh *= 2
    o_ref[...] = x + carry


def _cumsum_rows(x2, group):
    return pl.pallas_call(
        functools.partial(_cumsum_kernel, group=group),
        out_shape=jax.ShapeDtypeStruct(x2.shape, F32),
        name="cumsum_rows",
    )(x2)


def _online_softmax_step(s, v_blk, m, l, acc):
    m_new = jnp.maximum(m, jnp.max(s, axis=1, keepdims=True))
    alpha = jnp.exp2(m - m_new)
    p = jnp.exp2(s - m_new)
    l = alpha * l + jnp.sum(p, axis=1, keepdims=True)
    acc = alpha * acc + jnp.dot(p.astype(BF16), v_blk, preferred_element_type=F32)
    return m_new, l, acc


NORM_K_SLICE = 256
QUERY_GROUP = 256
N_BIAS_PARTS = 3
SUM_ROWS = 16


def _attn_prompt_kernel(q_ref, kb_ref, vt_ref, ct_ref, sga_ref, o_ref,
                        ka_ref, s_ref, p_ref, acc_ref, qa_ref, *, blk):
    kv_sub = s_ref.shape[1]
    seq = q_ref.shape[0]
    n_heads = q_ref.shape[1] // HEAD_DIM
    lane = lax.broadcasted_iota(jnp.int32, (LANES, LANES), 1)

    def head_cols(hh):
        return slice(hh * HEAD_DIM, (hh + 1) * HEAD_DIM)

    def load_keys(c):
        for hh in range(n_heads):
            for sb in range(c * kv_sub // LANES, (c + 1) * kv_sub // LANES):
                rows = slice(sb * LANES, (sb + 1) * LANES)
                rest = jnp.broadcast_to(ct_ref[hh, :, sb:sb + 1] * (-LOG2E), (LANES, LANES))
                extra = jnp.zeros((LANES, LANES), F32)
                for part in range(N_BIAS_PARTS):
                    piece = rest.astype(BF16).astype(F32)
                    extra = jnp.where(lane == part, piece, extra)
                    rest = rest - piece
                ka_ref[hh, rows, :] = jnp.concatenate(
                    [kb_ref[rows, head_cols(hh)], extra.astype(BF16)], axis=1)

    ones_feat = jnp.where(lax.broadcasted_iota(jnp.int32, (blk, LANES), 1) < N_BIAS_PARTS,
                          1.0, 0.0).astype(BF16)
    ones_rows = jnp.ones((SUM_ROWS, kv_sub), BF16)

    def load_queries(qb):
        for hh in range(n_heads):
            qa_ref[hh * blk:(hh + 1) * blk, :] = jnp.concatenate(
                [q_ref[qb * blk:(qb + 1) * blk, head_cols(hh)], ones_feat], axis=1)

    groups_per_head = blk // QUERY_GROUP
    n_groups = n_heads * groups_per_head
    group = [slice(t * QUERY_GROUP, (t + 1) * QUERY_GROUP) for t in range(n_groups)]
    head_of = [t // groups_per_head for t in range(n_groups)]
    order = [hh * groups_per_head + g for g in range(groups_per_head) for hh in range(n_heads)]

    def scores(c, g, n_keys):
        keys = slice(c * kv_sub, c * kv_sub + n_keys)
        return lax.dot_general(ka_ref[head_of[g], keys, :], qa_ref[group[g], :],
                               _NT, preferred_element_type=F32)

    own_tile = (QUERY_GROUP, QUERY_GROUP)
    not_after = (lax.broadcasted_iota(jnp.int32, own_tile, 0)
                 <= lax.broadcasted_iota(jnp.int32, own_tile, 1))

    def causal_tail(s):
        n_before = s.shape[0] - QUERY_GROUP
        tail = jnp.where(not_after, s[n_before:], -jnp.inf)
        return tail if n_before == 0 else jnp.concatenate([s[:n_before], tail], axis=0)

    def softmax(s, m):
        m_new = jnp.maximum(m, jnp.max(s, axis=0, keepdims=True))
        return m_new, jnp.exp2(m - m_new), jnp.exp2(s - m_new).astype(BF16)

    def accumulate(c, g, p, alpha):
        n_keys = p.shape[0]
        vt_sum = jnp.concatenate([vt_ref[c, head_cols(head_of[g]), :][:, :n_keys],
                                  ones_rows[:, :n_keys]], axis=0)
        pv = jnp.dot(vt_sum, p, preferred_element_type=F32)
        acc_ref[:, group[g]] = pv if alpha is None else alpha * acc_ref[:, group[g]] + pv

    def finish(qb):
        rows = slice(qb * blk, (qb + 1) * blk)
        for hh in range(n_heads):
            cols = slice(hh * blk, (hh + 1) * blk)
            out = (acc_ref[:HEAD_DIM, cols] / acc_ref[HEAD_DIM:HEAD_DIM + 1, cols]).T
            o_ref[rows, head_cols(hh)] = (
                out * sga_ref[rows, head_cols(hh)].astype(F32)).astype(BF16)

    chunks_per_block = blk // kv_sub
    items = []
    for qb in range(seq // blk):
        for c in range((qb + 1) * chunks_per_block):
            first_key = c * kv_sub - qb * blk
            n_keys, own = [], []
            for g in range(n_groups):
                first_query = (g % groups_per_head) * QUERY_GROUP
                n_keys.append(min(max(first_query + QUERY_GROUP - first_key, 0), kv_sub))
                own.append(first_key <= first_query < first_key + kv_sub)
            items.append((qb, c, n_keys, own))

    m = [None] * n_groups
    alpha = {}
    for t in range(len(items) + 2):
        qk_item = items[t] if t < len(items) else None
        sm_item = items[t - 1] if 1 <= t <= len(items) else None
        pv_item = items[t - 2] if t >= 2 else None
        if qk_item is not None:
            if qk_item[1] // chunks_per_block == qk_item[0]:
                load_keys(qk_item[1])
            if qk_item[1] == 0:
                load_queries(qk_item[0])
        for g in order:
            if qk_item is not None and qk_item[2][g]:
                qb, c, n_keys, _ = qk_item
                s_ref[t % 2, :n_keys[g], group[g]] = scores(c, g, n_keys[g])
            if pv_item is not None and pv_item[2][g]:
                qb, c, n_keys, _ = pv_item
                accumulate(c, g, p_ref[t % 2, :n_keys[g], group[g]],
                           None if c == 0 else alpha[t - 2, g])
            if sm_item is not None and sm_item[2][g]:
                qb, c, n_keys, own = sm_item
                s = s_ref[(t - 1) % 2, :n_keys[g], group[g]]
                if own[g]:
                    s = causal_tail(s)
                m_old = jnp.full((1, QUERY_GROUP), -jnp.inf, F32) if c == 0 else m[g]
                m[g], alpha[t - 1, g], p_ref[(t - 1) % 2, :n_keys[g], group[g]] = softmax(s, m_old)
        if pv_item is not None and pv_item[1] == (pv_item[0] + 1) * chunks_per_block - 1:
            finish(pv_item[0])


def _attn_prompt(q, kb, vt, cum_t, sga, *, batch, seq, blk, kv_sub, heads_per_step):
    n, d_a = q.shape
    n_heads = d_a // HEAD_DIM
    assert blk % kv_sub == 0 and seq % blk == 0 and n_heads % heads_per_step == 0
    n_chunks = seq // kv_sub
    width = heads_per_step * HEAD_DIM
    wide = heads_per_step * blk
    per_seq = lambda b, h: (b, h)
    return pl.pallas_call(
        functools.partial(_attn_prompt_kernel, blk=blk),
        grid=(batch, n_heads // heads_per_step),
        in_specs=[
            pl.BlockSpec((seq, width), per_seq),
            pl.BlockSpec((seq, width), per_seq),
            pl.BlockSpec((n_chunks, width, kv_sub), lambda b, h: (b, h, 0)),
            pl.BlockSpec((heads_per_step, None, LANES, seq // LANES), lambda b, h: (h, b, 0, 0)),
            pl.BlockSpec((seq, width), per_seq),
        ],
        out_specs=pl.BlockSpec((seq, width), per_seq),
        out_shape=jax.ShapeDtypeStruct((n, d_a), BF16),
        scratch_shapes=[pltpu.VMEM((heads_per_step, seq, HEAD_DIM + LANES), BF16),
                        pltpu.VMEM((2, kv_sub, wide), F32),
                        pltpu.VMEM((2, kv_sub, wide), BF16),
                        pltpu.VMEM((HEAD_DIM + SUM_ROWS, wide), F32),
                        pltpu.VMEM((wide, HEAD_DIM + LANES), BF16)],
        compiler_params=_compiler_params(("parallel", "parallel")),
        name="attn_prompt",
    )(q, kb, vt, cum_t, sga)


def _attn_sample_kernel(q_ref, kn_ref, vn_ref, ck_ref, cv_ref, cp_ref, cn_ref, sga_ref, o_ref,
                        m_ref, l_ref, acc_ref, *, n_heads):
    c = pl.program_id(1)

    @pl.when(c == 0)
    def _():
        m_ref[...] = jnp.full(m_ref.shape, -jnp.inf, F32)
        l_ref[...] = jnp.zeros(l_ref.shape, F32)
        acc_ref[...] = jnp.zeros(acc_ref.shape, F32)

    def head_cols(h):
        return slice(h * HEAD_DIM, (h + 1) * HEAD_DIM)

    chunk = ck_ref.shape[0] // n_heads

    def cached_head(ref, h):
        return ref[pl.ds(h, chunk, stride=n_heads), :].astype(BF16)

    heads = range(n_heads)
    s = [lax.dot_general(q_ref[:, head_cols(h)], cached_head(ck_ref, h), _NT,
                         preferred_element_type=F32) + cp_ref[h:h + 1, :] * (-LOG2E)
         for h in heads]
    m_new = [jnp.maximum(m_ref[h], jnp.max(s[h], axis=1, keepdims=True)) for h in heads]
    p = [jnp.exp2(s[h] - m_new[h]) for h in heads]
    pv = [jnp.dot(p[h].astype(BF16), cached_head(cv_ref, h), preferred_element_type=F32)
          for h in heads]
    for h in heads:
        alpha = jnp.exp2(m_ref[h] - m_new[h])
        l_ref[h] = alpha * l_ref[h] + jnp.sum(p[h], axis=1, keepdims=True)
        acc_ref[h] = alpha * acc_ref[h] + pv[h]
        m_ref[h] = m_new[h]

    @pl.when(c == pl.num_programs(1) - 1)
    def _():
        for h in range(n_heads):
            q = q_ref[:, head_cols(h)]
            s = lax.dot_general(q, kn_ref[:, head_cols(h)], _NT, preferred_element_type=F32)
            s = s + cn_ref[h:h + 1, :] * (-LOG2E)
            rows = lax.broadcasted_iota(jnp.int32, s.shape, 0)
            cols = lax.broadcasted_iota(jnp.int32, s.shape, 1)
            s = jnp.where(cols <= rows, s, -jnp.inf)
            _, l, acc = _online_softmax_step(s, vn_ref[:, head_cols(h)],
                                             m_ref[h], l_ref[h], acc_ref[h])
            o_ref[:, head_cols(h)] = (
                acc / l * sga_ref[:, head_cols(h)].astype(F32)).astype(BF16)


def _attn_sample(q, k_new, v_new, cache_k2, cache_v2, cum_past, cum_new, sga, *,
                 batch, t_new, past, chunk):
    n, d_a = q.shape
    n_heads = d_a // HEAD_DIM
    n_chunks = past // chunk
    new_map = lambda b, c: (b, 0)
    cache_map = lambda b, c: (b * n_chunks + c, 0)
    return pl.pallas_call(
        functools.partial(_attn_sample_kernel, n_heads=n_heads),
        grid=(batch, n_chunks),
        in_specs=[
            pl.BlockSpec((t_new, d_a), new_map),
            pl.BlockSpec((t_new, d_a), new_map),
            pl.BlockSpec((t_new, d_a), new_map),
            pl.BlockSpec((chunk * n_heads, HEAD_DIM), cache_map),
            pl.BlockSpec((chunk * n_heads, HEAD_DIM), cache_map),
            pl.BlockSpec((None, None, n_heads, chunk), lambda b, c: (b, c, 0, 0)),
            pl.BlockSpec((None, n_heads, t_new), lambda b, c: (b, 0, 0)),
            pl.BlockSpec((t_new, d_a), new_map),
        ],
        out_specs=pl.BlockSpec((t_new, d_a), new_map),
        out_shape=jax.ShapeDtypeStruct((n, d_a), BF16),
        scratch_shapes=[pltpu.VMEM((n_heads, t_new, 1), F32),
                        pltpu.VMEM((n_heads, t_new, 1), F32),
                        pltpu.VMEM((n_heads, t_new, HEAD_DIM), F32)],
        compiler_params=_compiler_params(("parallel", "arbitrary")),
        name="attn_sample",
    )(q, k_new, v_new, cache_k2, cache_v2, cum_past, cum_new, sga)


def _out_proj_kernel(oa_ref, ob_ref, w_ref, x_ref, fg_ref, y_ref, *, final_norm):
    o = jnp.concatenate([oa_ref[...], ob_ref[...]], axis=1)
    y = x_ref[...] + jnp.dot(o, w_ref[...], preferred_element_type=F32)
    if final_norm:
        y = _rmsnorm_rows(y, fg_ref[...])
    y_ref[...] = y


def _out_proj(out_a, out_b, w_out, x2, final_g, *, tm, final_norm):
    n, d = x2.shape
    d_a = out_a.shape[1]
    d_b = out_b.shape[1]
    row = lambda i: (i, 0)
    const = lambda i: (0, 0)
    return pl.pallas_call(
        functools.partial(_out_proj_kernel, final_norm=final_norm),
        grid=(n // tm,),
        in_specs=[
            pl.BlockSpec((tm, d_a), row),
            pl.BlockSpec((tm, d_b), row),
            pl.BlockSpec(w_out.shape, const),
            pl.BlockSpec((tm, d), row),
            pl.BlockSpec((1, d), const),
        ],
        out_specs=pl.BlockSpec((tm, d), row),
        out_shape=jax.ShapeDtypeStruct((n, d), F32),
        compiler_params=_compiler_params(("parallel",)),
        name="out_proj",
    )(out_a, out_b, w_out, x2, final_g)


def _pad_rows(a, rows):
    return jnp.pad(a, ((0, rows - a.shape[0]), (0, 0)))


def _mixing_weights(w_s, b_s, seq):
    c = min(seq, GMLP_CHUNK)
    reps = GMLP_CHUNK // c
    w = w_s[:, :c, :c] * jnp.tril(jnp.ones((c, c), w_s.dtype))
    if reps > 1:
        w = jnp.einsum('rs,gab->grasb', jnp.eye(reps, dtype=w.dtype), w)
        w = w.reshape(w_s.shape[0], GMLP_CHUNK, GMLP_CHUNK)
    b = jnp.tile(b_s[:, :c], (1, reps))
    b_full = jnp.broadcast_to(b[:, :, None], (b.shape[0], GMLP_CHUNK, LANES))
    return w.astype(BF16), b_full.astype(F32)


def _split_w_in_kernel(wt_ref, qkv_ref, rest_ref, *, off_rest):
    wt = wt_ref[...]
    qkv_ref[...] = wt[:qkv_ref.shape[1], :].T.astype(BF16)
    rest_ref[...] = wt[off_rest:, :].T.astype(BF16)


def _split_w_in(w_in_t, off_f, off_rest, *, tc=256):
    d_in, d = w_in_t.shape
    return pl.pallas_call(
        functools.partial(_split_w_in_kernel, off_rest=off_rest),
        grid=(d // tc,),
        in_specs=[pl.BlockSpec((d_in, tc), lambda i: (0, i))],
        out_specs=[pl.BlockSpec((tc, off_f), lambda i: (i, 0)),
                   pl.BlockSpec((tc, d_in - off_rest), lambda i: (i, 0))],
        out_shape=[jax.ShapeDtypeStruct((d, off_f), BF16),
                   jax.ShapeDtypeStruct((d, d_in - off_rest), BF16)],
        compiler_params=_compiler_params(("parallel",)),
        name="split_w_in",
    )(w_in_t)


def _prepare_weights(params):
    norm_g, w_in, b_f, ln_g, ln_b, w_s, b_s, w_out = params
    n_heads = b_f.shape[0]
    off_f = 3 * n_heads * HEAD_DIM
    off_ga = off_f + n_heads
    w_in_t = w_in.T
    w_qkv, w_rest = _split_w_in(w_in_t, off_f, off_ga)
    return dict(
        norm_g=norm_g[None], b_f=b_f[:, None], ln_g=ln_g[None], ln_b=ln_b[None],
        w_qkv=w_qkv, w_rest=w_rest,
        w_ft=_pad_rows(w_in_t[off_f:off_ga], 16).astype(BF16),
        w_out=w_out.astype(BF16), w_s=w_s, b_s=b_s)


def _layer(x, w, caches, final_g, *, final_norm, tm):
    batch, seq, d = x.shape
    n = batch * seq
    n_heads = w['b_f'].shape[0]
    d_b = w['ln_g'].shape[1]

    x2 = x.reshape(n, d)
    w_mix, b_mix = _mixing_weights(w['w_s'], w['b_s'], seq)

    blk, kv_sub = 1024, 512
    h, q, logft = _norm_q(x2, w['norm_g'], w['w_qkv'], w['w_ft'], w['b_f'], tm=tm)
    k, kb, v, vb, sga = _proj_kv(h, w['w_qkv'], w['w_rest'], n_heads=n_heads, tm=tm,
                                 v_sub=kv_sub if caches is None else None)
    gm = _proj_gmlp(h, w['w_rest'], w['ln_g'], w['ln_b'], w_mix, b_mix,
                    tm=tm, emit_vn=caches is not None)

    logf = logft.T.reshape(batch, seq, n_heads)
    if caches is None:
        out_b = gm[0]
        vn = None
        group = seq // LANES
        cum = _cumsum_rows(logft.reshape(n_heads * batch * group, LANES), group)
        cum_t = cum.reshape(n_heads, batch, group, LANES).transpose(0, 1, 3, 2)
        out_a = _attn_prompt(q, kb, vb, cum_t, sga, batch=batch, seq=seq, blk=blk, kv_sub=kv_sub,
                             heads_per_step=2)
    else:
        out_b, vn = gm
        cache_k, cache_v, cache_logf = caches
        past = cache_k.shape[1]
        chunk = 2048
        total = past + seq
        group = -(-total // LANES)
        lf_all = jnp.concatenate(
            [cache_logf.astype(F32).transpose(0, 2, 1), logf.transpose(0, 2, 1)], axis=2)
        lf_all = jnp.pad(lf_all, ((0, 0), (0, 0), (0, group * LANES - total)))
        cum = _cumsum_rows(lf_all.reshape(batch * n_heads * group, LANES), group)
        cum = cum.reshape(batch, n_heads, group * LANES)
        cum_past = cum[:, :, :past].reshape(batch, n_heads, past // chunk, chunk)
        cum_past = cum_past.transpose(0, 2, 1, 3)
        cum_new = cum[:, :, past:total]
        out_a = _attn_sample(q, kb, vb, cache_k.reshape(batch * past * n_heads, HEAD_DIM),
                             cache_v.reshape(batch * past * n_heads, HEAD_DIM),
                             cum_past, cum_new, sga,
                             batch=batch, t_new=seq, past=past, chunk=chunk)

    y2 = _out_proj(out_a, out_b, w['w_out'], x2, final_g[None], tm=tm, final_norm=final_norm)
    y = y2.reshape(batch, seq, d)
    k4 = k.reshape(batch, seq, n_heads, HEAD_DIM)
    v4 = v.reshape(batch, seq, n_heads, HEAD_DIM)
    vn3 = None if vn is None else vn.reshape(batch, seq, d_b)
    return y, k4, v4, logf, vn3


def kernel(x_prompt, x_sample, cache_k, cache_v, cache_logf, norm_g, w_in, b_f, ln_g, ln_b,
           w_s, b_s, w_out, final_g):
    depth = norm_g.shape[0]
    hp, hs = x_prompt, x_sample
    kp, vp, fp, ksm, vsm, fsm, gsm = [], [], [], [], [], [], []
    n_sample = x_sample.shape[0] * x_sample.shape[1]
    for l in range(depth):
        w = _prepare_weights(
            (norm_g[l], w_in[l], b_f[l], ln_g[l], ln_b[l], w_s[l], b_s[l], w_out[l]))
        last = l == depth - 1
        hp, k1, v1, f1, _ = _layer(hp, w, None, final_g, final_norm=last, tm=512)
        hs, k2, v2, f2, g2 = _layer(hs, w, (cache_k[l], cache_v[l], cache_logf[l]), final_g,
                                    final_norm=last, tm=n_sample)
        kp.append(k1); vp.append(v1); fp.append(f1)
        ksm.append(k2); vsm.append(v2); fsm.append(f2); gsm.append(g2)
    return (hp, hs, jnp.stack(kp), jnp.stack(vp), jnp.stack(fp),
            jnp.stack(ksm), jnp.stack(vsm), jnp.stack(fsm), jnp.stack(gsm))
```

```python
import functools
import math

import jax
import jax.numpy as jnp
from jax import lax
from jax.experimental import pallas as pl
from jax.experimental.pallas import tpu as pltpu

F32 = jnp.float32
BF16 = jnp.bfloat16

HEAD_DIM = 128
GMLP_CHUNK = 128
RMS_EPS = 1e-6
LN_EPS = 1e-5
LOG2E = math.log2(math.e)
LANES = 128
VMEM_LIMIT_BYTES = 48 * 1024 * 1024

ROW_TILE = 512
ATTN_QUERY_BLOCK = 1024
ATTN_KEY_CHUNK = 512
ATTN_HEADS_PER_STEP = 2
CACHE_CHUNK = 2048
W_SPLIT_COLS = 256

_NT = (((1,), (1,)), ((), ()))


def _compiler_params(semantics):
    return pltpu.CompilerParams(dimension_semantics=semantics,
                                vmem_limit_bytes=VMEM_LIMIT_BYTES)


def _rmsnorm_rows(x, g):
    return x * lax.rsqrt(jnp.mean(x * x, axis=-1, keepdims=True) + RMS_EPS) * g


def _gelu_tanh(x):
    c = math.sqrt(2.0 / math.pi)
    return 0.5 * x * (1.0 + jnp.tanh(c * (x + 0.044715 * (x * x * x))))


def _silu(x):
    return x * jax.nn.sigmoid(x)


def _store_heads_on_sublanes(ref, z, n_heads):
    rows = z.shape[0]
    for h in range(n_heads):
        ref[pl.ds(h, rows, stride=n_heads), :] = z[:, h * HEAD_DIM:(h + 1) * HEAD_DIM]


def _project(h, w_ref):
    return jnp.dot(h, w_ref[...], preferred_element_type=F32)


def _resident(shape, block_index):
    return pl.BlockSpec(shape, lambda i: block_index, pipeline_mode=pl.Buffered(1))


def _norm_q_kernel(x_ref, g_ref, wq_ref, wft_ref, bf_ref, h_ref, q_ref, logft_ref,
                   *, q_scale, n_heads):
    hb = _rmsnorm_rows(x_ref[...], g_ref[...]).astype(BF16)
    h_ref[...] = hb
    q_ref[...] = (_project(hb, wq_ref) * q_scale).astype(BF16)
    zf = lax.dot_general(wft_ref[...], hb, _NT, preferred_element_type=F32)
    t = zf[:n_heads] + bf_ref[...]
    logft_ref[...] = jnp.minimum(t, 0.0) - jnp.log1p(jnp.exp(-jnp.abs(t)))


def _norm_q(x2, norm_g, w_qkv, w_ft, b_f, *, tm):
    n, d = x2.shape
    n_heads = b_f.shape[0]
    d_a = n_heads * HEAD_DIM
    row = lambda i: (i, 0)
    return pl.pallas_call(
        functools.partial(_norm_q_kernel, q_scale=HEAD_DIM ** -0.5 * LOG2E, n_heads=n_heads),
        grid=(n // tm,),
        in_specs=[
            pl.BlockSpec((tm, d), row),
            _resident((1, d), (0, 0)),
            _resident((d, d_a), (0, 0)),
            _resident(w_ft.shape, (0, 0)),
            _resident((n_heads, 1), (0, 0)),
        ],
        out_specs=[
            pl.BlockSpec((tm, d), row),
            pl.BlockSpec((tm, d_a), row),
            pl.BlockSpec((n_heads, tm), lambda i: (0, i)),
        ],
        out_shape=[
            jax.ShapeDtypeStruct((n, d), BF16),
            jax.ShapeDtypeStruct((n, d_a), BF16),
            jax.ShapeDtypeStruct((n_heads, n), F32),
        ],
        compiler_params=_compiler_params(("parallel",)),
        name="norm_q",
    )(x2, norm_g, w_qkv, w_ft, b_f)


def _proj_kv_kernel(h_ref, wk_ref, wv_ref, wga_ref, k_ref, kb_ref, v_ref, vb_ref, sga_ref,
                    *, n_heads, v_sub):
    h = h_ref[...]
    z = _project(h, wk_ref)
    _store_heads_on_sublanes(k_ref, z, n_heads)
    kb_ref[...] = z.astype(BF16)
    z = _project(h, wv_ref)
    _store_heads_on_sublanes(v_ref, z, n_heads)
    if v_sub is None:
        vb_ref[...] = z.astype(BF16)
    else:
        for c in range(z.shape[0] // v_sub):
            vb_ref[c] = z[c * v_sub:(c + 1) * v_sub, :].T.astype(BF16)
    sga_ref[...] = _silu(_project(h, wga_ref)).astype(BF16)


def _proj_kv(h, w_qkv, w_rest, *, n_heads, tm, v_sub=None):
    n, d = h.shape
    d_a = n_heads * HEAD_DIM
    row = lambda i: (i, 0)
    if v_sub is None:
        vb_spec = pl.BlockSpec((tm, d_a), row)
        vb_shape = jax.ShapeDtypeStruct((n, d_a), BF16)
    else:
        vb_spec = pl.BlockSpec((tm // v_sub, d_a, v_sub), lambda i: (i, 0, 0))
        vb_shape = jax.ShapeDtypeStruct((n // v_sub, d_a, v_sub), BF16)
    return pl.pallas_call(
        functools.partial(_proj_kv_kernel, n_heads=n_heads, v_sub=v_sub),
        grid=(n // tm,),
        in_specs=[
            pl.BlockSpec((tm, d), row),
            _resident((d, d_a), (0, 1)),
            _resident((d, d_a), (0, 2)),
            _resident((d, d_a), (0, 0)),
        ],
        out_specs=[
            pl.BlockSpec((tm * n_heads, HEAD_DIM), row),
            pl.BlockSpec((tm, d_a), row),
            pl.BlockSpec((tm * n_heads, HEAD_DIM), row),
            vb_spec,
            pl.BlockSpec((tm, d_a), row),
        ],
        out_shape=[
            jax.ShapeDtypeStruct((n * n_heads, HEAD_DIM), F32),
            jax.ShapeDtypeStruct((n, d_a), BF16),
            jax.ShapeDtypeStruct((n * n_heads, HEAD_DIM), F32),
            vb_shape,
            jax.ShapeDtypeStruct((n, d_a), BF16),
        ],
        compiler_params=_compiler_params(("parallel",)),
        name="proj_kv",
    )(h, w_qkv, w_qkv, w_rest)


def _proj_gmlp_kernel(h_ref, wu_ref, wv_ref, wg_ref, lng_ref, lnb_ref, wmix_ref, bmix_ref,
                      ob_ref, *rest, n_groups, emit_vn):
    if emit_vn:
        vn_ref, mix_ref, gu_ref = rest
    else:
        mix_ref, gu_ref = rest
    h = h_ref[...]
    tm = h.shape[0]
    n_chunks = tm // GMLP_CHUNK
    d_g = wv_ref.shape[1] // n_groups

    a = _gelu_tanh(_project(h, wv_ref))
    gu_ref[...] = _gelu_tanh(_project(h, wu_ref))
    mu = jnp.mean(a, axis=-1, keepdims=True)
    ac = a - mu
    var = jnp.mean(ac * ac, axis=-1, keepdims=True)
    vn = ac * lax.rsqrt(var + LN_EPS) * lng_ref[...] + lnb_ref[...]
    if emit_vn:
        vn_ref[...] = vn
    vb = vn.astype(BF16)
    for g in range(n_groups):
        cols = slice(g * d_g, (g + 1) * d_g)
        rhs = jnp.concatenate(
            [vb[r * GMLP_CHUNK:(r + 1) * GMLP_CHUNK, cols] for r in range(n_chunks)], axis=1)
        mixed = jnp.dot(wmix_ref[g], rhs, preferred_element_type=F32)
        for r in range(n_chunks):
            mix_ref[r * GMLP_CHUNK:(r + 1) * GMLP_CHUNK, cols] = (
                mixed[:, r * d_g:(r + 1) * d_g] + bmix_ref[g])

    gate = _silu(_project(h, wg_ref))
    ob_ref[...] = (gu_ref[...] * mix_ref[...] * gate).astype(BF16)


def _proj_gmlp(h, w_rest, ln_g, ln_b, w_mix, b_mix, *, tm, emit_vn):
    n, d = h.shape
    d_b = ln_g.shape[1]
    n_groups = w_mix.shape[0]
    row = lambda i: (i, 0)
    out_specs = [pl.BlockSpec((tm, d_b), row)]
    out_shape = [jax.ShapeDtypeStruct((n, d_b), BF16)]
    if emit_vn:
        out_specs.append(pl.BlockSpec((tm, d_b), row))
        out_shape.append(jax.ShapeDtypeStruct((n, d_b), F32))
    return pl.pallas_call(
        functools.partial(_proj_gmlp_kernel, n_groups=n_groups, emit_vn=emit_vn),
        grid=(n // tm,),
        in_specs=[
            pl.BlockSpec((tm, d), row),
            _resident((d, d_b), (0, 1)),
            _resident((d, d_b), (0, 2)),
            _resident((d, d_b), (0, 3)),
            _resident((1, d_b), (0, 0)),
            _resident((1, d_b), (0, 0)),
            _resident(w_mix.shape, (0, 0, 0)),
            _resident(b_mix.shape, (0, 0, 0)),
        ],
        out_specs=out_specs,
        out_shape=out_shape,
        scratch_shapes=[pltpu.VMEM((tm, d_b), F32), pltpu.VMEM((tm, d_b), F32)],
        compiler_params=_compiler_params(("parallel",)),
        name="proj_gmlp",
    )(h, w_rest, w_rest, w_rest, ln_g, ln_b, w_mix, b_mix)


def _cumsum_kernel(x_ref, o_ref, *, group):
    x = x_ref[...]
    lane = lax.broadcasted_iota(jnp.int32, x.shape, 1)
    sh = 1
    while sh < LANES:
        x = x + jnp.where(lane >= sh, pltpu.roll(x, sh, axis=1), 0.0)
        sh *= 2
    sub = lax.broadcasted_iota(jnp.int32, x.shape, 0) % group
    tot = jnp.broadcast_to(x[:, LANES - 1:LANES], x.shape)
    carry = jnp.where(sub >= 1, pltpu.roll(tot, 1, axis=0), 0.0)
    sh = 1
    while sh < group:
        carry = carry + jnp.where(sub >= sh, pltpu.roll(carry, sh, axis=0), 0.0)
        sh *= 2
    o_ref[...] = x + carry


def _cumsum_rows(x2, group):
    return pl.pallas_call(
        functools.partial(_cumsum_kernel, group=group),
        out_shape=jax.ShapeDtypeStruct(x2.shape, F32),
        name="cumsum_rows",
    )(x2)


def _online_softmax_step(s, v_blk, m, l, acc):
    m_new = jnp.maximum(m, jnp.max(s, axis=1, keepdims=True))
    alpha = jnp.exp2(m - m_new)
    p = jnp.exp2(s - m_new)
    l = alpha * l + jnp.sum(p, axis=1, keepdims=True)
    acc = alpha * acc + jnp.dot(p.astype(BF16), v_blk, preferred_element_type=F32)
    return m_new, l, acc


QUERY_GROUP = 256
N_BIAS_PARTS = 3
SUM_ROWS = 16


def _attn_prompt_kernel(q_ref, kb_ref, vt_ref, ct_ref, sga_ref, o_ref,
                        ka_ref, s_ref, p_ref, acc_ref, qa_ref, *, blk):
    kv_sub = s_ref.shape[1]
    seq = q_ref.shape[0]
    n_heads = q_ref.shape[1] // HEAD_DIM
    lane = lax.broadcasted_iota(jnp.int32, (LANES, LANES), 1)

    def head_cols(hh):
        return slice(hh * HEAD_DIM, (hh + 1) * HEAD_DIM)

    def load_keys(c):
        for hh in range(n_heads):
            for sb in range(c * kv_sub // LANES, (c + 1) * kv_sub // LANES):
                rows = slice(sb * LANES, (sb + 1) * LANES)
                rest = jnp.broadcast_to(ct_ref[hh, :, sb:sb + 1] * (-LOG2E), (LANES, LANES))
                extra = jnp.zeros((LANES, LANES), F32)
                for part in range(N_BIAS_PARTS):
                    piece = rest.astype(BF16).astype(F32)
                    extra = jnp.where(lane == part, piece, extra)
                    rest = rest - piece
                ka_ref[hh, rows, :] = jnp.concatenate(
                    [kb_ref[rows, head_cols(hh)], extra.astype(BF16)], axis=1)

    ones_feat = jnp.where(lax.broadcasted_iota(jnp.int32, (blk, LANES), 1) < N_BIAS_PARTS,
                          1.0, 0.0).astype(BF16)
    ones_rows = jnp.ones((SUM_ROWS, kv_sub), BF16)

    def load_queries(qb):
        for hh in range(n_heads):
            qa_ref[hh * blk:(hh + 1) * blk, :] = jnp.concatenate(
                [q_ref[qb * blk:(qb + 1) * blk, head_cols(hh)], ones_feat], axis=1)

    groups_per_head = blk // QUERY_GROUP
    n_groups = n_heads * groups_per_head
    group = [slice(t * QUERY_GROUP, (t + 1) * QUERY_GROUP) for t in range(n_groups)]
    head_of = [t // groups_per_head for t in range(n_groups)]
    order = [hh * groups_per_head + g for g in range(groups_per_head) for hh in range(n_heads)]

    def scores(c, g, n_keys):
        keys = slice(c * kv_sub, c * kv_sub + n_keys)
        return lax.dot_general(ka_ref[head_of[g], keys, :], qa_ref[group[g], :],
                               _NT, preferred_element_type=F32)

    own_tile = (QUERY_GROUP, QUERY_GROUP)
    not_after = (lax.broadcasted_iota(jnp.int32, own_tile, 0)
                 <= lax.broadcasted_iota(jnp.int32, own_tile, 1))

    def causal_tail(s):
        n_before = s.shape[0] - QUERY_GROUP
        tail = jnp.where(not_after, s[n_before:], -jnp.inf)
        return tail if n_before == 0 else jnp.concatenate([s[:n_before], tail], axis=0)

    def softmax(s, m):
        m_new = jnp.maximum(m, jnp.max(s, axis=0, keepdims=True))
        return m_new, jnp.exp2(m - m_new), jnp.exp2(s - m_new).astype(BF16)

    def accumulate(c, g, p, alpha):
        n_keys = p.shape[0]
        vt_sum = jnp.concatenate([vt_ref[c, head_cols(head_of[g]), :][:, :n_keys],
                                  ones_rows[:, :n_keys]], axis=0)
        pv = jnp.dot(vt_sum, p, preferred_element_type=F32)
        acc_ref[:, group[g]] = pv if alpha is None else alpha * acc_ref[:, group[g]] + pv

    def finish(qb):
        rows = slice(qb * blk, (qb + 1) * blk)
        for hh in range(n_heads):
            cols = slice(hh * blk, (hh + 1) * blk)
            out = (acc_ref[:HEAD_DIM, cols] / acc_ref[HEAD_DIM:HEAD_DIM + 1, cols]).T
            o_ref[rows, head_cols(hh)] = (
                out * sga_ref[rows, head_cols(hh)].astype(F32)).astype(BF16)

    chunks_per_block = blk // kv_sub
    items = []
    for qb in range(seq // blk):
        for c in range((qb + 1) * chunks_per_block):
            first_key = c * kv_sub - qb * blk
            n_keys, own = [], []
            for g in range(n_groups):
                first_query = (g % groups_per_head) * QUERY_GROUP
                n_keys.append(min(max(first_query + QUERY_GROUP - first_key, 0), kv_sub))
                own.append(first_key <= first_query < first_key + kv_sub)
            items.append((qb, c, n_keys, own))

    m = [None] * n_groups
    alpha = {}
    for t in range(len(items) + 2):
        qk_item = items[t] if t < len(items) else None
        sm_item = items[t - 1] if 1 <= t <= len(items) else None
        pv_item = items[t - 2] if t >= 2 else None
        if qk_item is not None:
            if qk_item[1] // chunks_per_block == qk_item[0]:
                load_keys(qk_item[1])
            if qk_item[1] == 0:
                load_queries(qk_item[0])
        for g in order:
            if qk_item is not None and qk_item[2][g]:
                qb, c, n_keys, _ = qk_item
                s_ref[t % 2, :n_keys[g], group[g]] = scores(c, g, n_keys[g])
            if pv_item is not None and pv_item[2][g]:
                qb, c, n_keys, _ = pv_item
                accumulate(c, g, p_ref[t % 2, :n_keys[g], group[g]],
                           None if c == 0 else alpha[t - 2, g])
            if sm_item is not None and sm_item[2][g]:
                qb, c, n_keys, own = sm_item
                s = s_ref[(t - 1) % 2, :n_keys[g], group[g]]
                if own[g]:
                    s = causal_tail(s)
                m_old = jnp.full((1, QUERY_GROUP), -jnp.inf, F32) if c == 0 else m[g]
                m[g], alpha[t - 1, g], p_ref[(t - 1) % 2, :n_keys[g], group[g]] = softmax(s, m_old)
        if pv_item is not None and pv_item[1] == (pv_item[0] + 1) * chunks_per_block - 1:
            finish(pv_item[0])


def _attn_prompt(q, kb, vt, cum_t, sga, *, batch, seq, blk, kv_sub, heads_per_step):
    n, d_a = q.shape
    n_heads = d_a // HEAD_DIM
    assert blk % kv_sub == 0 and seq % blk == 0 and n_heads % heads_per_step == 0
    n_chunks = seq // kv_sub
    width = heads_per_step * HEAD_DIM
    wide = heads_per_step * blk
    per_seq = lambda b, h: (b, h)
    return pl.pallas_call(
        functools.partial(_attn_prompt_kernel, blk=blk),
        grid=(batch, n_heads // heads_per_step),
        in_specs=[
            pl.BlockSpec((seq, width), per_seq),
            pl.BlockSpec((seq, width), per_seq),
            pl.BlockSpec((n_chunks, width, kv_sub), lambda b, h: (b, h, 0)),
            pl.BlockSpec((heads_per_step, None, LANES, seq // LANES), lambda b, h: (h, b, 0, 0)),
            pl.BlockSpec((seq, width), per_seq),
        ],
        out_specs=pl.BlockSpec((seq, width), per_seq),
        out_shape=jax.ShapeDtypeStruct((n, d_a), BF16),
        scratch_shapes=[pltpu.VMEM((heads_per_step, seq, HEAD_DIM + LANES), BF16),
                        pltpu.VMEM((2, kv_sub, wide), F32),
                        pltpu.VMEM((2, kv_sub, wide), BF16),
                        pltpu.VMEM((HEAD_DIM + SUM_ROWS, wide), F32),
                        pltpu.VMEM((wide, HEAD_DIM + LANES), BF16)],
        compiler_params=_compiler_params(("parallel", "parallel")),
        name="attn_prompt",
    )(q, kb, vt, cum_t, sga)


def _attn_sample_kernel(q_ref, kn_ref, vn_ref, ck_ref, cv_ref, cp_ref, cn_ref, sga_ref, o_ref,
                        m_ref, l_ref, acc_ref, *, n_heads):
    c = pl.program_id(1)

    @pl.when(c == 0)
    def _():
        m_ref[...] = jnp.full(m_ref.shape, -jnp.inf, F32)
        l_ref[...] = jnp.zeros(l_ref.shape, F32)
        acc_ref[...] = jnp.zeros(acc_ref.shape, F32)

    def head_cols(h):
        return slice(h * HEAD_DIM, (h + 1) * HEAD_DIM)

    chunk = ck_ref.shape[0] // n_heads

    def cached_head(ref, h):
        return ref[pl.ds(h, chunk, stride=n_heads), :].astype(BF16)

    heads = range(n_heads)
    s = [lax.dot_general(q_ref[:, head_cols(h)], cached_head(ck_ref, h), _NT,
                         preferred_element_type=F32) + cp_ref[h:h + 1, :] * (-LOG2E)
         for h in heads]
    m_new = [jnp.maximum(m_ref[h], jnp.max(s[h], axis=1, keepdims=True)) for h in heads]
    p = [jnp.exp2(s[h] - m_new[h]) for h in heads]
    pv = [jnp.dot(p[h].astype(BF16), cached_head(cv_ref, h), preferred_element_type=F32)
          for h in heads]
    for h in heads:
        alpha = jnp.exp2(m_ref[h] - m_new[h])
        l_ref[h] = alpha * l_ref[h] + jnp.sum(p[h], axis=1, keepdims=True)
        acc_ref[h] = alpha * acc_ref[h] + pv[h]
        m_ref[h] = m_new[h]

    @pl.when(c == pl.num_programs(1) - 1)
    def _():
        for h in range(n_heads):
            q = q_ref[:, head_cols(h)]
            s = lax.dot_general(q, kn_ref[:, head_cols(h)], _NT, preferred_element_type=F32)
            s = s + cn_ref[h:h + 1, :] * (-LOG2E)
            rows = lax.broadcasted_iota(jnp.int32, s.shape, 0)
            cols = lax.broadcasted_iota(jnp.int32, s.shape, 1)
            s = jnp.where(cols <= rows, s, -jnp.inf)
            _, l, acc = _online_softmax_step(s, vn_ref[:, head_cols(h)],
                                             m_ref[h], l_ref[h], acc_ref[h])
            o_ref[:, head_cols(h)] = (
                acc / l * sga_ref[:, head_cols(h)].astype(F32)).astype(BF16)


def _attn_sample(q, k_new, v_new, cache_k2, cache_v2, cum_past, cum_new, sga, *,
                 batch, t_new, past, chunk):
    n, d_a = q.shape
    n_heads = d_a // HEAD_DIM
    n_chunks = past // chunk
    new_map = lambda b, c: (b, 0)
    cache_map = lambda b, c: (b * n_chunks + c, 0)
    return pl.pallas_call(
        functools.partial(_attn_sample_kernel, n_heads=n_heads),
        grid=(batch, n_chunks),
        in_specs=[
            pl.BlockSpec((t_new, d_a), new_map),
            pl.BlockSpec((t_new, d_a), new_map),
            pl.BlockSpec((t_new, d_a), new_map),
            pl.BlockSpec((chunk * n_heads, HEAD_DIM), cache_map),
            pl.BlockSpec((chunk * n_heads, HEAD_DIM), cache_map),
            pl.BlockSpec((None, None, n_heads, chunk), lambda b, c: (b, c, 0, 0)),
            pl.BlockSpec((None, n_heads, t_new), lambda b, c: (b, 0, 0)),
            pl.BlockSpec((t_new, d_a), new_map),
        ],
        out_specs=pl.BlockSpec((t_new, d_a), new_map),
        out_shape=jax.ShapeDtypeStruct((n, d_a), BF16),
        scratch_shapes=[pltpu.VMEM((n_heads, t_new, 1), F32),
                        pltpu.VMEM((n_heads, t_new, 1), F32),
                        pltpu.VMEM((n_heads, t_new, HEAD_DIM), F32)],
        compiler_params=_compiler_params(("parallel", "arbitrary")),
        name="attn_sample",
    )(q, k_new, v_new, cache_k2, cache_v2, cum_past, cum_new, sga)


def _out_proj_kernel(oa_ref, ob_ref, w_ref, x_ref, fg_ref, y_ref, *, final_norm):
    o = jnp.concatenate([oa_ref[...], ob_ref[...]], axis=1)
    y = x_ref[...] + jnp.dot(o, w_ref[...], preferred_element_type=F32)
    if final_norm:
        y = _rmsnorm_rows(y, fg_ref[...])
    y_ref[...] = y


def _out_proj(out_a, out_b, w_out, x2, final_g, *, tm, final_norm):
    n, d = x2.shape
    d_a = out_a.shape[1]
    d_b = out_b.shape[1]
    row = lambda i: (i, 0)
    const = lambda i: (0, 0)
    return pl.pallas_call(
        functools.partial(_out_proj_kernel, final_norm=final_norm),
        grid=(n // tm,),
        in_specs=[
            pl.BlockSpec((tm, d_a), row),
            pl.BlockSpec((tm, d_b), row),
            pl.BlockSpec(w_out.shape, const),
            pl.BlockSpec((tm, d), row),
            pl.BlockSpec((1, d), const),
        ],
        out_specs=pl.BlockSpec((tm, d), row),
        out_shape=jax.ShapeDtypeStruct((n, d), F32),
        compiler_params=_compiler_params(("parallel",)),
        name="out_proj",
    )(out_a, out_b, w_out, x2, final_g)


def _pad_rows(a, rows):
    return jnp.pad(a, ((0, rows - a.shape[0]), (0, 0)))


def _mixing_weights(w_s, b_s, seq):
    c = min(seq, GMLP_CHUNK)
    reps = GMLP_CHUNK // c
    w = w_s[:, :c, :c] * jnp.tril(jnp.ones((c, c), w_s.dtype))
    if reps > 1:
        w = jnp.einsum('rs,gab->grasb', jnp.eye(reps, dtype=w.dtype), w)
        w = w.reshape(w_s.shape[0], GMLP_CHUNK, GMLP_CHUNK)
    b = jnp.tile(b_s[:, :c], (1, reps))
    b_full = jnp.broadcast_to(b[:, :, None], (b.shape[0], GMLP_CHUNK, LANES))
    return w.astype(BF16), b_full.astype(F32)


def _split_w_in_kernel(wt_ref, qkv_ref, rest_ref, *, off_rest):
    wt = wt_ref[...]
    qkv_ref[...] = wt[:qkv_ref.shape[1], :].T.astype(BF16)
    rest_ref[...] = wt[off_rest:, :].T.astype(BF16)


def _split_w_in(w_in_t, off_f, off_rest, *, tc=W_SPLIT_COLS):
    d_in, d = w_in_t.shape
    return pl.pallas_call(
        functools.partial(_split_w_in_kernel, off_rest=off_rest),
        grid=(d // tc,),
        in_specs=[pl.BlockSpec((d_in, tc), lambda i: (0, i))],
        out_specs=[pl.BlockSpec((tc, off_f), lambda i: (i, 0)),
                   pl.BlockSpec((tc, d_in - off_rest), lambda i: (i, 0))],
        out_shape=[jax.ShapeDtypeStruct((d, off_f), BF16),
                   jax.ShapeDtypeStruct((d, d_in - off_rest), BF16)],
        compiler_params=_compiler_params(("parallel",)),
        name="split_w_in",
    )(w_in_t)


def _prepare_weights(params):
    norm_g, w_in, b_f, ln_g, ln_b, w_s, b_s, w_out = params
    n_heads = b_f.shape[0]
    off_f = 3 * n_heads * HEAD_DIM
    off_ga = off_f + n_heads
    w_in_t = w_in.T
    w_qkv, w_rest = _split_w_in(w_in_t, off_f, off_ga)
    return dict(
        norm_g=norm_g[None], b_f=b_f[:, None], ln_g=ln_g[None], ln_b=ln_b[None],
        w_qkv=w_qkv, w_rest=w_rest,
        w_ft=_pad_rows(w_in_t[off_f:off_ga], 16).astype(BF16),
        w_out=w_out.astype(BF16), w_s=w_s, b_s=b_s)


def _layer(x, w, caches, final_g, *, final_norm, tm):
    batch, seq, d = x.shape
    n = batch * seq
    n_heads = w['b_f'].shape[0]
    d_b = w['ln_g'].shape[1]

    x2 = x.reshape(n, d)
    w_mix, b_mix = _mixing_weights(w['w_s'], w['b_s'], seq)

    blk, kv_sub = ATTN_QUERY_BLOCK, ATTN_KEY_CHUNK
    h, q, logft = _norm_q(x2, w['norm_g'], w['w_qkv'], w['w_ft'], w['b_f'], tm=tm)
    k, kb, v, vb, sga = _proj_kv(h, w['w_qkv'], w['w_rest'], n_heads=n_heads, tm=tm,
                                 v_sub=kv_sub if caches is None else None)
    gm = _proj_gmlp(h, w['w_rest'], w['ln_g'], w['ln_b'], w_mix, b_mix,
                    tm=tm, emit_vn=caches is not None)

    logf = logft.T.reshape(batch, seq, n_heads)
    if caches is None:
        out_b = gm[0]
        vn = None
        group = seq // LANES
        cum = _cumsum_rows(logft.reshape(n_heads * batch * group, LANES), group)
        cum_t = cum.reshape(n_heads, batch, group, LANES).transpose(0, 1, 3, 2)
        out_a = _attn_prompt(q, kb, vb, cum_t, sga, batch=batch, seq=seq, blk=blk, kv_sub=kv_sub,
                             heads_per_step=ATTN_HEADS_PER_STEP)
    else:
        out_b, vn = gm
        cache_k, cache_v, cache_logf = caches
        past = cache_k.shape[1]
        chunk = CACHE_CHUNK
        total = past + seq
        group = -(-total // LANES)
        lf_all = jnp.concatenate(
            [cache_logf.astype(F32).transpose(0, 2, 1), logf.transpose(0, 2, 1)], axis=2)
        lf_all = jnp.pad(lf_all, ((0, 0), (0, 0), (0, group * LANES - total)))
        cum = _cumsum_rows(lf_all.reshape(batch * n_heads * group, LANES), group)
        cum = cum.reshape(batch, n_heads, group * LANES)
        cum_past = cum[:, :, :past].reshape(batch, n_heads, past // chunk, chunk)
        cum_past = cum_past.transpose(0, 2, 1, 3)
        cum_new = cum[:, :, past:total]
        out_a = _attn_sample(q, kb, vb, cache_k.reshape(batch * past * n_heads, HEAD_DIM),
                             cache_v.reshape(batch * past * n_heads, HEAD_DIM),
                             cum_past, cum_new, sga,
                             batch=batch, t_new=seq, past=past, chunk=chunk)

    y2 = _out_proj(out_a, out_b, w['w_out'], x2, final_g[None], tm=tm, final_norm=final_norm)
    y = y2.reshape(batch, seq, d)
    k4 = k.reshape(batch, seq, n_heads, HEAD_DIM)
    v4 = v.reshape(batch, seq, n_heads, HEAD_DIM)
    vn3 = None if vn is None else vn.reshape(batch, seq, d_b)
    return y, k4, v4, logf, vn3


def kernel(x_prompt, x_sample, cache_k, cache_v, cache_logf, norm_g, w_in, b_f, ln_g, ln_b,
           w_s, b_s, w_out, final_g):
    depth = norm_g.shape[0]
    hp, hs = x_prompt, x_sample
    kp, vp, fp, ksm, vsm, fsm, gsm = [], [], [], [], [], [], []
    n_sample = x_sample.shape[0] * x_sample.shape[1]
    for l in range(depth):
        w = _prepare_weights(
            (norm_g[l], w_in[l], b_f[l], ln_g[l], ln_b[l], w_s[l], b_s[l], w_out[l]))
        last = l == depth - 1
        hp, k1, v1, f1, _ = _layer(hp, w, None, final_g, final_norm=last, tm=ROW_TILE)
        hs, k2, v2, f2, g2 = _layer(hs, w, (cache_k[l], cache_v[l], cache_logf[l]), final_g,
                                    final_norm=last, tm=n_sample)
        kp.append(k1); vp.append(v1); fp.append(f1)
        ksm.append(k2); vsm.append(v2); fsm.append(f2); gsm.append(g2)
    return (hp, hs, jnp.stack(kp), jnp.stack(vp), jnp.stack(fp),
            jnp.stack(ksm), jnp.stack(vsm), jnp.stack(fsm), jnp.stack(gsm))
```

```python
import functools
import math

import jax
import jax.numpy as jnp
from jax import lax
from jax.experimental import pallas as pl
from jax.experimental.pallas import tpu as pltpu

F32 = jnp.float32
BF16 = jnp.bfloat16

HEAD_DIM = 128
GMLP_CHUNK = 128
RMS_EPS = 1e-6
LN_EPS = 1e-5
LOG2E = math.log2(math.e)
LANES = 128
VMEM_LIMIT_BYTES = 48 * 1024 * 1024

ROW_TILE = 512
ATTN_QUERY_BLOCK = 1024
ATTN_KEY_CHUNK = 512
ATTN_HEADS_PER_STEP = 1
CACHE_CHUNK = 2048
W_SPLIT_COLS = 256

_NT = (((1,), (1,)), ((), ()))


def _compiler_params(semantics):
    return pltpu.CompilerParams(dimension_semantics=semantics,
                                vmem_limit_bytes=VMEM_LIMIT_BYTES)


def _rmsnorm_rows(x, g):
    return x * lax.rsqrt(jnp.mean(x * x, axis=-1, keepdims=True) + RMS_EPS) * g


def _gelu_tanh(x):
    c = math.sqrt(2.0 / math.pi)
    return 0.5 * x * (1.0 + jnp.tanh(c * (x + 0.044715 * (x * x * x))))


def _silu(x):
    return x * jax.nn.sigmoid(x)


def _store_heads_on_sublanes(ref, z, n_heads):
    rows = z.shape[0]
    for h in range(n_heads):
        ref[pl.ds(h, rows, stride=n_heads), :] = z[:, h * HEAD_DIM:(h + 1) * HEAD_DIM]


def _project(h, w_ref):
    return jnp.dot(h, w_ref[...], preferred_element_type=F32)


def _resident(shape, block_index):
    return pl.BlockSpec(shape, lambda i: block_index, pipeline_mode=pl.Buffered(1))


def _norm_q_kernel(x_ref, g_ref, wq_ref, wft_ref, bf_ref, h_ref, q_ref, logft_ref,
                   *, q_scale, n_heads):
    hb = _rmsnorm_rows(x_ref[...], g_ref[...]).astype(BF16)
    h_ref[...] = hb
    q_ref[...] = (_project(hb, wq_ref) * q_scale).astype(BF16)
    zf = lax.dot_general(wft_ref[...], hb, _NT, preferred_element_type=F32)
    t = zf[:n_heads] + bf_ref[...]
    logft_ref[...] = jnp.minimum(t, 0.0) - jnp.log1p(jnp.exp(-jnp.abs(t)))


def _norm_q(x2, norm_g, w_qkv, w_ft, b_f, *, tm):
    n, d = x2.shape
    n_heads = b_f.shape[0]
    d_a = n_heads * HEAD_DIM
    row = lambda i: (i, 0)
    return pl.pallas_call(
        functools.partial(_norm_q_kernel, q_scale=HEAD_DIM ** -0.5 * LOG2E, n_heads=n_heads),
        grid=(n // tm,),
        in_specs=[
            pl.BlockSpec((tm, d), row),
            _resident((1, d), (0, 0)),
            _resident((d, d_a), (0, 0)),
            _resident(w_ft.shape, (0, 0)),
            _resident((n_heads, 1), (0, 0)),
        ],
        out_specs=[
            pl.BlockSpec((tm, d), row),
            pl.BlockSpec((tm, d_a), row),
            pl.BlockSpec((n_heads, tm), lambda i: (0, i)),
        ],
        out_shape=[
            jax.ShapeDtypeStruct((n, d), BF16),
            jax.ShapeDtypeStruct((n, d_a), BF16),
            jax.ShapeDtypeStruct((n_heads, n), F32),
        ],
        compiler_params=_compiler_params(("parallel",)),
        name="norm_q",
    )(x2, norm_g, w_qkv, w_ft, b_f)


def _proj_kv_kernel(h_ref, wk_ref, wv_ref, wga_ref, k_ref, kb_ref, v_ref, vb_ref, sga_ref,
                    *, n_heads, v_sub):
    h = h_ref[...]
    z = _project(h, wk_ref)
    _store_heads_on_sublanes(k_ref, z, n_heads)
    kb_ref[...] = z.astype(BF16)
    z = _project(h, wv_ref)
    _store_heads_on_sublanes(v_ref, z, n_heads)
    if v_sub is None:
        vb_ref[...] = z.astype(BF16)
    else:
        for c in range(z.shape[0] // v_sub):
            vb_ref[c] = z[c * v_sub:(c + 1) * v_sub, :].T.astype(BF16)
    sga_ref[...] = _silu(_project(h, wga_ref)).astype(BF16)


def _proj_kv(h, w_qkv, w_rest, *, n_heads, tm, v_sub=None):
    n, d = h.shape
    d_a = n_heads * HEAD_DIM
    row = lambda i: (i, 0)
    if v_sub is None:
        vb_spec = pl.BlockSpec((tm, d_a), row)
        vb_shape = jax.ShapeDtypeStruct((n, d_a), BF16)
    else:
        vb_spec = pl.BlockSpec((tm // v_sub, d_a, v_sub), lambda i: (i, 0, 0))
        vb_shape = jax.ShapeDtypeStruct((n // v_sub, d_a, v_sub), BF16)
    return pl.pallas_call(
        functools.partial(_proj_kv_kernel, n_heads=n_heads, v_sub=v_sub),
        grid=(n // tm,),
        in_specs=[
            pl.BlockSpec((tm, d), row),
            _resident((d, d_a), (0, 1)),
            _resident((d, d_a), (0, 2)),
            _resident((d, d_a), (0, 0)),
        ],
        out_specs=[
            pl.BlockSpec((tm * n_heads, HEAD_DIM), row),
            pl.BlockSpec((tm, d_a), row),
            pl.BlockSpec((tm * n_heads, HEAD_DIM), row),
            vb_spec,
            pl.BlockSpec((tm, d_a), row),
        ],
        out_shape=[
            jax.ShapeDtypeStruct((n * n_heads, HEAD_DIM), F32),
            jax.ShapeDtypeStruct((n, d_a), BF16),
            jax.ShapeDtypeStruct((n * n_heads, HEAD_DIM), F32),
            vb_shape,
            jax.ShapeDtypeStruct((n, d_a), BF16),
        ],
        compiler_params=_compiler_params(("parallel",)),
        name="proj_kv",
    )(h, w_qkv, w_qkv, w_rest)


def _proj_gmlp_kernel(h_ref, wu_ref, wv_ref, wg_ref, lng_ref, lnb_ref, wmix_ref, bmix_ref,
                      ob_ref, *rest, n_groups, emit_vn):
    if emit_vn:
        vn_ref, mix_ref, gu_ref = rest
    else:
        mix_ref, gu_ref = rest
    h = h_ref[...]
    tm = h.shape[0]
    n_chunks = tm // GMLP_CHUNK
    d_g = wv_ref.shape[1] // n_groups

    a = _gelu_tanh(_project(h, wv_ref))
    gu_ref[...] = _gelu_tanh(_project(h, wu_ref))
    mu = jnp.mean(a, axis=-1, keepdims=True)
    ac = a - mu
    var = jnp.mean(ac * ac, axis=-1, keepdims=True)
    vn = ac * lax.rsqrt(var + LN_EPS) * lng_ref[...] + lnb_ref[...]
    if emit_vn:
        vn_ref[...] = vn
    vb = vn.astype(BF16)
    for g in range(n_groups):
        cols = slice(g * d_g, (g + 1) * d_g)
        rhs = jnp.concatenate(
            [vb[r * GMLP_CHUNK:(r + 1) * GMLP_CHUNK, cols] for r in range(n_chunks)], axis=1)
        mixed = jnp.dot(wmix_ref[g], rhs, preferred_element_type=F32)
        for r in range(n_chunks):
            mix_ref[r * GMLP_CHUNK:(r + 1) * GMLP_CHUNK, cols] = (
                mixed[:, r * d_g:(r + 1) * d_g] + bmix_ref[g])

    gate = _silu(_project(h, wg_ref))
    ob_ref[...] = (gu_ref[...] * mix_ref[...] * gate).astype(BF16)


def _proj_gmlp(h, w_rest, ln_g, ln_b, w_mix, b_mix, *, tm, emit_vn):
    n, d = h.shape
    d_b = ln_g.shape[1]
    n_groups = w_mix.shape[0]
    row = lambda i: (i, 0)
    out_specs = [pl.BlockSpec((tm, d_b), row)]
    out_shape = [jax.ShapeDtypeStruct((n, d_b), BF16)]
    if emit_vn:
        out_specs.append(pl.BlockSpec((tm, d_b), row))
        out_shape.append(jax.ShapeDtypeStruct((n, d_b), F32))
    return pl.pallas_call(
        functools.partial(_proj_gmlp_kernel, n_groups=n_groups, emit_vn=emit_vn),
        grid=(n // tm,),
        in_specs=[
            pl.BlockSpec((tm, d), row),
            _resident((d, d_b), (0, 1)),
            _resident((d, d_b), (0, 2)),
            _resident((d, d_b), (0, 3)),
            _resident((1, d_b), (0, 0)),
            _resident((1, d_b), (0, 0)),
            _resident(w_mix.shape, (0, 0, 0)),
            _resident(b_mix.shape, (0, 0, 0)),
        ],
        out_specs=out_specs,
        out_shape=out_shape,
        scratch_shapes=[pltpu.VMEM((tm, d_b), F32), pltpu.VMEM((tm, d_b), F32)],
        compiler_params=_compiler_params(("parallel",)),
        name="proj_gmlp",
    )(h, w_rest, w_rest, w_rest, ln_g, ln_b, w_mix, b_mix)


def _cumsum_kernel(x_ref, o_ref, *, group):
    x = x_ref[...]
    lane = lax.broadcasted_iota(jnp.int32, x.shape, 1)
    sh = 1
    while sh < LANES:
        x = x + jnp.where(lane >= sh, pltpu.roll(x, sh, axis=1), 0.0)
        sh *= 2
    sub = lax.broadcasted_iota(jnp.int32, x.shape, 0) % group
    tot = jnp.broadcast_to(x[:, LANES - 1:LANES], x.shape)
    carry = jnp.where(sub >= 1, pltpu.roll(tot, 1, axis=0), 0.0)
    sh = 1
    while sh < group:
        carry = carry + jnp.where(sub >= sh, pltpu.roll(carry, sh, axis=0), 0.0)
        sh *= 2
    o_ref[...] = x + carry


def _cumsum_rows(x2, group):
    return pl.pallas_call(
        functools.partial(_cumsum_kernel, group=group),
        out_shape=jax.ShapeDtypeStruct(x2.shape, F32),
        name="cumsum_rows",
    )(x2)


def _online_softmax_step(s, v_blk, m, l, acc):
    m_new = jnp.maximum(m, jnp.max(s, axis=1, keepdims=True))
    alpha = jnp.exp2(m - m_new)
    p = jnp.exp2(s - m_new)
    l = alpha * l + jnp.sum(p, axis=1, keepdims=True)
    acc = alpha * acc + jnp.dot(p.astype(BF16), v_blk, preferred_element_type=F32)
    return m_new, l, acc


QUERY_GROUP = 256
N_BIAS_PARTS = 3
SUM_ROWS = 16


def _attn_prompt_kernel(q_ref, kb_ref, vt_ref, ct_ref, sga_ref, o_ref,
                        ka_ref, s_ref, p_ref, acc_ref, qa_ref, *, blk):
    kv_sub = s_ref.shape[1]
    seq = q_ref.shape[0]
    n_heads = q_ref.shape[1] // HEAD_DIM
    lane = lax.broadcasted_iota(jnp.int32, (LANES, LANES), 1)

    def head_cols(hh):
        return slice(hh * HEAD_DIM, (hh + 1) * HEAD_DIM)

    def load_keys(c):
        for hh in range(n_heads):
            for sb in range(c * kv_sub // LANES, (c + 1) * kv_sub // LANES):
                rows = slice(sb * LANES, (sb + 1) * LANES)
                rest = jnp.broadcast_to(ct_ref[hh, :, sb:sb + 1] * (-LOG2E), (LANES, LANES))
                extra = jnp.zeros((LANES, LANES), F32)
                for part in range(N_BIAS_PARTS):
                    piece = rest.astype(BF16).astype(F32)
                    extra = jnp.where(lane == part, piece, extra)
                    rest = rest - piece
                ka_ref[hh, rows, :] = jnp.concatenate(
                    [kb_ref[rows, head_cols(hh)], extra.astype(BF16)], axis=1)

    ones_feat = jnp.where(lax.broadcasted_iota(jnp.int32, (blk, LANES), 1) < N_BIAS_PARTS,
                          1.0, 0.0).astype(BF16)
    ones_rows = jnp.ones((SUM_ROWS, kv_sub), BF16)

    def load_queries(qb):
        for hh in range(n_heads):
            qa_ref[hh * blk:(hh + 1) * blk, :] = jnp.concatenate(
                [q_ref[qb * blk:(qb + 1) * blk, head_cols(hh)], ones_feat], axis=1)

    groups_per_head = blk // QUERY_GROUP
    n_groups = n_heads * groups_per_head
    group = [slice(t * QUERY_GROUP, (t + 1) * QUERY_GROUP) for t in range(n_groups)]
    head_of = [t // groups_per_head for t in range(n_groups)]
    order = [hh * groups_per_head + g for g in range(groups_per_head) for hh in range(n_heads)]

    def scores(c, g, n_keys):
        keys = slice(c * kv_sub, c * kv_sub + n_keys)
        return lax.dot_general(ka_ref[head_of[g], keys, :], qa_ref[group[g], :],
                               _NT, preferred_element_type=F32)

    own_tile = (QUERY_GROUP, QUERY_GROUP)
    not_after = (lax.broadcasted_iota(jnp.int32, own_tile, 0)
                 <= lax.broadcasted_iota(jnp.int32, own_tile, 1))

    def causal_tail(s):
        n_before = s.shape[0] - QUERY_GROUP
        tail = jnp.where(not_after, s[n_before:], -jnp.inf)
        return tail if n_before == 0 else jnp.concatenate([s[:n_before], tail], axis=0)

    def softmax(s, m):
        m_new = jnp.maximum(m, jnp.max(s, axis=0, keepdims=True))
        return m_new, jnp.exp2(m - m_new), jnp.exp2(s - m_new).astype(BF16)

    def accumulate(c, g, p, alpha):
        n_keys = p.shape[0]
        vt_sum = jnp.concatenate([vt_ref[c, head_cols(head_of[g]), :][:, :n_keys],
                                  ones_rows[:, :n_keys]], axis=0)
        pv = jnp.dot(vt_sum, p, preferred_element_type=F32)
        acc_ref[:, group[g]] = pv if alpha is None else alpha * acc_ref[:, group[g]] + pv

    def finish(qb):
        rows = slice(qb * blk, (qb + 1) * blk)
        for hh in range(n_heads):
            cols = slice(hh * blk, (hh + 1) * blk)
            out = (acc_ref[:HEAD_DIM, cols] / acc_ref[HEAD_DIM:HEAD_DIM + 1, cols]).T
            o_ref[rows, head_cols(hh)] = (
                out * sga_ref[rows, head_cols(hh)].astype(F32)).astype(BF16)

    chunks_per_block = blk // kv_sub
    items = []
    for qb in range(seq // blk):
        for c in range((qb + 1) * chunks_per_block):
            first_key = c * kv_sub - qb * blk
            n_keys, own = [], []
            for g in range(n_groups):
                first_query = (g % groups_per_head) * QUERY_GROUP
                n_keys.append(min(max(first_query + QUERY_GROUP - first_key, 0), kv_sub))
                own.append(first_key <= first_query < first_key + kv_sub)
            items.append((qb, c, n_keys, own))

    m = [None] * n_groups
    alpha = {}
    for t in range(len(items) + 2):
        qk_item = items[t] if t < len(items) else None
        sm_item = items[t - 1] if 1 <= t <= len(items) else None
        pv_item = items[t - 2] if t >= 2 else None
        if qk_item is not None:
            if qk_item[1] // chunks_per_block == qk_item[0]:
                load_keys(qk_item[1])
            if qk_item[1] == 0:
                load_queries(qk_item[0])
        for g in order:
            if qk_item is not None and qk_item[2][g]:
                qb, c, n_keys, _ = qk_item
                s_ref[t % 2, :n_keys[g], group[g]] = scores(c, g, n_keys[g])
            if pv_item is not None and pv_item[2][g]:
                qb, c, n_keys, _ = pv_item
                accumulate(c, g, p_ref[t % 2, :n_keys[g], group[g]],
                           None if c == 0 else alpha[t - 2, g])
            if sm_item is not None and sm_item[2][g]:
                qb, c, n_keys, own = sm_item
                s = s_ref[(t - 1) % 2, :n_keys[g], group[g]]
                if own[g]:
                    s = causal_tail(s)
                m_old = jnp.full((1, QUERY_GROUP), -jnp.inf, F32) if c == 0 else m[g]
                m[g], alpha[t - 1, g], p_ref[(t - 1) % 2, :n_keys[g], group[g]] = softmax(s, m_old)
        if pv_item is not None and pv_item[1] == (pv_item[0] + 1) * chunks_per_block - 1:
            finish(pv_item[0])


def _attn_prompt(q, kb, vt, cum_t, sga, *, batch, seq, blk, kv_sub, heads_per_step):
    n, d_a = q.shape
    n_heads = d_a // HEAD_DIM
    assert blk % kv_sub == 0 and seq % blk == 0 and n_heads % heads_per_step == 0
    n_chunks = seq // kv_sub
    width = heads_per_step * HEAD_DIM
    wide = heads_per_step * blk
    per_seq = lambda b, h: (b, h)
    return pl.pallas_call(
        functools.partial(_attn_prompt_kernel, blk=blk),
        grid=(batch, n_heads // heads_per_step),
        in_specs=[
            pl.BlockSpec((seq, width), per_seq),
            pl.BlockSpec((seq, width), per_seq),
            pl.BlockSpec((n_chunks, width, kv_sub), lambda b, h: (b, h, 0)),
            pl.BlockSpec((heads_per_step, None, LANES, seq // LANES), lambda b, h: (h, b, 0, 0)),
            pl.BlockSpec((seq, width), per_seq),
        ],
        out_specs=pl.BlockSpec((seq, width), per_seq),
        out_shape=jax.ShapeDtypeStruct((n, d_a), BF16),
        scratch_shapes=[pltpu.VMEM((heads_per_step, seq, HEAD_DIM + LANES), BF16),
                        pltpu.VMEM((2, kv_sub, wide), F32),
                        pltpu.VMEM((2, kv_sub, wide), BF16),
                        pltpu.VMEM((HEAD_DIM + SUM_ROWS, wide), F32),
                        pltpu.VMEM((wide, HEAD_DIM + LANES), BF16)],
        compiler_params=_compiler_params(("parallel", "parallel")),
        name="attn_prompt",
    )(q, kb, vt, cum_t, sga)


def _attn_sample_kernel(q_ref, kn_ref, vn_ref, ck_ref, cv_ref, cp_ref, cn_ref, sga_ref, o_ref,
                        m_ref, l_ref, acc_ref, *, n_heads):
    c = pl.program_id(1)

    @pl.when(c == 0)
    def _():
        m_ref[...] = jnp.full(m_ref.shape, -jnp.inf, F32)
        l_ref[...] = jnp.zeros(l_ref.shape, F32)
        acc_ref[...] = jnp.zeros(acc_ref.shape, F32)

    def head_cols(h):
        return slice(h * HEAD_DIM, (h + 1) * HEAD_DIM)

    chunk = ck_ref.shape[0] // n_heads

    def cached_head(ref, h):
        return ref[pl.ds(h, chunk, stride=n_heads), :].astype(BF16)

    heads = range(n_heads)
    s = [lax.dot_general(q_ref[:, head_cols(h)], cached_head(ck_ref, h), _NT,
                         preferred_element_type=F32) + cp_ref[h:h + 1, :] * (-LOG2E)
         for h in heads]
    m_new = [jnp.maximum(m_ref[h], jnp.max(s[h], axis=1, keepdims=True)) for h in heads]
    p = [jnp.exp2(s[h] - m_new[h]) for h in heads]
    pv = [jnp.dot(p[h].astype(BF16), cached_head(cv_ref, h), preferred_element_type=F32)
          for h in heads]
    for h in heads:
        alpha = jnp.exp2(m_ref[h] - m_new[h])
        l_ref[h] = alpha * l_ref[h] + jnp.sum(p[h], axis=1, keepdims=True)
        acc_ref[h] = alpha * acc_ref[h] + pv[h]
        m_ref[h] = m_new[h]

    @pl.when(c == pl.num_programs(1) - 1)
    def _():
        for h in range(n_heads):
            q = q_ref[:, head_cols(h)]
            s = lax.dot_general(q, kn_ref[:, head_cols(h)], _NT, preferred_element_type=F32)
            s = s + cn_ref[h:h + 1, :] * (-LOG2E)
            rows = lax.broadcasted_iota(jnp.int32, s.shape, 0)
            cols = lax.broadcasted_iota(jnp.int32, s.shape, 1)
            s = jnp.where(cols <= rows, s, -jnp.inf)
            _, l, acc = _online_softmax_step(s, vn_ref[:, head_cols(h)],
                                             m_ref[h], l_ref[h], acc_ref[h])
            o_ref[:, head_cols(h)] = (
                acc / l * sga_ref[:, head_cols(h)].astype(F32)).astype(BF16)


def _attn_sample(q, k_new, v_new, cache_k2, cache_v2, cum_past, cum_new, sga, *,
                 batch, t_new, past, chunk):
    n, d_a = q.shape
    n_heads = d_a // HEAD_DIM
    n_chunks = past // chunk
    new_map = lambda b, c: (b, 0)
    cache_map = lambda b, c: (b * n_chunks + c, 0)
    return pl.pallas_call(
        functools.partial(_attn_sample_kernel, n_heads=n_heads),
        grid=(batch, n_chunks),
        in_specs=[
            pl.BlockSpec((t_new, d_a), new_map),
            pl.BlockSpec((t_new, d_a), new_map),
            pl.BlockSpec((t_new, d_a), new_map),
            pl.BlockSpec((chunk * n_heads, HEAD_DIM), cache_map),
            pl.BlockSpec((chunk * n_heads, HEAD_DIM), cache_map),
            pl.BlockSpec((None, None, n_heads, chunk), lambda b, c: (b, c, 0, 0)),
            pl.BlockSpec((None, n_heads, t_new), lambda b, c: (b, 0, 0)),
            pl.BlockSpec((t_new, d_a), new_map),
        ],
        out_specs=pl.BlockSpec((t_new, d_a), new_map),
        out_shape=jax.ShapeDtypeStruct((n, d_a), BF16),
        scratch_shapes=[pltpu.VMEM((n_heads, t_new, 1), F32),
                        pltpu.VMEM((n_heads, t_new, 1), F32),
                        pltpu.VMEM((n_heads, t_new, HEAD_DIM), F32)],
        compiler_params=_compiler_params(("parallel", "arbitrary")),
        name="attn_sample",
    )(q, k_new, v_new, cache_k2, cache_v2, cum_past, cum_new, sga)


def _out_proj_kernel(oa_ref, ob_ref, w_ref, x_ref, fg_ref, y_ref, *, final_norm):
    o = jnp.concatenate([oa_ref[...], ob_ref[...]], axis=1)
    y = x_ref[...] + jnp.dot(o, w_ref[...], preferred_element_type=F32)
    if final_norm:
        y = _rmsnorm_rows(y, fg_ref[...])
    y_ref[...] = y


def _out_proj(out_a, out_b, w_out, x2, final_g, *, tm, final_norm):
    n, d = x2.shape
    d_a = out_a.shape[1]
    d_b = out_b.shape[1]
    row = lambda i: (i, 0)
    const = lambda i: (0, 0)
    return pl.pallas_call(
        functools.partial(_out_proj_kernel, final_norm=final_norm),
        grid=(n // tm,),
        in_specs=[
            pl.BlockSpec((tm, d_a), row),
            pl.BlockSpec((tm, d_b), row),
            pl.BlockSpec(w_out.shape, const),
            pl.BlockSpec((tm, d), row),
            pl.BlockSpec((1, d), const),
        ],
        out_specs=pl.BlockSpec((tm, d), row),
        out_shape=jax.ShapeDtypeStruct((n, d), F32),
        compiler_params=_compiler_params(("parallel",)),
        name="out_proj",
    )(out_a, out_b, w_out, x2, final_g)


def _pad_rows(a, rows):
    return jnp.pad(a, ((0, rows - a.shape[0]), (0, 0)))


def _mixing_weights(w_s, b_s, seq):
    c = min(seq, GMLP_CHUNK)
    reps = GMLP_CHUNK // c
    w = w_s[:, :c, :c] * jnp.tril(jnp.ones((c, c), w_s.dtype))
    if reps > 1:
        w = jnp.einsum('rs,gab->grasb', jnp.eye(reps, dtype=w.dtype), w)
        w = w.reshape(w_s.shape[0], GMLP_CHUNK, GMLP_CHUNK)
    b = jnp.tile(b_s[:, :c], (1, reps))
    b_full = jnp.broadcast_to(b[:, :, None], (b.shape[0], GMLP_CHUNK, LANES))
    return w.astype(BF16), b_full.astype(F32)


def _split_w_in_kernel(wt_ref, qkv_ref, rest_ref, *, off_rest):
    wt = wt_ref[...]
    qkv_ref[...] = wt[:qkv_ref.shape[1], :].T.astype(BF16)
    rest_ref[...] = wt[off_rest:, :].T.astype(BF16)


def _split_w_in(w_in_t, off_f, off_rest, *, tc=W_SPLIT_COLS):
    d_in, d = w_in_t.shape
    return pl.pallas_call(
        functools.partial(_split_w_in_kernel, off_rest=off_rest),
        grid=(d // tc,),
        in_specs=[pl.BlockSpec((d_in, tc), lambda i: (0, i))],
        out_specs=[pl.BlockSpec((tc, off_f), lambda i: (i, 0)),
                   pl.BlockSpec((tc, d_in - off_rest), lambda i: (i, 0))],
        out_shape=[jax.ShapeDtypeStruct((d, off_f), BF16),
                   jax.ShapeDtypeStruct((d, d_in - off_rest), BF16)],
        compiler_params=_compiler_params(("parallel",)),
        name="split_w_in",
    )(w_in_t)


def _prepare_weights(params):
    norm_g, w_in, b_f, ln_g, ln_b, w_s, b_s, w_out = params
    n_heads = b_f.shape[0]
    off_f = 3 * n_heads * HEAD_DIM
    off_ga = off_f + n_heads
    w_in_t = w_in.T
    w_qkv, w_rest = _split_w_in(w_in_t, off_f, off_ga)
    return dict(
        norm_g=norm_g[None], b_f=b_f[:, None], ln_g=ln_g[None], ln_b=ln_b[None],
        w_qkv=w_qkv, w_rest=w_rest,
        w_ft=_pad_rows(w_in_t[off_f:off_ga], 16).astype(BF16),
        w_out=w_out.astype(BF16), w_s=w_s, b_s=b_s)


def _layer(x, w, caches, final_g, *, final_norm, tm):
    batch, seq, d = x.shape
    n = batch * seq
    n_heads = w['b_f'].shape[0]
    d_b = w['ln_g'].shape[1]

    x2 = x.reshape(n, d)
    w_mix, b_mix = _mixing_weights(w['w_s'], w['b_s'], seq)

    blk, kv_sub = ATTN_QUERY_BLOCK, ATTN_KEY_CHUNK
    h, q, logft = _norm_q(x2, w['norm_g'], w['w_qkv'], w['w_ft'], w['b_f'], tm=tm)
    k, kb, v, vb, sga = _proj_kv(h, w['w_qkv'], w['w_rest'], n_heads=n_heads, tm=tm,
                                 v_sub=kv_sub if caches is None else None)
    gm = _proj_gmlp(h, w['w_rest'], w['ln_g'], w['ln_b'], w_mix, b_mix,
                    tm=tm, emit_vn=caches is not None)

    logf = logft.T.reshape(batch, seq, n_heads)
    if caches is None:
        out_b = gm[0]
        vn = None
        group = seq // LANES
        cum = _cumsum_rows(logft.reshape(n_heads * batch * group, LANES), group)
        cum_t = cum.reshape(n_heads, batch, group, LANES).transpose(0, 1, 3, 2)
        out_a = _attn_prompt(q, kb, vb, cum_t, sga, batch=batch, seq=seq, blk=blk, kv_sub=kv_sub,
                             heads_per_step=ATTN_HEADS_PER_STEP)
    else:
        out_b, vn = gm
        cache_k, cache_v, cache_logf = caches
        past = cache_k.shape[1]
        chunk = CACHE_CHUNK
        total = past + seq
        group = -(-total // LANES)
        lf_all = jnp.concatenate(
            [cache_logf.astype(F32).transpose(0, 2, 1), logf.transpose(0, 2, 1)], axis=2)
        lf_all = jnp.pad(lf_all, ((0, 0), (0, 0), (0, group * LANES - total)))
        cum = _cumsum_rows(lf_all.reshape(batch * n_heads * group, LANES), group)
        cum = cum.reshape(batch, n_heads, group * LANES)
        cum_past = cum[:, :, :past].reshape(batch, n_heads, past // chunk, chunk)
        cum_past = cum_past.transpose(0, 2, 1, 3)
        cum_new = cum[:, :, past:total]
        out_a = _attn_sample(q, kb, vb, cache_k.reshape(batch * past * n_heads, HEAD_DIM),
                             cache_v.reshape(batch * past * n_heads, HEAD_DIM),
                             cum_past, cum_new, sga,
                             batch=batch, t_new=seq, past=past, chunk=chunk)

    y2 = _out_proj(out_a, out_b, w['w_out'], x2, final_g[None], tm=tm, final_norm=final_norm)
    y = y2.reshape(batch, seq, d)
    k4 = k.reshape(batch, seq, n_heads, HEAD_DIM)
    v4 = v.reshape(batch, seq, n_heads, HEAD_DIM)
    vn3 = None if vn is None else vn.reshape(batch, seq, d_b)
    return y, k4, v4, logf, vn3


def kernel(x_prompt, x_sample, cache_k, cache_v, cache_logf, norm_g, w_in, b_f, ln_g, ln_b,
           w_s, b_s, w_out, final_g):
    depth = norm_g.shape[0]
    hp, hs = x_prompt, x_sample
    kp, vp, fp, ksm, vsm, fsm, gsm = [], [], [], [], [], [], []
    n_sample = x_sample.shape[0] * x_sample.shape[1]
    for l in range(depth):
        w = _prepare_weights(
            (norm_g[l], w_in[l], b_f[l], ln_g[l], ln_b[l], w_s[l], b_s[l], w_out[l]))
        last = l == depth - 1
        hp, k1, v1, f1, _ = _layer(hp, w, None, final_g, final_norm=last, tm=ROW_TILE)
        hs, k2, v2, f2, g2 = _layer(hs, w, (cache_k[l], cache_v[l], cache_logf[l]), final_g,
                                    final_norm=last, tm=n_sample)
        kp.append(k1); vp.append(v1); fp.append(f1)
        ksm.append(k2); vsm.append(v2); fsm.append(f2); gsm.append(g2)
    return (hp, hs, jnp.stack(kp), jnp.stack(vp), jnp.stack(fp),
            jnp.stack(ksm), jnp.stack(vsm), jnp.stack(fsm), jnp.stack(gsm))
```

```python
import functools
import math

import jax
import jax.numpy as jnp
from jax import lax
from jax.experimental import pallas as pl
from jax.experimental.pallas import tpu as pltpu

F32 = jnp.float32
BF16 = jnp.bfloat16

HEAD_DIM = 128
GMLP_CHUNK = 128
RMS_EPS = 1e-6
LN_EPS = 1e-5
LOG2E = math.log2(math.e)
LANES = 128
VMEM_LIMIT_BYTES = 48 * 1024 * 1024

ROW_TILE = 512
ATTN_QUERY_BLOCK = 2048
ATTN_KEY_CHUNK = 512
ATTN_HEADS_PER_STEP = 1
CACHE_CHUNK = 2048
W_SPLIT_COLS = 256

_NT = (((1,), (1,)), ((), ()))


def _compiler_params(semantics):
    return pltpu.CompilerParams(dimension_semantics=semantics,
                                vmem_limit_bytes=VMEM_LIMIT_BYTES)


def _rmsnorm_rows(x, g):
    return x * lax.rsqrt(jnp.mean(x * x, axis=-1, keepdims=True) + RMS_EPS) * g


def _gelu_tanh(x):
    c = math.sqrt(2.0 / math.pi)
    return 0.5 * x * (1.0 + jnp.tanh(c * (x + 0.044715 * (x * x * x))))


def _silu(x):
    return x * jax.nn.sigmoid(x)


def _store_heads_on_sublanes(ref, z, n_heads):
    rows = z.shape[0]
    for h in range(n_heads):
        ref[pl.ds(h, rows, stride=n_heads), :] = z[:, h * HEAD_DIM:(h + 1) * HEAD_DIM]


def _project(h, w_ref):
    return jnp.dot(h, w_ref[...], preferred_element_type=F32)


def _resident(shape, block_index):
    return pl.BlockSpec(shape, lambda i: block_index, pipeline_mode=pl.Buffered(1))


def _norm_q_kernel(x_ref, g_ref, wq_ref, wft_ref, bf_ref, h_ref, q_ref, logft_ref,
                   *, q_scale, n_heads):
    hb = _rmsnorm_rows(x_ref[...], g_ref[...]).astype(BF16)
    h_ref[...] = hb
    q_ref[...] = (_project(hb, wq_ref) * q_scale).astype(BF16)
    zf = lax.dot_general(wft_ref[...], hb, _NT, preferred_element_type=F32)
    t = zf[:n_heads] + bf_ref[...]
    logft_ref[...] = jnp.minimum(t, 0.0) - jnp.log1p(jnp.exp(-jnp.abs(t)))


def _norm_q(x2, norm_g, w_qkv, w_ft, b_f, *, tm):
    n, d = x2.shape
    n_heads = b_f.shape[0]
    d_a = n_heads * HEAD_DIM
    row = lambda i: (i, 0)
    return pl.pallas_call(
        functools.partial(_norm_q_kernel, q_scale=HEAD_DIM ** -0.5 * LOG2E, n_heads=n_heads),
        grid=(n // tm,),
        in_specs=[
            pl.BlockSpec((tm, d), row),
            _resident((1, d), (0, 0)),
            _resident((d, d_a), (0, 0)),
            _resident(w_ft.shape, (0, 0)),
            _resident((n_heads, 1), (0, 0)),
        ],
        out_specs=[
            pl.BlockSpec((tm, d), row),
            pl.BlockSpec((tm, d_a), row),
            pl.BlockSpec((n_heads, tm), lambda i: (0, i)),
        ],
        out_shape=[
            jax.ShapeDtypeStruct((n, d), BF16),
            jax.ShapeDtypeStruct((n, d_a), BF16),
            jax.ShapeDtypeStruct((n_heads, n), F32),
        ],
        compiler_params=_compiler_params(("parallel",)),
        name="norm_q",
    )(x2, norm_g, w_qkv, w_ft, b_f)


def _proj_kv_kernel(h_ref, wk_ref, wv_ref, wga_ref, k_ref, kb_ref, v_ref, vb_ref, sga_ref,
                    *, n_heads, v_sub):
    h = h_ref[...]
    z = _project(h, wk_ref)
    _store_heads_on_sublanes(k_ref, z, n_heads)
    kb_ref[...] = z.astype(BF16)
    z = _project(h, wv_ref)
    _store_heads_on_sublanes(v_ref, z, n_heads)
    if v_sub is None:
        vb_ref[...] = z.astype(BF16)
    else:
        for c in range(z.shape[0] // v_sub):
            vb_ref[c] = z[c * v_sub:(c + 1) * v_sub, :].T.astype(BF16)
    sga_ref[...] = _silu(_project(h, wga_ref)).astype(BF16)


def _proj_kv(h, w_qkv, w_rest, *, n_heads, tm, v_sub=None):
    n, d = h.shape
    d_a = n_heads * HEAD_DIM
    row = lambda i: (i, 0)
    if v_sub is None:
        vb_spec = pl.BlockSpec((tm, d_a), row)
        vb_shape = jax.ShapeDtypeStruct((n, d_a), BF16)
    else:
        vb_spec = pl.BlockSpec((tm // v_sub, d_a, v_sub), lambda i: (i, 0, 0))
        vb_shape = jax.ShapeDtypeStruct((n // v_sub, d_a, v_sub), BF16)
    return pl.pallas_call(
        functools.partial(_proj_kv_kernel, n_heads=n_heads, v_sub=v_sub),
        grid=(n // tm,),
        in_specs=[
            pl.BlockSpec((tm, d), row),
            _resident((d, d_a), (0, 1)),
            _resident((d, d_a), (0, 2)),
            _resident((d, d_a), (0, 0)),
        ],
        out_specs=[
            pl.BlockSpec((tm * n_heads, HEAD_DIM), row),
            pl.BlockSpec((tm, d_a), row),
            pl.BlockSpec((tm * n_heads, HEAD_DIM), row),
            vb_spec,
            pl.BlockSpec((tm, d_a), row),
        ],
        out_shape=[
            jax.ShapeDtypeStruct((n * n_heads, HEAD_DIM), F32),
            jax.ShapeDtypeStruct((n, d_a), BF16),
            jax.ShapeDtypeStruct((n * n_heads, HEAD_DIM), F32),
            vb_shape,
            jax.ShapeDtypeStruct((n, d_a), BF16),
        ],
        compiler_params=_compiler_params(("parallel",)),
        name="proj_kv",
    )(h, w_qkv, w_qkv, w_rest)


def _proj_gmlp_kernel(h_ref, wu_ref, wv_ref, wg_ref, lng_ref, lnb_ref, wmix_ref, bmix_ref,
                      ob_ref, *rest, n_groups, emit_vn):
    if emit_vn:
        vn_ref, mix_ref, gu_ref = rest
    else:
        mix_ref, gu_ref = rest
    h = h_ref[...]
    tm = h.shape[0]
    n_chunks = tm // GMLP_CHUNK
    d_g = wv_ref.shape[1] // n_groups

    a = _gelu_tanh(_project(h, wv_ref))
    gu_ref[...] = _gelu_tanh(_project(h, wu_ref))
    mu = jnp.mean(a, axis=-1, keepdims=True)
    ac = a - mu
    var = jnp.mean(ac * ac, axis=-1, keepdims=True)
    vn = ac * lax.rsqrt(var + LN_EPS) * lng_ref[...] + lnb_ref[...]
    if emit_vn:
        vn_ref[...] = vn
    vb = vn.astype(BF16)
    for g in range(n_groups):
        cols = slice(g * d_g, (g + 1) * d_g)
        rhs = jnp.concatenate(
            [vb[r * GMLP_CHUNK:(r + 1) * GMLP_CHUNK, cols] for r in range(n_chunks)], axis=1)
        mixed = jnp.dot(wmix_ref[g], rhs, preferred_element_type=F32)
        for r in range(n_chunks):
            mix_ref[r * GMLP_CHUNK:(r + 1) * GMLP_CHUNK, cols] = (
                mixed[:, r * d_g:(r + 1) * d_g] + bmix_ref[g])

    gate = _silu(_project(h, wg_ref))
    ob_ref[...] = (gu_ref[...] * mix_ref[...] * gate).astype(BF16)


def _proj_gmlp(h, w_rest, ln_g, ln_b, w_mix, b_mix, *, tm, emit_vn):
    n, d = h.shape
    d_b = ln_g.shape[1]
    n_groups = w_mix.shape[0]
    row = lambda i: (i, 0)
    out_specs = [pl.BlockSpec((tm, d_b), row)]
    out_shape = [jax.ShapeDtypeStruct((n, d_b), BF16)]
    if emit_vn:
        out_specs.append(pl.BlockSpec((tm, d_b), row))
        out_shape.append(jax.ShapeDtypeStruct((n, d_b), F32))
    return pl.pallas_call(
        functools.partial(_proj_gmlp_kernel, n_groups=n_groups, emit_vn=emit_vn),
        grid=(n // tm,),
        in_specs=[
            pl.BlockSpec((tm, d), row),
            _resident((d, d_b), (0, 1)),
            _resident((d, d_b), (0, 2)),
            _resident((d, d_b), (0, 3)),
            _resident((1, d_b), (0, 0)),
            _resident((1, d_b), (0, 0)),
            _resident(w_mix.shape, (0, 0, 0)),
            _resident(b_mix.shape, (0, 0, 0)),
        ],
        out_specs=out_specs,
        out_shape=out_shape,
        scratch_shapes=[pltpu.VMEM((tm, d_b), F32), pltpu.VMEM((tm, d_b), F32)],
        compiler_params=_compiler_params(("parallel",)),
        name="proj_gmlp",
    )(h, w_rest, w_rest, w_rest, ln_g, ln_b, w_mix, b_mix)


def _cumsum_kernel(x_ref, o_ref, *, group):
    x = x_ref[...]
    lane = lax.broadcasted_iota(jnp.int32, x.shape, 1)
    sh = 1
    while sh < LANES:
        x = x + jnp.where(lane >= sh, pltpu.roll(x, sh, axis=1), 0.0)
        sh *= 2
    sub = lax.broadcasted_iota(jnp.int32, x.shape, 0) % group
    tot = jnp.broadcast_to(x[:, LANES - 1:LANES], x.shape)
    carry = jnp.where(sub >= 1, pltpu.roll(tot, 1, axis=0), 0.0)
    sh = 1
    while sh < group:
        carry = carry + jnp.where(sub >= sh, pltpu.roll(carry, sh, axis=0), 0.0)
        sh *= 2
    o_ref[...] = x + carry


def _cumsum_rows(x2, group):
    return pl.pallas_call(
        functools.partial(_cumsum_kernel, group=group),
        out_shape=jax.ShapeDtypeStruct(x2.shape, F32),
        name="cumsum_rows",
    )(x2)


def _online_softmax_step(s, v_blk, m, l, acc):
    m_new = jnp.maximum(m, jnp.max(s, axis=1, keepdims=True))
    alpha = jnp.exp2(m - m_new)
    p = jnp.exp2(s - m_new)
    l = alpha * l + jnp.sum(p, axis=1, keepdims=True)
    acc = alpha * acc + jnp.dot(p.astype(BF16), v_blk, preferred_element_type=F32)
    return m_new, l, acc


QUERY_GROUP = 256
N_BIAS_PARTS = 3
SUM_ROWS = 16


def _attn_prompt_kernel(q_ref, kb_ref, vt_ref, ct_ref, sga_ref, o_ref,
                        ka_ref, s_ref, p_ref, acc_ref, qa_ref, *, blk):
    kv_sub = s_ref.shape[1]
    seq = q_ref.shape[0]
    n_heads = q_ref.shape[1] // HEAD_DIM
    lane = lax.broadcasted_iota(jnp.int32, (LANES, LANES), 1)

    def head_cols(hh):
        return slice(hh * HEAD_DIM, (hh + 1) * HEAD_DIM)

    def load_keys(c):
        for hh in range(n_heads):
            for sb in range(c * kv_sub // LANES, (c + 1) * kv_sub // LANES):
                rows = slice(sb * LANES, (sb + 1) * LANES)
                rest = jnp.broadcast_to(ct_ref[hh, :, sb:sb + 1] * (-LOG2E), (LANES, LANES))
                extra = jnp.zeros((LANES, LANES), F32)
                for part in range(N_BIAS_PARTS):
                    piece = rest.astype(BF16).astype(F32)
                    extra = jnp.where(lane == part, piece, extra)
                    rest = rest - piece
                ka_ref[hh, rows, :] = jnp.concatenate(
                    [kb_ref[rows, head_cols(hh)], extra.astype(BF16)], axis=1)

    ones_feat = jnp.where(lax.broadcasted_iota(jnp.int32, (blk, LANES), 1) < N_BIAS_PARTS,
                          1.0, 0.0).astype(BF16)
    ones_rows = jnp.ones((SUM_ROWS, kv_sub), BF16)

    def load_queries(qb):
        for hh in range(n_heads):
            qa_ref[hh * blk:(hh + 1) * blk, :] = jnp.concatenate(
                [q_ref[qb * blk:(qb + 1) * blk, head_cols(hh)], ones_feat], axis=1)

    groups_per_head = blk // QUERY_GROUP
    n_groups = n_heads * groups_per_head
    group = [slice(t * QUERY_GROUP, (t + 1) * QUERY_GROUP) for t in range(n_groups)]
    head_of = [t // groups_per_head for t in range(n_groups)]
    order = [hh * groups_per_head + g for g in range(groups_per_head) for hh in range(n_heads)]

    def scores(c, g, n_keys):
        keys = slice(c * kv_sub, c * kv_sub + n_keys)
        return lax.dot_general(ka_ref[head_of[g], keys, :], qa_ref[group[g], :],
                               _NT, preferred_element_type=F32)

    own_tile = (QUERY_GROUP, QUERY_GROUP)
    not_after = (lax.broadcasted_iota(jnp.int32, own_tile, 0)
                 <= lax.broadcasted_iota(jnp.int32, own_tile, 1))

    def causal_tail(s):
        n_before = s.shape[0] - QUERY_GROUP
        tail = jnp.where(not_after, s[n_before:], -jnp.inf)
        return tail if n_before == 0 else jnp.concatenate([s[:n_before], tail], axis=0)

    def softmax(s, m):
        m_new = jnp.maximum(m, jnp.max(s, axis=0, keepdims=True))
        return m_new, jnp.exp2(m - m_new), jnp.exp2(s - m_new).astype(BF16)

    def accumulate(c, g, p, alpha):
        n_keys = p.shape[0]
        vt_sum = jnp.concatenate([vt_ref[c, head_cols(head_of[g]), :][:, :n_keys],
                                  ones_rows[:, :n_keys]], axis=0)
        pv = jnp.dot(vt_sum, p, preferred_element_type=F32)
        acc_ref[:, group[g]] = pv if alpha is None else alpha * acc_ref[:, group[g]] + pv

    def finish(qb):
        rows = slice(qb * blk, (qb + 1) * blk)
        for hh in range(n_heads):
            cols = slice(hh * blk, (hh + 1) * blk)
            out = (acc_ref[:HEAD_DIM, cols] / acc_ref[HEAD_DIM:HEAD_DIM + 1, cols]).T
            o_ref[rows, head_cols(hh)] = (
                out * sga_ref[rows, head_cols(hh)].astype(F32)).astype(BF16)

    chunks_per_block = blk // kv_sub
    items = []
    for qb in range(seq // blk):
        for c in range((qb + 1) * chunks_per_block):
            first_key = c * kv_sub - qb * blk
            n_keys, own = [], []
            for g in range(n_groups):
                first_query = (g % groups_per_head) * QUERY_GROUP
                n_keys.append(min(max(first_query + QUERY_GROUP - first_key, 0), kv_sub))
                own.append(first_key <= first_query < first_key + kv_sub)
            items.append((qb, c, n_keys, own))

    m = [None] * n_groups
    alpha = {}
    for t in range(len(items) + 2):
        qk_item = items[t] if t < len(items) else None
        sm_item = items[t - 1] if 1 <= t <= len(items) else None
        pv_item = items[t - 2] if t >= 2 else None
        if qk_item is not None:
            if qk_item[1] // chunks_per_block == qk_item[0]:
                load_keys(qk_item[1])
            if qk_item[1] == 0:
                load_queries(qk_item[0])
        for g in order:
            if qk_item is not None and qk_item[2][g]:
                qb, c, n_keys, _ = qk_item
                s_ref[t % 2, :n_keys[g], group[g]] = scores(c, g, n_keys[g])
            if pv_item is not None and pv_item[2][g]:
                qb, c, n_keys, _ = pv_item
                accumulate(c, g, p_ref[t % 2, :n_keys[g], group[g]],
                           None if c == 0 else alpha[t - 2, g])
            if sm_item is not None and sm_item[2][g]:
                qb, c, n_keys, own = sm_item
                s = s_ref[(t - 1) % 2, :n_keys[g], group[g]]
                if own[g]:
                    s = causal_tail(s)
                m_old = jnp.full((1, QUERY_GROUP), -jnp.inf, F32) if c == 0 else m[g]
                m[g], alpha[t - 1, g], p_ref[(t - 1) % 2, :n_keys[g], group[g]] = softmax(s, m_old)
        if pv_item is not None and pv_item[1] == (pv_item[0] + 1) * chunks_per_block - 1:
            finish(pv_item[0])


def _attn_prompt(q, kb, vt, cum_t, sga, *, batch, seq, blk, kv_sub, heads_per_step):
    n, d_a = q.shape
    n_heads = d_a // HEAD_DIM
    assert blk % kv_sub == 0 and seq % blk == 0 and n_heads % heads_per_step == 0
    n_chunks = seq // kv_sub
    width = heads_per_step * HEAD_DIM
    wide = heads_per_step * blk
    per_seq = lambda b, h: (b, h)
    return pl.pallas_call(
        functools.partial(_attn_prompt_kernel, blk=blk),
        grid=(batch, n_heads // heads_per_step),
        in_specs=[
            pl.BlockSpec((seq, width), per_seq),
            pl.BlockSpec((seq, width), per_seq),
            pl.BlockSpec((n_chunks, width, kv_sub), lambda b, h: (b, h, 0)),
            pl.BlockSpec((heads_per_step, None, LANES, seq // LANES), lambda b, h: (h, b, 0, 0)),
            pl.BlockSpec((seq, width), per_seq),
        ],
        out_specs=pl.BlockSpec((seq, width), per_seq),
        out_shape=jax.ShapeDtypeStruct((n, d_a), BF16),
        scratch_shapes=[pltpu.VMEM((heads_per_step, seq, HEAD_DIM + LANES), BF16),
                        pltpu.VMEM((2, kv_sub, wide), F32),
                        pltpu.VMEM((2, kv_sub, wide), BF16),
                        pltpu.VMEM((HEAD_DIM + SUM_ROWS, wide), F32),
                        pltpu.VMEM((wide, HEAD_DIM + LANES), BF16)],
        compiler_params=_compiler_params(("parallel", "parallel")),
        name="attn_prompt",
    )(q, kb, vt, cum_t, sga)


def _attn_sample_kernel(q_ref, kn_ref, vn_ref, ck_ref, cv_ref, cp_ref, cn_ref, sga_ref, o_ref,
                        m_ref, l_ref, acc_ref, *, n_heads):
    c = pl.program_id(1)

    @pl.when(c == 0)
    def _():
        m_ref[...] = jnp.full(m_ref.shape, -jnp.inf, F32)
        l_ref[...] = jnp.zeros(l_ref.shape, F32)
        acc_ref[...] = jnp.zeros(acc_ref.shape, F32)

    def head_cols(h):
        return slice(h * HEAD_DIM, (h + 1) * HEAD_DIM)

    chunk = ck_ref.shape[0] // n_heads

    def cached_head(ref, h):
        return ref[pl.ds(h, chunk, stride=n_heads), :].astype(BF16)

    heads = range(n_heads)
    s = [lax.dot_general(q_ref[:, head_cols(h)], cached_head(ck_ref, h), _NT,
                         preferred_element_type=F32) + cp_ref[h:h + 1, :] * (-LOG2E)
         for h in heads]
    m_new = [jnp.maximum(m_ref[h], jnp.max(s[h], axis=1, keepdims=True)) for h in heads]
    p = [jnp.exp2(s[h] - m_new[h]) for h in heads]
    pv = [jnp.dot(p[h].astype(BF16), cached_head(cv_ref, h), preferred_element_type=F32)
          for h in heads]
    for h in heads:
        alpha = jnp.exp2(m_ref[h] - m_new[h])
        l_ref[h] = alpha * l_ref[h] + jnp.sum(p[h], axis=1, keepdims=True)
        acc_ref[h] = alpha * acc_ref[h] + pv[h]
        m_ref[h] = m_new[h]

    @pl.when(c == pl.num_programs(1) - 1)
    def _():
        for h in range(n_heads):
            q = q_ref[:, head_cols(h)]
            s = lax.dot_general(q, kn_ref[:, head_cols(h)], _NT, preferred_element_type=F32)
            s = s + cn_ref[h:h + 1, :] * (-LOG2E)
            rows = lax.broadcasted_iota(jnp.int32, s.shape, 0)
            cols = lax.broadcasted_iota(jnp.int32, s.shape, 1)
            s = jnp.where(cols <= rows, s, -jnp.inf)
            _, l, acc = _online_softmax_step(s, vn_ref[:, head_cols(h)],
                                             m_ref[h], l_ref[h], acc_ref[h])
            o_ref[:, head_cols(h)] = (
                acc / l * sga_ref[:, head_cols(h)].astype(F32)).astype(BF16)


def _attn_sample(q, k_new, v_new, cache_k2, cache_v2, cum_past, cum_new, sga, *,
                 batch, t_new, past, chunk):
    n, d_a = q.shape
    n_heads = d_a // HEAD_DIM
    n_chunks = past // chunk
    new_map = lambda b, c: (b, 0)
    cache_map = lambda b, c: (b * n_chunks + c, 0)
    return pl.pallas_call(
        functools.partial(_attn_sample_kernel, n_heads=n_heads),
        grid=(batch, n_chunks),
        in_specs=[
            pl.BlockSpec((t_new, d_a), new_map),
            pl.BlockSpec((t_new, d_a), new_map),
            pl.BlockSpec((t_new, d_a), new_map),
            pl.BlockSpec((chunk * n_heads, HEAD_DIM), cache_map),
            pl.BlockSpec((chunk * n_heads, HEAD_DIM), cache_map),
            pl.BlockSpec((None, None, n_heads, chunk), lambda b, c: (b, c, 0, 0)),
            pl.BlockSpec((None, n_heads, t_new), lambda b, c: (b, 0, 0)),
            pl.BlockSpec((t_new, d_a), new_map),
        ],
        out_specs=pl.BlockSpec((t_new, d_a), new_map),
        out_shape=jax.ShapeDtypeStruct((n, d_a), BF16),
        scratch_shapes=[pltpu.VMEM((n_heads, t_new, 1), F32),
                        pltpu.VMEM((n_heads, t_new, 1), F32),
                        pltpu.VMEM((n_heads, t_new, HEAD_DIM), F32)],
        compiler_params=_compiler_params(("parallel", "arbitrary")),
        name="attn_sample",
    )(q, k_new, v_new, cache_k2, cache_v2, cum_past, cum_new, sga)


def _out_proj_kernel(oa_ref, ob_ref, w_ref, x_ref, fg_ref, y_ref, *, final_norm):
    o = jnp.concatenate([oa_ref[...], ob_ref[...]], axis=1)
    y = x_ref[...] + jnp.dot(o, w_ref[...], preferred_element_type=F32)
    if final_norm:
        y = _rmsnorm_rows(y, fg_ref[...])
    y_ref[...] = y


def _out_proj(out_a, out_b, w_out, x2, final_g, *, tm, final_norm):
    n, d = x2.shape
    d_a = out_a.shape[1]
    d_b = out_b.shape[1]
    row = lambda i: (i, 0)
    const = lambda i: (0, 0)
    return pl.pallas_call(
        functools.partial(_out_proj_kernel, final_norm=final_norm),
        grid=(n // tm,),
        in_specs=[
            pl.BlockSpec((tm, d_a), row),
            pl.BlockSpec((tm, d_b), row),
            pl.BlockSpec(w_out.shape, const),
            pl.BlockSpec((tm, d), row),
            pl.BlockSpec((1, d), const),
        ],
        out_specs=pl.BlockSpec((tm, d), row),
        out_shape=jax.ShapeDtypeStruct((n, d), F32),
        compiler_params=_compiler_params(("parallel",)),
        name="out_proj",
    )(out_a, out_b, w_out, x2, final_g)


def _pad_rows(a, rows):
    return jnp.pad(a, ((0, rows - a.shape[0]), (0, 0)))


def _mixing_weights(w_s, b_s, seq):
    c = min(seq, GMLP_CHUNK)
    reps = GMLP_CHUNK // c
    w = w_s[:, :c, :c] * jnp.tril(jnp.ones((c, c), w_s.dtype))
    if reps > 1:
        w = jnp.einsum('rs,gab->grasb', jnp.eye(reps, dtype=w.dtype), w)
        w = w.reshape(w_s.shape[0], GMLP_CHUNK, GMLP_CHUNK)
    b = jnp.tile(b_s[:, :c], (1, reps))
    b_full = jnp.broadcast_to(b[:, :, None], (b.shape[0], GMLP_CHUNK, LANES))
    return w.astype(BF16), b_full.astype(F32)


def _split_w_in_kernel(wt_ref, qkv_ref, rest_ref, *, off_rest):
    wt = wt_ref[...]
    qkv_ref[...] = wt[:qkv_ref.shape[1], :].T.astype(BF16)
    rest_ref[...] = wt[off_rest:, :].T.astype(BF16)


def _split_w_in(w_in_t, off_f, off_rest, *, tc=W_SPLIT_COLS):
    d_in, d = w_in_t.shape
    return pl.pallas_call(
        functools.partial(_split_w_in_kernel, off_rest=off_rest),
        grid=(d // tc,),
        in_specs=[pl.BlockSpec((d_in, tc), lambda i: (0, i))],
        out_specs=[pl.BlockSpec((tc, off_f), lambda i: (i, 0)),
                   pl.BlockSpec((tc, d_in - off_rest), lambda i: (i, 0))],
        out_shape=[jax.ShapeDtypeStruct((d, off_f), BF16),
                   jax.ShapeDtypeStruct((d, d_in - off_rest), BF16)],
        compiler_params=_compiler_params(("parallel",)),
        name="split_w_in",
    )(w_in_t)


def _prepare_weights(params):
    norm_g, w_in, b_f, ln_g, ln_b, w_s, b_s, w_out = params
    n_heads = b_f.shape[0]
    off_f = 3 * n_heads * HEAD_DIM
    off_ga = off_f + n_heads
    w_in_t = w_in.T
    w_qkv, w_rest = _split_w_in(w_in_t, off_f, off_ga)
    return dict(
        norm_g=norm_g[None], b_f=b_f[:, None], ln_g=ln_g[None], ln_b=ln_b[None],
        w_qkv=w_qkv, w_rest=w_rest,
        w_ft=_pad_rows(w_in_t[off_f:off_ga], 16).astype(BF16),
        w_out=w_out.astype(BF16), w_s=w_s, b_s=b_s)


def _layer(x, w, caches, final_g, *, final_norm, tm):
    batch, seq, d = x.shape
    n = batch * seq
    n_heads = w['b_f'].shape[0]
    d_b = w['ln_g'].shape[1]

    x2 = x.reshape(n, d)
    w_mix, b_mix = _mixing_weights(w['w_s'], w['b_s'], seq)

    blk, kv_sub = ATTN_QUERY_BLOCK, ATTN_KEY_CHUNK
    h, q, logft = _norm_q(x2, w['norm_g'], w['w_qkv'], w['w_ft'], w['b_f'], tm=tm)
    k, kb, v, vb, sga = _proj_kv(h, w['w_qkv'], w['w_rest'], n_heads=n_heads, tm=tm,
                                 v_sub=kv_sub if caches is None else None)
    gm = _proj_gmlp(h, w['w_rest'], w['ln_g'], w['ln_b'], w_mix, b_mix,
                    tm=tm, emit_vn=caches is not None)

    logf = logft.T.reshape(batch, seq, n_heads)
    if caches is None:
        out_b = gm[0]
        vn = None
        group = seq // LANES
        cum = _cumsum_rows(logft.reshape(n_heads * batch * group, LANES), group)
        cum_t = cum.reshape(n_heads, batch, group, LANES).transpose(0, 1, 3, 2)
        out_a = _attn_prompt(q, kb, vb, cum_t, sga, batch=batch, seq=seq, blk=blk, kv_sub=kv_sub,
                             heads_per_step=ATTN_HEADS_PER_STEP)
    else:
        out_b, vn = gm
        cache_k, cache_v, cache_logf = caches
        past = cache_k.shape[1]
        chunk = CACHE_CHUNK
        total = past + seq
        group = -(-total // LANES)
        lf_all = jnp.concatenate(
            [cache_logf.astype(F32).transpose(0, 2, 1), logf.transpose(0, 2, 1)], axis=2)
        lf_all = jnp.pad(lf_all, ((0, 0), (0, 0), (0, group * LANES - total)))
        cum = _cumsum_rows(lf_all.reshape(batch * n_heads * group, LANES), group)
        cum = cum.reshape(batch, n_heads, group * LANES)
        cum_past = cum[:, :, :past].reshape(batch, n_heads, past // chunk, chunk)
        cum_past = cum_past.transpose(0, 2, 1, 3)
        cum_new = cum[:, :, past:total]
        out_a = _attn_sample(q, kb, vb, cache_k.reshape(batch * past * n_heads, HEAD_DIM),
                             cache_v.reshape(batch * past * n_heads, HEAD_DIM),
                             cum_past, cum_new, sga,
                             batch=batch, t_new=seq, past=past, chunk=chunk)

    y2 = _out_proj(out_a, out_b, w['w_out'], x2, final_g[None], tm=tm, final_norm=final_norm)
    y = y2.reshape(batch, seq, d)
    k4 = k.reshape(batch, seq, n_heads, HEAD_DIM)
    v4 = v.reshape(batch, seq, n_heads, HEAD_DIM)
    vn3 = None if vn is None else vn.reshape(batch, seq, d_b)
    return y, k4, v4, logf, vn3


def kernel(x_prompt, x_sample, cache_k, cache_v, cache_logf, norm_g, w_in, b_f, ln_g, ln_b,
           w_s, b_s, w_out, final_g):
    depth = norm_g.shape[0]
    hp, hs = x_prompt, x_sample
    kp, vp, fp, ksm, vsm, fsm, gsm = [], [], [], [], [], [], []
    n_sample = x_sample.shape[0] * x_sample.shape[1]
    for l in range(depth):
        w = _prepare_weights(
            (norm_g[l], w_in[l], b_f[l], ln_g[l], ln_b[l], w_s[l], b_s[l], w_out[l]))
        last = l == depth - 1
        hp, k1, v1, f1, _ = _layer(hp, w, None, final_g, final_norm=last, tm=ROW_TILE)
        hs, k2, v2, f2, g2 = _layer(hs, w, (cache_k[l], cache_v[l], cache_logf[l]), final_g,
                                    final_norm=last, tm=n_sample)
        kp.append(k1); vp.append(v1); fp.append(f1)
        ksm.append(k2); vsm.append(v2); fsm.append(f2); gsm.append(g2)
    return (hp, hs, jnp.stack(kp), jnp.stack(vp), jnp.stack(fp),
            jnp.stack(ksm), jnp.stack(vsm), jnp.stack(fsm), jnp.stack(gsm))
```

```python
import functools
import math

import jax
import jax.numpy as jnp
from jax import lax
from jax.experimental import pallas as pl
from jax.experimental.pallas import tpu as pltpu

F32 = jnp.float32
BF16 = jnp.bfloat16

HEAD_DIM = 128
GMLP_CHUNK = 128
RMS_EPS = 1e-6
LN_EPS = 1e-5
LOG2E = math.log2(math.e)
LANES = 128
VMEM_LIMIT_BYTES = 48 * 1024 * 1024

ROW_TILE = 512
V_T_CHUNK = 512
ATTN_QUERY_BLOCK = 1024
ATTN_KEY_CHUNK = 1024
ATTN_HEADS_PER_STEP = 1
CACHE_CHUNK = 2048
W_SPLIT_COLS = 256

_NT = (((1,), (1,)), ((), ()))


def _compiler_params(semantics):
    return pltpu.CompilerParams(dimension_semantics=semantics,
                                vmem_limit_bytes=VMEM_LIMIT_BYTES)


def _rmsnorm_rows(x, g):
    return x * lax.rsqrt(jnp.mean(x * x, axis=-1, keepdims=True) + RMS_EPS) * g


def _gelu_tanh(x):
    c = math.sqrt(2.0 / math.pi)
    return 0.5 * x * (1.0 + jnp.tanh(c * (x + 0.044715 * (x * x * x))))


def _silu(x):
    return x * jax.nn.sigmoid(x)


def _store_heads_on_sublanes(ref, z, n_heads):
    rows = z.shape[0]
    for h in range(n_heads):
        ref[pl.ds(h, rows, stride=n_heads), :] = z[:, h * HEAD_DIM:(h + 1) * HEAD_DIM]


def _project(h, w_ref):
    return jnp.dot(h, w_ref[...], preferred_element_type=F32)


def _resident(shape, block_index):
    return pl.BlockSpec(shape, lambda i: block_index, pipeline_mode=pl.Buffered(1))


def _norm_q_kernel(x_ref, g_ref, wq_ref, wft_ref, bf_ref, h_ref, q_ref, logft_ref,
                   *, q_scale, n_heads):
    hb = _rmsnorm_rows(x_ref[...], g_ref[...]).astype(BF16)
    h_ref[...] = hb
    q_ref[...] = (_project(hb, wq_ref) * q_scale).astype(BF16)
    zf = lax.dot_general(wft_ref[...], hb, _NT, preferred_element_type=F32)
    t = zf[:n_heads] + bf_ref[...]
    logft_ref[...] = jnp.minimum(t, 0.0) - jnp.log1p(jnp.exp(-jnp.abs(t)))


def _norm_q(x2, norm_g, w_qkv, w_ft, b_f, *, tm):
    n, d = x2.shape
    n_heads = b_f.shape[0]
    d_a = n_heads * HEAD_DIM
    row = lambda i: (i, 0)
    return pl.pallas_call(
        functools.partial(_norm_q_kernel, q_scale=HEAD_DIM ** -0.5 * LOG2E, n_heads=n_heads),
        grid=(n // tm,),
        in_specs=[
            pl.BlockSpec((tm, d), row),
            _resident((1, d), (0, 0)),
            _resident((d, d_a), (0, 0)),
            _resident(w_ft.shape, (0, 0)),
            _resident((n_heads, 1), (0, 0)),
        ],
        out_specs=[
            pl.BlockSpec((tm, d), row),
            pl.BlockSpec((tm, d_a), row),
            pl.BlockSpec((n_heads, tm), lambda i: (0, i)),
        ],
        out_shape=[
            jax.ShapeDtypeStruct((n, d), BF16),
            jax.ShapeDtypeStruct((n, d_a), BF16),
            jax.ShapeDtypeStruct((n_heads, n), F32),
        ],
        compiler_params=_compiler_params(("parallel",)),
        name="norm_q",
    )(x2, norm_g, w_qkv, w_ft, b_f)


def _proj_kv_kernel(h_ref, wk_ref, wv_ref, wga_ref, k_ref, kb_ref, v_ref, vb_ref, sga_ref,
                    *, n_heads, v_sub):
    h = h_ref[...]
    z = _project(h, wk_ref)
    _store_heads_on_sublanes(k_ref, z, n_heads)
    kb_ref[...] = z.astype(BF16)
    z = _project(h, wv_ref)
    _store_heads_on_sublanes(v_ref, z, n_heads)
    if v_sub is None:
        vb_ref[...] = z.astype(BF16)
    else:
        for c in range(z.shape[0] // v_sub):
            vb_ref[c] = z[c * v_sub:(c + 1) * v_sub, :].T.astype(BF16)
    sga_ref[...] = _silu(_project(h, wga_ref)).astype(BF16)


def _proj_kv(h, w_qkv, w_rest, *, n_heads, tm, v_sub=None):
    n, d = h.shape
    d_a = n_heads * HEAD_DIM
    row = lambda i: (i, 0)
    if v_sub is None:
        vb_spec = pl.BlockSpec((tm, d_a), row)
        vb_shape = jax.ShapeDtypeStruct((n, d_a), BF16)
    else:
        vb_spec = pl.BlockSpec((tm // v_sub, d_a, v_sub), lambda i: (i, 0, 0))
        vb_shape = jax.ShapeDtypeStruct((n // v_sub, d_a, v_sub), BF16)
    return pl.pallas_call(
        functools.partial(_proj_kv_kernel, n_heads=n_heads, v_sub=v_sub),
        grid=(n // tm,),
        in_specs=[
            pl.BlockSpec((tm, d), row),
            _resident((d, d_a), (0, 1)),
            _resident((d, d_a), (0, 2)),
            _resident((d, d_a), (0, 0)),
        ],
        out_specs=[
            pl.BlockSpec((tm * n_heads, HEAD_DIM), row),
            pl.BlockSpec((tm, d_a), row),
            pl.BlockSpec((tm * n_heads, HEAD_DIM), row),
            vb_spec,
            pl.BlockSpec((tm, d_a), row),
        ],
        out_shape=[
            jax.ShapeDtypeStruct((n * n_heads, HEAD_DIM), F32),
            jax.ShapeDtypeStruct((n, d_a), BF16),
            jax.ShapeDtypeStruct((n * n_heads, HEAD_DIM), F32),
            vb_shape,
            jax.ShapeDtypeStruct((n, d_a), BF16),
        ],
        compiler_params=_compiler_params(("parallel",)),
        name="proj_kv",
    )(h, w_qkv, w_qkv, w_rest)


def _proj_gmlp_kernel(h_ref, wu_ref, wv_ref, wg_ref, lng_ref, lnb_ref, wmix_ref, bmix_ref,
                      ob_ref, *rest, n_groups, emit_vn):
    if emit_vn:
        vn_ref, mix_ref, gu_ref = rest
    else:
        mix_ref, gu_ref = rest
    h = h_ref[...]
    tm = h.shape[0]
    n_chunks = tm // GMLP_CHUNK
    d_g = wv_ref.shape[1] // n_groups

    a = _gelu_tanh(_project(h, wv_ref))
    gu_ref[...] = _gelu_tanh(_project(h, wu_ref))
    mu = jnp.mean(a, axis=-1, keepdims=True)
    ac = a - mu
    var = jnp.mean(ac * ac, axis=-1, keepdims=True)
    vn = ac * lax.rsqrt(var + LN_EPS) * lng_ref[...] + lnb_ref[...]
    if emit_vn:
        vn_ref[...] = vn
    vb = vn.astype(BF16)
    for g in range(n_groups):
        cols = slice(g * d_g, (g + 1) * d_g)
        rhs = jnp.concatenate(
            [vb[r * GMLP_CHUNK:(r + 1) * GMLP_CHUNK, cols] for r in range(n_chunks)], axis=1)
        mixed = jnp.dot(wmix_ref[g], rhs, preferred_element_type=F32)
        for r in range(n_chunks):
            mix_ref[r * GMLP_CHUNK:(r + 1) * GMLP_CHUNK, cols] = (
                mixed[:, r * d_g:(r + 1) * d_g] + bmix_ref[g])

    gate = _silu(_project(h, wg_ref))
    ob_ref[...] = (gu_ref[...] * mix_ref[...] * gate).astype(BF16)


def _proj_gmlp(h, w_rest, ln_g, ln_b, w_mix, b_mix, *, tm, emit_vn):
    n, d = h.shape
    d_b = ln_g.shape[1]
    n_groups = w_mix.shape[0]
    row = lambda i: (i, 0)
    out_specs = [pl.BlockSpec((tm, d_b), row)]
    out_shape = [jax.ShapeDtypeStruct((n, d_b), BF16)]
    if emit_vn:
        out_specs.append(pl.BlockSpec((tm, d_b), row))
        out_shape.append(jax.ShapeDtypeStruct((n, d_b), F32))
    return pl.pallas_call(
        functools.partial(_proj_gmlp_kernel, n_groups=n_groups, emit_vn=emit_vn),
        grid=(n // tm,),
        in_specs=[
            pl.BlockSpec((tm, d), row),
            _resident((d, d_b), (0, 1)),
            _resident((d, d_b), (0, 2)),
            _resident((d, d_b), (0, 3)),
            _resident((1, d_b), (0, 0)),
            _resident((1, d_b), (0, 0)),
            _resident(w_mix.shape, (0, 0, 0)),
            _resident(b_mix.shape, (0, 0, 0)),
        ],
        out_specs=out_specs,
        out_shape=out_shape,
        scratch_shapes=[pltpu.VMEM((tm, d_b), F32), pltpu.VMEM((tm, d_b), F32)],
        compiler_params=_compiler_params(("parallel",)),
        name="proj_gmlp",
    )(h, w_rest, w_rest, w_rest, ln_g, ln_b, w_mix, b_mix)


def _cumsum_kernel(x_ref, o_ref, *, group):
    x = x_ref[...]
    lane = lax.broadcasted_iota(jnp.int32, x.shape, 1)
    sh = 1
    while sh < LANES:
        x = x + jnp.where(lane >= sh, pltpu.roll(x, sh, axis=1), 0.0)
        sh *= 2
    sub = lax.broadcasted_iota(jnp.int32, x.shape, 0) % group
    tot = jnp.broadcast_to(x[:, LANES - 1:LANES], x.shape)
    carry = jnp.where(sub >= 1, pltpu.roll(tot, 1, axis=0), 0.0)
    sh = 1
    while sh < group:
        carry = carry + jnp.where(sub >= sh, pltpu.roll(carry, sh, axis=0), 0.0)
        sh *= 2
    o_ref[...] = x + carry


def _cumsum_rows(x2, group):
    return pl.pallas_call(
        functools.partial(_cumsum_kernel, group=group),
        out_shape=jax.ShapeDtypeStruct(x2.shape, F32),
        name="cumsum_rows",
    )(x2)


def _online_softmax_step(s, v_blk, m, l, acc):
    m_new = jnp.maximum(m, jnp.max(s, axis=1, keepdims=True))
    alpha = jnp.exp2(m - m_new)
    p = jnp.exp2(s - m_new)
    l = alpha * l + jnp.sum(p, axis=1, keepdims=True)
    acc = alpha * acc + jnp.dot(p.astype(BF16), v_blk, preferred_element_type=F32)
    return m_new, l, acc


QUERY_GROUP = 256
N_BIAS_PARTS = 3
SUM_ROWS = 16


def _attn_prompt_kernel(q_ref, kb_ref, vt_ref, ct_ref, sga_ref, o_ref,
                        ka_ref, s_ref, p_ref, acc_ref, qa_ref, *, blk):
    kv_sub = s_ref.shape[1]
    seq = q_ref.shape[0]
    n_heads = q_ref.shape[1] // HEAD_DIM
    lane = lax.broadcasted_iota(jnp.int32, (LANES, LANES), 1)

    def head_cols(hh):
        return slice(hh * HEAD_DIM, (hh + 1) * HEAD_DIM)

    def load_keys(c):
        for hh in range(n_heads):
            for sb in range(c * kv_sub // LANES, (c + 1) * kv_sub // LANES):
                rows = slice(sb * LANES, (sb + 1) * LANES)
                rest = jnp.broadcast_to(ct_ref[hh, :, sb:sb + 1] * (-LOG2E), (LANES, LANES))
                extra = jnp.zeros((LANES, LANES), F32)
                for part in range(N_BIAS_PARTS):
                    piece = rest.astype(BF16).astype(F32)
                    extra = jnp.where(lane == part, piece, extra)
                    rest = rest - piece
                ka_ref[hh, rows, :] = jnp.concatenate(
                    [kb_ref[rows, head_cols(hh)], extra.astype(BF16)], axis=1)

    ones_feat = jnp.where(lax.broadcasted_iota(jnp.int32, (blk, LANES), 1) < N_BIAS_PARTS,
                          1.0, 0.0).astype(BF16)
    ones_rows = jnp.ones((SUM_ROWS, kv_sub), BF16)

    def load_queries(qb):
        for hh in range(n_heads):
            qa_ref[hh * blk:(hh + 1) * blk, :] = jnp.concatenate(
                [q_ref[qb * blk:(qb + 1) * blk, head_cols(hh)], ones_feat], axis=1)

    groups_per_head = blk // QUERY_GROUP
    n_groups = n_heads * groups_per_head
    group = [slice(t * QUERY_GROUP, (t + 1) * QUERY_GROUP) for t in range(n_groups)]
    head_of = [t // groups_per_head for t in range(n_groups)]
    order = [hh * groups_per_head + g for g in range(groups_per_head) for hh in range(n_heads)]

    def scores(c, g, n_keys):
        keys = slice(c * kv_sub, c * kv_sub + n_keys)
        return lax.dot_general(ka_ref[head_of[g], keys, :], qa_ref[group[g], :],
                               _NT, preferred_element_type=F32)

    own_tile = (QUERY_GROUP, QUERY_GROUP)
    not_after = (lax.broadcasted_iota(jnp.int32, own_tile, 0)
                 <= lax.broadcasted_iota(jnp.int32, own_tile, 1))

    def causal_tail(s):
        n_before = s.shape[0] - QUERY_GROUP
        tail = jnp.where(not_after, s[n_before:], -jnp.inf)
        return tail if n_before == 0 else jnp.concatenate([s[:n_before], tail], axis=0)

    def softmax(s, m):
        m_new = jnp.maximum(m, jnp.max(s, axis=0, keepdims=True))
        return m_new, jnp.exp2(m - m_new), jnp.exp2(s - m_new).astype(BF16)

    v_chunk = vt_ref.shape[2]

    def values_t(c, hh, n_keys):
        first = c * kv_sub
        parts = []
        while n_keys > 0:
            i, off = divmod(first, v_chunk)
            take = min(v_chunk - off, n_keys)
            parts.append(vt_ref[i, head_cols(hh), off:off + take])
            first += take
            n_keys -= take
        return parts[0] if len(parts) == 1 else jnp.concatenate(parts, axis=1)

    def accumulate(c, g, p, alpha):
        n_keys = p.shape[0]
        vt_sum = jnp.concatenate([values_t(c, head_of[g], n_keys), ones_rows[:, :n_keys]],
                                 axis=0)
        pv = jnp.dot(vt_sum, p, preferred_element_type=F32)
        acc_ref[:, group[g]] = pv if alpha is None else alpha * acc_ref[:, group[g]] + pv

    def finish(qb):
        rows = slice(qb * blk, (qb + 1) * blk)
        for hh in range(n_heads):
            cols = slice(hh * blk, (hh + 1) * blk)
            out = (acc_ref[:HEAD_DIM, cols] / acc_ref[HEAD_DIM:HEAD_DIM + 1, cols]).T
            o_ref[rows, head_cols(hh)] = (
                out * sga_ref[rows, head_cols(hh)].astype(F32)).astype(BF16)

    chunks_per_block = blk // kv_sub
    items = []
    for qb in range(seq // blk):
        for c in range((qb + 1) * chunks_per_block):
            first_key = c * kv_sub - qb * blk
            n_keys, own = [], []
            for g in range(n_groups):
                first_query = (g % groups_per_head) * QUERY_GROUP
                n_keys.append(min(max(first_query + QUERY_GROUP - first_key, 0), kv_sub))
                own.append(first_key <= first_query < first_key + kv_sub)
            items.append((qb, c, n_keys, own))

    m = [None] * n_groups
    alpha = {}
    for t in range(len(items) + 2):
        qk_item = items[t] if t < len(items) else None
        sm_item = items[t - 1] if 1 <= t <= len(items) else None
        pv_item = items[t - 2] if t >= 2 else None
        if qk_item is not None:
            if qk_item[1] // chunks_per_block == qk_item[0]:
                load_keys(qk_item[1])
            if qk_item[1] == 0:
                load_queries(qk_item[0])
        for g in order:
            if qk_item is not None and qk_item[2][g]:
                qb, c, n_keys, _ = qk_item
                s_ref[t % 2, :n_keys[g], group[g]] = scores(c, g, n_keys[g])
            if pv_item is not None and pv_item[2][g]:
                qb, c, n_keys, _ = pv_item
                accumulate(c, g, p_ref[t % 2, :n_keys[g], group[g]],
                           None if c == 0 else alpha[t - 2, g])
            if sm_item is not None and sm_item[2][g]:
                qb, c, n_keys, own = sm_item
                s = s_ref[(t - 1) % 2, :n_keys[g], group[g]]
                if own[g]:
                    s = causal_tail(s)
                m_old = jnp.full((1, QUERY_GROUP), -jnp.inf, F32) if c == 0 else m[g]
                m[g], alpha[t - 1, g], p_ref[(t - 1) % 2, :n_keys[g], group[g]] = softmax(s, m_old)
        if pv_item is not None and pv_item[1] == (pv_item[0] + 1) * chunks_per_block - 1:
            finish(pv_item[0])


def _attn_prompt(q, kb, vt, cum_t, sga, *, batch, seq, blk, kv_sub, heads_per_step):
    n, d_a = q.shape
    n_heads = d_a // HEAD_DIM
    v_chunk = vt.shape[2]
    assert blk % kv_sub == 0 and seq % blk == 0 and n_heads % heads_per_step == 0
    assert seq % v_chunk == 0 and v_chunk % QUERY_GROUP == 0
    width = heads_per_step * HEAD_DIM
    wide = heads_per_step * blk
    per_seq = lambda b, h: (b, h)
    return pl.pallas_call(
        functools.partial(_attn_prompt_kernel, blk=blk),
        grid=(batch, n_heads // heads_per_step),
        in_specs=[
            pl.BlockSpec((seq, width), per_seq),
            pl.BlockSpec((seq, width), per_seq),
            pl.BlockSpec((seq // v_chunk, width, v_chunk), lambda b, h: (b, h, 0)),
            pl.BlockSpec((heads_per_step, None, LANES, seq // LANES), lambda b, h: (h, b, 0, 0)),
            pl.BlockSpec((seq, width), per_seq),
        ],
        out_specs=pl.BlockSpec((seq, width), per_seq),
        out_shape=jax.ShapeDtypeStruct((n, d_a), BF16),
        scratch_shapes=[pltpu.VMEM((heads_per_step, seq, HEAD_DIM + LANES), BF16),
                        pltpu.VMEM((2, kv_sub, wide), F32),
                        pltpu.VMEM((2, kv_sub, wide), BF16),
                        pltpu.VMEM((HEAD_DIM + SUM_ROWS, wide), F32),
                        pltpu.VMEM((wide, HEAD_DIM + LANES), BF16)],
        compiler_params=_compiler_params(("parallel", "parallel")),
        name="attn_prompt",
    )(q, kb, vt, cum_t, sga)


def _attn_sample_kernel(q_ref, kn_ref, vn_ref, ck_ref, cv_ref, cp_ref, cn_ref, sga_ref, o_ref,
                        m_ref, l_ref, acc_ref, *, n_heads):
    c = pl.program_id(1)

    @pl.when(c == 0)
    def _():
        m_ref[...] = jnp.full(m_ref.shape, -jnp.inf, F32)
        l_ref[...] = jnp.zeros(l_ref.shape, F32)
        acc_ref[...] = jnp.zeros(acc_ref.shape, F32)

    def head_cols(h):
        return slice(h * HEAD_DIM, (h + 1) * HEAD_DIM)

    chunk = ck_ref.shape[0] // n_heads

    def cached_head(ref, h):
        return ref[pl.ds(h, chunk, stride=n_heads), :].astype(BF16)

    heads = range(n_heads)
    s = [lax.dot_general(q_ref[:, head_cols(h)], cached_head(ck_ref, h), _NT,
                         preferred_element_type=F32) + cp_ref[h:h + 1, :] * (-LOG2E)
         for h in heads]
    m_new = [jnp.maximum(m_ref[h], jnp.max(s[h], axis=1, keepdims=True)) for h in heads]
    p = [jnp.exp2(s[h] - m_new[h]) for h in heads]
    pv = [jnp.dot(p[h].astype(BF16), cached_head(cv_ref, h), preferred_element_type=F32)
          for h in heads]
    for h in heads:
        alpha = jnp.exp2(m_ref[h] - m_new[h])
        l_ref[h] = alpha * l_ref[h] + jnp.sum(p[h], axis=1, keepdims=True)
        acc_ref[h] = alpha * acc_ref[h] + pv[h]
        m_ref[h] = m_new[h]

    @pl.when(c == pl.num_programs(1) - 1)
    def _():
        for h in range(n_heads):
            q = q_ref[:, head_cols(h)]
            s = lax.dot_general(q, kn_ref[:, head_cols(h)], _NT, preferred_element_type=F32)
            s = s + cn_ref[h:h + 1, :] * (-LOG2E)
            rows = lax.broadcasted_iota(jnp.int32, s.shape, 0)
            cols = lax.broadcasted_iota(jnp.int32, s.shape, 1)
            s = jnp.where(cols <= rows, s, -jnp.inf)
            _, l, acc = _online_softmax_step(s, vn_ref[:, head_cols(h)],
                                             m_ref[h], l_ref[h], acc_ref[h])
            o_ref[:, head_cols(h)] = (
                acc / l * sga_ref[:, head_cols(h)].astype(F32)).astype(BF16)


def _attn_sample(q, k_new, v_new, cache_k2, cache_v2, cum_past, cum_new, sga, *,
                 batch, t_new, past, chunk):
    n, d_a = q.shape
    n_heads = d_a // HEAD_DIM
    n_chunks = past // chunk
    new_map = lambda b, c: (b, 0)
    cache_map = lambda b, c: (b * n_chunks + c, 0)
    return pl.pallas_call(
        functools.partial(_attn_sample_kernel, n_heads=n_heads),
        grid=(batch, n_chunks),
        in_specs=[
            pl.BlockSpec((t_new, d_a), new_map),
            pl.BlockSpec((t_new, d_a), new_map),
            pl.BlockSpec((t_new, d_a), new_map),
            pl.BlockSpec((chunk * n_heads, HEAD_DIM), cache_map),
            pl.BlockSpec((chunk * n_heads, HEAD_DIM), cache_map),
            pl.BlockSpec((None, None, n_heads, chunk), lambda b, c: (b, c, 0, 0)),
            pl.BlockSpec((None, n_heads, t_new), lambda b, c: (b, 0, 0)),
            pl.BlockSpec((t_new, d_a), new_map),
        ],
        out_specs=pl.BlockSpec((t_new, d_a), new_map),
        out_shape=jax.ShapeDtypeStruct((n, d_a), BF16),
        scratch_shapes=[pltpu.VMEM((n_heads, t_new, 1), F32),
                        pltpu.VMEM((n_heads, t_new, 1), F32),
                        pltpu.VMEM((n_heads, t_new, HEAD_DIM), F32)],
        compiler_params=_compiler_params(("parallel", "arbitrary")),
        name="attn_sample",
    )(q, k_new, v_new, cache_k2, cache_v2, cum_past, cum_new, sga)


def _out_proj_kernel(oa_ref, ob_ref, w_ref, x_ref, fg_ref, y_ref, *, final_norm):
    o = jnp.concatenate([oa_ref[...], ob_ref[...]], axis=1)
    y = x_ref[...] + jnp.dot(o, w_ref[...], preferred_element_type=F32)
    if final_norm:
        y = _rmsnorm_rows(y, fg_ref[...])
    y_ref[...] = y


def _out_proj(out_a, out_b, w_out, x2, final_g, *, tm, final_norm):
    n, d = x2.shape
    d_a = out_a.shape[1]
    d_b = out_b.shape[1]
    row = lambda i: (i, 0)
    const = lambda i: (0, 0)
    return pl.pallas_call(
        functools.partial(_out_proj_kernel, final_norm=final_norm),
        grid=(n // tm,),
        in_specs=[
            pl.BlockSpec((tm, d_a), row),
            pl.BlockSpec((tm, d_b), row),
            pl.BlockSpec(w_out.shape, const),
            pl.BlockSpec((tm, d), row),
            pl.BlockSpec((1, d), const),
        ],
        out_specs=pl.BlockSpec((tm, d), row),
        out_shape=jax.ShapeDtypeStruct((n, d), F32),
        compiler_params=_compiler_params(("parallel",)),
        name="out_proj",
    )(out_a, out_b, w_out, x2, final_g)


def _pad_rows(a, rows):
    return jnp.pad(a, ((0, rows - a.shape[0]), (0, 0)))


def _mixing_weights(w_s, b_s, seq):
    c = min(seq, GMLP_CHUNK)
    reps = GMLP_CHUNK // c
    w = w_s[:, :c, :c] * jnp.tril(jnp.ones((c, c), w_s.dtype))
    if reps > 1:
        w = jnp.einsum('rs,gab->grasb', jnp.eye(reps, dtype=w.dtype), w)
        w = w.reshape(w_s.shape[0], GMLP_CHUNK, GMLP_CHUNK)
    b = jnp.tile(b_s[:, :c], (1, reps))
    b_full = jnp.broadcast_to(b[:, :, None], (b.shape[0], GMLP_CHUNK, LANES))
    return w.astype(BF16), b_full.astype(F32)


def _split_w_in_kernel(wt_ref, qkv_ref, rest_ref, *, off_rest):
    wt = wt_ref[...]
    qkv_ref[...] = wt[:qkv_ref.shape[1], :].T.astype(BF16)
    rest_ref[...] = wt[off_rest:, :].T.astype(BF16)


def _split_w_in(w_in_t, off_f, off_rest, *, tc=W_SPLIT_COLS):
    d_in, d = w_in_t.shape
    return pl.pallas_call(
        functools.partial(_split_w_in_kernel, off_rest=off_rest),
        grid=(d // tc,),
        in_specs=[pl.BlockSpec((d_in, tc), lambda i: (0, i))],
        out_specs=[pl.BlockSpec((tc, off_f), lambda i: (i, 0)),
                   pl.BlockSpec((tc, d_in - off_rest), lambda i: (i, 0))],
        out_shape=[jax.ShapeDtypeStruct((d, off_f), BF16),
                   jax.ShapeDtypeStruct((d, d_in - off_rest), BF16)],
        compiler_params=_compiler_params(("parallel",)),
        name="split_w_in",
    )(w_in_t)


def _prepare_weights(params):
    norm_g, w_in, b_f, ln_g, ln_b, w_s, b_s, w_out = params
    n_heads = b_f.shape[0]
    off_f = 3 * n_heads * HEAD_DIM
    off_ga = off_f + n_heads
    w_in_t = w_in.T
    w_qkv, w_rest = _split_w_in(w_in_t, off_f, off_ga)
    return dict(
        norm_g=norm_g[None], b_f=b_f[:, None], ln_g=ln_g[None], ln_b=ln_b[None],
        w_qkv=w_qkv, w_rest=w_rest,
        w_ft=_pad_rows(w_in_t[off_f:off_ga], 16).astype(BF16),
        w_out=w_out.astype(BF16), w_s=w_s, b_s=b_s)


def _layer(x, w, caches, final_g, *, final_norm, tm):
    batch, seq, d = x.shape
    n = batch * seq
    n_heads = w['b_f'].shape[0]
    d_b = w['ln_g'].shape[1]

    x2 = x.reshape(n, d)
    w_mix, b_mix = _mixing_weights(w['w_s'], w['b_s'], seq)

    blk, kv_sub = ATTN_QUERY_BLOCK, ATTN_KEY_CHUNK
    h, q, logft = _norm_q(x2, w['norm_g'], w['w_qkv'], w['w_ft'], w['b_f'], tm=tm)
    k, kb, v, vb, sga = _proj_kv(h, w['w_qkv'], w['w_rest'], n_heads=n_heads, tm=tm,
                                 v_sub=V_T_CHUNK if caches is None else None)
    gm = _proj_gmlp(h, w['w_rest'], w['ln_g'], w['ln_b'], w_mix, b_mix,
                    tm=tm, emit_vn=caches is not None)

    logf = logft.T.reshape(batch, seq, n_heads)
    if caches is None:
        out_b = gm[0]
        vn = None
        group = seq // LANES
        cum = _cumsum_rows(logft.reshape(n_heads * batch * group, LANES), group)
        cum_t = cum.reshape(n_heads, batch, group, LANES).transpose(0, 1, 3, 2)
        out_a = _attn_prompt(q, kb, vb, cum_t, sga, batch=batch, seq=seq, blk=blk, kv_sub=kv_sub,
                             heads_per_step=ATTN_HEADS_PER_STEP)
    else:
        out_b, vn = gm
        cache_k, cache_v, cache_logf = caches
        past = cache_k.shape[1]
        chunk = CACHE_CHUNK
        total = past + seq
        group = -(-total // LANES)
        lf_all = jnp.concatenate(
            [cache_logf.astype(F32).transpose(0, 2, 1), logf.transpose(0, 2, 1)], axis=2)
        lf_all = jnp.pad(lf_all, ((0, 0), (0, 0), (0, group * LANES - total)))
        cum = _cumsum_rows(lf_all.reshape(batch * n_heads * group, LANES), group)
        cum = cum.reshape(batch, n_heads, group * LANES)
        cum_past = cum[:, :, :past].reshape(batch, n_heads, past // chunk, chunk)
        cum_past = cum_past.transpose(0, 2, 1, 3)
        cum_new = cum[:, :, past:total]
        out_a = _attn_sample(q, kb, vb, cache_k.reshape(batch * past * n_heads, HEAD_DIM),
                             cache_v.reshape(batch * past * n_heads, HEAD_DIM),
                             cum_past, cum_new, sga,
                             batch=batch, t_new=seq, past=past, chunk=chunk)

    y2 = _out_proj(out_a, out_b, w['w_out'], x2, final_g[None], tm=tm, final_norm=final_norm)
    y = y2.reshape(batch, seq, d)
    k4 = k.reshape(batch, seq, n_heads, HEAD_DIM)
    v4 = v.reshape(batch, seq, n_heads, HEAD_DIM)
    vn3 = None if vn is None else vn.reshape(batch, seq, d_b)
    return y, k4, v4, logf, vn3


def kernel(x_prompt, x_sample, cache_k, cache_v, cache_logf, norm_g, w_in, b_f, ln_g, ln_b,
           w_s, b_s, w_out, final_g):
    depth = norm_g.shape[0]
    hp, hs = x_prompt, x_sample
    kp, vp, fp, ksm, vsm, fsm, gsm = [], [], [], [], [], [], []
    n_sample = x_sample.shape[0] * x_sample.shape[1]
    for l in range(depth):
        w = _prepare_weights(
            (norm_g[l], w_in[l], b_f[l], ln_g[l], ln_b[l], w_s[l], b_s[l], w_out[l]))
        last = l == depth - 1
        hp, k1, v1, f1, _ = _layer(hp, w, None, final_g, final_norm=last, tm=ROW_TILE)
        hs, k2, v2, f2, g2 = _layer(hs, w, (cache_k[l], cache_v[l], cache_logf[l]), final_g,
                                    final_norm=last, tm=n_sample)
        kp.append(k1); vp.append(v1); fp.append(f1)
        ksm.append(k2); vsm.append(v2); fsm.append(f2); gsm.append(g2)
    return (hp, hs, jnp.stack(kp), jnp.stack(vp), jnp.stack(fp),
            jnp.stack(ksm), jnp.stack(vsm), jnp.stack(fsm), jnp.stack(gsm))
```

```python
import functools
import math

import jax
import jax.numpy as jnp
from jax import lax
from jax.experimental import pallas as pl
from jax.experimental.pallas import tpu as pltpu

F32 = jnp.float32
BF16 = jnp.bfloat16

HEAD_DIM = 128
GMLP_CHUNK = 128
RMS_EPS = 1e-6
LN_EPS = 1e-5
LOG2E = math.log2(math.e)
LANES = 128
VMEM_LIMIT_BYTES = 48 * 1024 * 1024

ROW_TILE = 512
V_T_CHUNK = 512
ATTN_QUERY_BLOCK = 512
ATTN_KEY_CHUNK = 512
ATTN_HEADS_PER_STEP = 1
CACHE_CHUNK = 2048
W_SPLIT_COLS = 256

_NT = (((1,), (1,)), ((), ()))


def _compiler_params(semantics):
    return pltpu.CompilerParams(dimension_semantics=semantics,
                                vmem_limit_bytes=VMEM_LIMIT_BYTES)


def _rmsnorm_rows(x, g):
    return x * lax.rsqrt(jnp.mean(x * x, axis=-1, keepdims=True) + RMS_EPS) * g


def _gelu_tanh(x):
    c = math.sqrt(2.0 / math.pi)
    return 0.5 * x * (1.0 + jnp.tanh(c * (x + 0.044715 * (x * x * x))))


def _silu(x):
    return x * jax.nn.sigmoid(x)


def _store_heads_on_sublanes(ref, z, n_heads):
    rows = z.shape[0]
    for h in range(n_heads):
        ref[pl.ds(h, rows, stride=n_heads), :] = z[:, h * HEAD_DIM:(h + 1) * HEAD_DIM]


def _project(h, w_ref):
    return jnp.dot(h, w_ref[...], preferred_element_type=F32)


def _resident(shape, block_index):
    return pl.BlockSpec(shape, lambda i: block_index, pipeline_mode=pl.Buffered(1))


def _norm_q_kernel(x_ref, g_ref, wq_ref, wft_ref, bf_ref, h_ref, q_ref, logft_ref,
                   *, q_scale, n_heads):
    hb = _rmsnorm_rows(x_ref[...], g_ref[...]).astype(BF16)
    h_ref[...] = hb
    q_ref[...] = (_project(hb, wq_ref) * q_scale).astype(BF16)
    zf = lax.dot_general(wft_ref[...], hb, _NT, preferred_element_type=F32)
    t = zf[:n_heads] + bf_ref[...]
    logft_ref[...] = jnp.minimum(t, 0.0) - jnp.log1p(jnp.exp(-jnp.abs(t)))


def _norm_q(x2, norm_g, w_qkv, w_ft, b_f, *, tm):
    n, d = x2.shape
    n_heads = b_f.shape[0]
    d_a = n_heads * HEAD_DIM
    row = lambda i: (i, 0)
    return pl.pallas_call(
        functools.partial(_norm_q_kernel, q_scale=HEAD_DIM ** -0.5 * LOG2E, n_heads=n_heads),
        grid=(n // tm,),
        in_specs=[
            pl.BlockSpec((tm, d), row),
            _resident((1, d), (0, 0)),
            _resident((d, d_a), (0, 0)),
            _resident(w_ft.shape, (0, 0)),
            _resident((n_heads, 1), (0, 0)),
        ],
        out_specs=[
            pl.BlockSpec((tm, d), row),
            pl.BlockSpec((tm, d_a), row),
            pl.BlockSpec((n_heads, tm), lambda i: (0, i)),
        ],
        out_shape=[
            jax.ShapeDtypeStruct((n, d), BF16),
            jax.ShapeDtypeStruct((n, d_a), BF16),
            jax.ShapeDtypeStruct((n_heads, n), F32),
        ],
        compiler_params=_compiler_params(("parallel",)),
        name="norm_q",
    )(x2, norm_g, w_qkv, w_ft, b_f)


def _proj_kv_kernel(h_ref, wk_ref, wv_ref, wga_ref, k_ref, kb_ref, v_ref, vb_ref, sga_ref,
                    *, n_heads, v_sub):
    h = h_ref[...]
    z = _project(h, wk_ref)
    _store_heads_on_sublanes(k_ref, z, n_heads)
    kb_ref[...] = z.astype(BF16)
    z = _project(h, wv_ref)
    _store_heads_on_sublanes(v_ref, z, n_heads)
    if v_sub is None:
        vb_ref[...] = z.astype(BF16)
    else:
        for c in range(z.shape[0] // v_sub):
            vb_ref[c] = z[c * v_sub:(c + 1) * v_sub, :].T.astype(BF16)
    sga_ref[...] = _silu(_project(h, wga_ref)).astype(BF16)


def _proj_kv(h, w_qkv, w_rest, *, n_heads, tm, v_sub=None):
    n, d = h.shape
    d_a = n_heads * HEAD_DIM
    row = lambda i: (i, 0)
    if v_sub is None:
        vb_spec = pl.BlockSpec((tm, d_a), row)
        vb_shape = jax.ShapeDtypeStruct((n, d_a), BF16)
    else:
        vb_spec = pl.BlockSpec((tm // v_sub, d_a, v_sub), lambda i: (i, 0, 0))
        vb_shape = jax.ShapeDtypeStruct((n // v_sub, d_a, v_sub), BF16)
    return pl.pallas_call(
        functools.partial(_proj_kv_kernel, n_heads=n_heads, v_sub=v_sub),
        grid=(n // tm,),
        in_specs=[
            pl.BlockSpec((tm, d), row),
            _resident((d, d_a), (0, 1)),
            _resident((d, d_a), (0, 2)),
            _resident((d, d_a), (0, 0)),
        ],
        out_specs=[
            pl.BlockSpec((tm * n_heads, HEAD_DIM), row),
            pl.BlockSpec((tm, d_a), row),
            pl.BlockSpec((tm * n_heads, HEAD_DIM), row),
            vb_spec,
            pl.BlockSpec((tm, d_a), row),
        ],
        out_shape=[
            jax.ShapeDtypeStruct((n * n_heads, HEAD_DIM), F32),
            jax.ShapeDtypeStruct((n, d_a), BF16),
            jax.ShapeDtypeStruct((n * n_heads, HEAD_DIM), F32),
            vb_shape,
            jax.ShapeDtypeStruct((n, d_a), BF16),
        ],
        compiler_params=_compiler_params(("parallel",)),
        name="proj_kv",
    )(h, w_qkv, w_qkv, w_rest)


def _proj_gmlp_kernel(h_ref, wu_ref, wv_ref, wg_ref, lng_ref, lnb_ref, wmix_ref, bmix_ref,
                      ob_ref, *rest, n_groups, emit_vn):
    if emit_vn:
        vn_ref, mix_ref, gu_ref = rest
    else:
        mix_ref, gu_ref = rest
    h = h_ref[...]
    tm = h.shape[0]
    n_chunks = tm // GMLP_CHUNK
    d_g = wv_ref.shape[1] // n_groups

    a = _gelu_tanh(_project(h, wv_ref))
    gu_ref[...] = _gelu_tanh(_project(h, wu_ref))
    mu = jnp.mean(a, axis=-1, keepdims=True)
    ac = a - mu
    var = jnp.mean(ac * ac, axis=-1, keepdims=True)
    vn = ac * lax.rsqrt(var + LN_EPS) * lng_ref[...] + lnb_ref[...]
    if emit_vn:
        vn_ref[...] = vn
    vb = vn.astype(BF16)
    for g in range(n_groups):
        cols = slice(g * d_g, (g + 1) * d_g)
        rhs = jnp.concatenate(
            [vb[r * GMLP_CHUNK:(r + 1) * GMLP_CHUNK, cols] for r in range(n_chunks)], axis=1)
        mixed = jnp.dot(wmix_ref[g], rhs, preferred_element_type=F32)
        for r in range(n_chunks):
            mix_ref[r * GMLP_CHUNK:(r + 1) * GMLP_CHUNK, cols] = (
                mixed[:, r * d_g:(r + 1) * d_g] + bmix_ref[g])

    gate = _silu(_project(h, wg_ref))
    ob_ref[...] = (gu_ref[...] * mix_ref[...] * gate).astype(BF16)


def _proj_gmlp(h, w_rest, ln_g, ln_b, w_mix, b_mix, *, tm, emit_vn):
    n, d = h.shape
    d_b = ln_g.shape[1]
    n_groups = w_mix.shape[0]
    row = lambda i: (i, 0)
    out_specs = [pl.BlockSpec((tm, d_b), row)]
    out_shape = [jax.ShapeDtypeStruct((n, d_b), BF16)]
    if emit_vn:
        out_specs.append(pl.BlockSpec((tm, d_b), row))
        out_shape.append(jax.ShapeDtypeStruct((n, d_b), F32))
    return pl.pallas_call(
        functools.partial(_proj_gmlp_kernel, n_groups=n_groups, emit_vn=emit_vn),
        grid=(n // tm,),
        in_specs=[
            pl.BlockSpec((tm, d), row),
            _resident((d, d_b), (0, 1)),
            _resident((d, d_b), (0, 2)),
            _resident((d, d_b), (0, 3)),
            _resident((1, d_b), (0, 0)),
            _resident((1, d_b), (0, 0)),
            _resident(w_mix.shape, (0, 0, 0)),
            _resident(b_mix.shape, (0, 0, 0)),
        ],
        out_specs=out_specs,
        out_shape=out_shape,
        scratch_shapes=[pltpu.VMEM((tm, d_b), F32), pltpu.VMEM((tm, d_b), F32)],
        compiler_params=_compiler_params(("parallel",)),
        name="proj_gmlp",
    )(h, w_rest, w_rest, w_rest, ln_g, ln_b, w_mix, b_mix)


def _cumsum_kernel(x_ref, o_ref, *, group):
    x = x_ref[...]
    lane = lax.broadcasted_iota(jnp.int32, x.shape, 1)
    sh = 1
    while sh < LANES:
        x = x + jnp.where(lane >= sh, pltpu.roll(x, sh, axis=1), 0.0)
        sh *= 2
    sub = lax.broadcasted_iota(jnp.int32, x.shape, 0) % group
    tot = jnp.broadcast_to(x[:, LANES - 1:LANES], x.shape)
    carry = jnp.where(sub >= 1, pltpu.roll(tot, 1, axis=0), 0.0)
    sh = 1
    while sh < group:
        carry = carry + jnp.where(sub >= sh, pltpu.roll(carry, sh, axis=0), 0.0)
        sh *= 2
    o_ref[...] = x + carry


def _cumsum_rows(x2, group):
    return pl.pallas_call(
        functools.partial(_cumsum_kernel, group=group),
        out_shape=jax.ShapeDtypeStruct(x2.shape, F32),
        name="cumsum_rows",
    )(x2)


def _online_softmax_step(s, v_blk, m, l, acc):
    m_new = jnp.maximum(m, jnp.max(s, axis=1, keepdims=True))
    alpha = jnp.exp2(m - m_new)
    p = jnp.exp2(s - m_new)
    l = alpha * l + jnp.sum(p, axis=1, keepdims=True)
    acc = alpha * acc + jnp.dot(p.astype(BF16), v_blk, preferred_element_type=F32)
    return m_new, l, acc


QUERY_GROUP = 256
N_BIAS_PARTS = 3
SUM_ROWS = 16


def _attn_prompt_kernel(q_ref, kb_ref, vt_ref, ct_ref, sga_ref, o_ref,
                        ka_ref, s_ref, p_ref, acc_ref, qa_ref, *, blk):
    kv_sub = s_ref.shape[1]
    seq = q_ref.shape[0]
    n_heads = q_ref.shape[1] // HEAD_DIM
    lane = lax.broadcasted_iota(jnp.int32, (LANES, LANES), 1)

    def head_cols(hh):
        return slice(hh * HEAD_DIM, (hh + 1) * HEAD_DIM)

    def load_keys(c):
        for hh in range(n_heads):
            for sb in range(c * kv_sub // LANES, (c + 1) * kv_sub // LANES):
                rows = slice(sb * LANES, (sb + 1) * LANES)
                rest = jnp.broadcast_to(ct_ref[hh, :, sb:sb + 1] * (-LOG2E), (LANES, LANES))
                extra = jnp.zeros((LANES, LANES), F32)
                for part in range(N_BIAS_PARTS):
                    piece = rest.astype(BF16).astype(F32)
                    extra = jnp.where(lane == part, piece, extra)
                    rest = rest - piece
                ka_ref[hh, rows, :] = jnp.concatenate(
                    [kb_ref[rows, head_cols(hh)], extra.astype(BF16)], axis=1)

    ones_feat = jnp.where(lax.broadcasted_iota(jnp.int32, (blk, LANES), 1) < N_BIAS_PARTS,
                          1.0, 0.0).astype(BF16)
    ones_rows = jnp.ones((SUM_ROWS, kv_sub), BF16)

    def load_queries(qb):
        for hh in range(n_heads):
            qa_ref[hh * blk:(hh + 1) * blk, :] = jnp.concatenate(
                [q_ref[qb * blk:(qb + 1) * blk, head_cols(hh)], ones_feat], axis=1)

    groups_per_head = blk // QUERY_GROUP
    n_groups = n_heads * groups_per_head
    group = [slice(t * QUERY_GROUP, (t + 1) * QUERY_GROUP) for t in range(n_groups)]
    head_of = [t // groups_per_head for t in range(n_groups)]
    order = [hh * groups_per_head + g for g in range(groups_per_head) for hh in range(n_heads)]

    def scores(c, g, n_keys):
        keys = slice(c * kv_sub, c * kv_sub + n_keys)
        return lax.dot_general(ka_ref[head_of[g], keys, :], qa_ref[group[g], :],
                               _NT, preferred_element_type=F32)

    own_tile = (QUERY_GROUP, QUERY_GROUP)
    not_after = (lax.broadcasted_iota(jnp.int32, own_tile, 0)
                 <= lax.broadcasted_iota(jnp.int32, own_tile, 1))

    def causal_tail(s):
        n_before = s.shape[0] - QUERY_GROUP
        tail = jnp.where(not_after, s[n_before:], -jnp.inf)
        return tail if n_before == 0 else jnp.concatenate([s[:n_before], tail], axis=0)

    def softmax(s, m):
        m_new = jnp.maximum(m, jnp.max(s, axis=0, keepdims=True))
        return m_new, jnp.exp2(m - m_new), jnp.exp2(s - m_new).astype(BF16)

    v_chunk = vt_ref.shape[2]

    def values_t(c, hh, n_keys):
        first = c * kv_sub
        parts = []
        while n_keys > 0:
            i, off = divmod(first, v_chunk)
            take = min(v_chunk - off, n_keys)
            parts.append(vt_ref[i, head_cols(hh), off:off + take])
            first += take
            n_keys -= take
        return parts[0] if len(parts) == 1 else jnp.concatenate(parts, axis=1)

    def accumulate(c, g, p, alpha):
        n_keys = p.shape[0]
        vt_sum = jnp.concatenate([values_t(c, head_of[g], n_keys), ones_rows[:, :n_keys]],
                                 axis=0)
        pv = jnp.dot(vt_sum, p, preferred_element_type=F32)
        acc_ref[:, group[g]] = pv if alpha is None else alpha * acc_ref[:, group[g]] + pv

    def finish(qb):
        rows = slice(qb * blk, (qb + 1) * blk)
        for hh in range(n_heads):
            cols = slice(hh * blk, (hh + 1) * blk)
            out = (acc_ref[:HEAD_DIM, cols] / acc_ref[HEAD_DIM:HEAD_DIM + 1, cols]).T
            o_ref[rows, head_cols(hh)] = (
                out * sga_ref[rows, head_cols(hh)].astype(F32)).astype(BF16)

    chunks_per_block = blk // kv_sub
    items = []
    for qb in range(seq // blk):
        for c in range((qb + 1) * chunks_per_block):
            first_key = c * kv_sub - qb * blk
            n_keys, own = [], []
            for g in range(n_groups):
                first_query = (g % groups_per_head) * QUERY_GROUP
                n_keys.append(min(max(first_query + QUERY_GROUP - first_key, 0), kv_sub))
                own.append(first_key <= first_query < first_key + kv_sub)
            items.append((qb, c, n_keys, own))

    m = [None] * n_groups
    alpha = {}
    for t in range(len(items) + 2):
        qk_item = items[t] if t < len(items) else None
        sm_item = items[t - 1] if 1 <= t <= len(items) else None
        pv_item = items[t - 2] if t >= 2 else None
        if qk_item is not None:
            if qk_item[1] // chunks_per_block == qk_item[0]:
                load_keys(qk_item[1])
            if qk_item[1] == 0:
                load_queries(qk_item[0])
        for g in order:
            if qk_item is not None and qk_item[2][g]:
                qb, c, n_keys, _ = qk_item
                s_ref[t % 2, :n_keys[g], group[g]] = scores(c, g, n_keys[g])
            if pv_item is not None and pv_item[2][g]:
                qb, c, n_keys, _ = pv_item
                accumulate(c, g, p_ref[t % 2, :n_keys[g], group[g]],
                           None if c == 0 else alpha[t - 2, g])
            if sm_item is not None and sm_item[2][g]:
                qb, c, n_keys, own = sm_item
                s = s_ref[(t - 1) % 2, :n_keys[g], group[g]]
                if own[g]:
                    s = causal_tail(s)
                m_old = jnp.full((1, QUERY_GROUP), -jnp.inf, F32) if c == 0 else m[g]
                m[g], alpha[t - 1, g], p_ref[(t - 1) % 2, :n_keys[g], group[g]] = softmax(s, m_old)
        if pv_item is not None and pv_item[1] == (pv_item[0] + 1) * chunks_per_block - 1:
            finish(pv_item[0])


def _attn_prompt(q, kb, vt, cum_t, sga, *, batch, seq, blk, kv_sub, heads_per_step):
    n, d_a = q.shape
    n_heads = d_a // HEAD_DIM
    v_chunk = vt.shape[2]
    assert blk % kv_sub == 0 and seq % blk == 0 and n_heads % heads_per_step == 0
    assert seq % v_chunk == 0 and v_chunk % QUERY_GROUP == 0
    width = heads_per_step * HEAD_DIM
    wide = heads_per_step * blk
    per_seq = lambda b, h: (b, h)
    return pl.pallas_call(
        functools.partial(_attn_prompt_kernel, blk=blk),
        grid=(batch, n_heads // heads_per_step),
        in_specs=[
            pl.BlockSpec((seq, width), per_seq),
            pl.BlockSpec((seq, width), per_seq),
            pl.BlockSpec((seq // v_chunk, width, v_chunk), lambda b, h: (b, h, 0)),
            pl.BlockSpec((heads_per_step, None, LANES, seq // LANES), lambda b, h: (h, b, 0, 0)),
            pl.BlockSpec((seq, width), per_seq),
        ],
        out_specs=pl.BlockSpec((seq, width), per_seq),
        out_shape=jax.ShapeDtypeStruct((n, d_a), BF16),
        scratch_shapes=[pltpu.VMEM((heads_per_step, seq, HEAD_DIM + LANES), BF16),
                        pltpu.VMEM((2, kv_sub, wide), F32),
                        pltpu.VMEM((2, kv_sub, wide), BF16),
                        pltpu.VMEM((HEAD_DIM + SUM_ROWS, wide), F32),
                        pltpu.VMEM((wide, HEAD_DIM + LANES), BF16)],
        compiler_params=_compiler_params(("parallel", "parallel")),
        name="attn_prompt",
    )(q, kb, vt, cum_t, sga)


def _attn_sample_kernel(q_ref, kn_ref, vn_ref, ck_ref, cv_ref, cp_ref, cn_ref, sga_ref, o_ref,
                        m_ref, l_ref, acc_ref, *, n_heads):
    c = pl.program_id(1)

    @pl.when(c == 0)
    def _():
        m_ref[...] = jnp.full(m_ref.shape, -jnp.inf, F32)
        l_ref[...] = jnp.zeros(l_ref.shape, F32)
        acc_ref[...] = jnp.zeros(acc_ref.shape, F32)

    def head_cols(h):
        return slice(h * HEAD_DIM, (h + 1) * HEAD_DIM)

    chunk = ck_ref.shape[0] // n_heads

    def cached_head(ref, h):
        return ref[pl.ds(h, chunk, stride=n_heads), :].astype(BF16)

    heads = range(n_heads)
    s = [lax.dot_general(q_ref[:, head_cols(h)], cached_head(ck_ref, h), _NT,
                         preferred_element_type=F32) + cp_ref[h:h + 1, :] * (-LOG2E)
         for h in heads]
    m_new = [jnp.maximum(m_ref[h], jnp.max(s[h], axis=1, keepdims=True)) for h in heads]
    p = [jnp.exp2(s[h] - m_new[h]) for h in heads]
    pv = [jnp.dot(p[h].astype(BF16), cached_head(cv_ref, h), preferred_element_type=F32)
          for h in heads]
    for h in heads:
        alpha = jnp.exp2(m_ref[h] - m_new[h])
        l_ref[h] = alpha * l_ref[h] + jnp.sum(p[h], axis=1, keepdims=True)
        acc_ref[h] = alpha * acc_ref[h] + pv[h]
        m_ref[h] = m_new[h]

    @pl.when(c == pl.num_programs(1) - 1)
    def _():
        for h in range(n_heads):
            q = q_ref[:, head_cols(h)]
            s = lax.dot_general(q, kn_ref[:, head_cols(h)], _NT, preferred_element_type=F32)
            s = s + cn_ref[h:h + 1, :] * (-LOG2E)
            rows = lax.broadcasted_iota(jnp.int32, s.shape, 0)
            cols = lax.broadcasted_iota(jnp.int32, s.shape, 1)
            s = jnp.where(cols <= rows, s, -jnp.inf)
            _, l, acc = _online_softmax_step(s, vn_ref[:, head_cols(h)],
                                             m_ref[h], l_ref[h], acc_ref[h])
            o_ref[:, head_cols(h)] = (
                acc / l * sga_ref[:, head_cols(h)].astype(F32)).astype(BF16)


def _attn_sample(q, k_new, v_new, cache_k2, cache_v2, cum_past, cum_new, sga, *,
                 batch, t_new, past, chunk):
    n, d_a = q.shape
    n_heads = d_a // HEAD_DIM
    n_chunks = past // chunk
    new_map = lambda b, c: (b, 0)
    cache_map = lambda b, c: (b * n_chunks + c, 0)
    return pl.pallas_call(
        functools.partial(_attn_sample_kernel, n_heads=n_heads),
        grid=(batch, n_chunks),
        in_specs=[
            pl.BlockSpec((t_new, d_a), new_map),
            pl.BlockSpec((t_new, d_a), new_map),
            pl.BlockSpec((t_new, d_a), new_map),
            pl.BlockSpec((chunk * n_heads, HEAD_DIM), cache_map),
            pl.BlockSpec((chunk * n_heads, HEAD_DIM), cache_map),
            pl.BlockSpec((None, None, n_heads, chunk), lambda b, c: (b, c, 0, 0)),
            pl.BlockSpec((None, n_heads, t_new), lambda b, c: (b, 0, 0)),
            pl.BlockSpec((t_new, d_a), new_map),
        ],
        out_specs=pl.BlockSpec((t_new, d_a), new_map),
        out_shape=jax.ShapeDtypeStruct((n, d_a), BF16),
        scratch_shapes=[pltpu.VMEM((n_heads, t_new, 1), F32),
                        pltpu.VMEM((n_heads, t_new, 1), F32),
                        pltpu.VMEM((n_heads, t_new, HEAD_DIM), F32)],
        compiler_params=_compiler_params(("parallel", "arbitrary")),
        name="attn_sample",
    )(q, k_new, v_new, cache_k2, cache_v2, cum_past, cum_new, sga)


def _out_proj_kernel(oa_ref, ob_ref, w_ref, x_ref, fg_ref, y_ref, *, final_norm):
    o = jnp.concatenate([oa_ref[...], ob_ref[...]], axis=1)
    y = x_ref[...] + jnp.dot(o, w_ref[...], preferred_element_type=F32)
    if final_norm:
        y = _rmsnorm_rows(y, fg_ref[...])
    y_ref[...] = y


def _out_proj(out_a, out_b, w_out, x2, final_g, *, tm, final_norm):
    n, d = x2.shape
    d_a = out_a.shape[1]
    d_b = out_b.shape[1]
    row = lambda i: (i, 0)
    const = lambda i: (0, 0)
    return pl.pallas_call(
        functools.partial(_out_proj_kernel, final_norm=final_norm),
        grid=(n // tm,),
        in_specs=[
            pl.BlockSpec((tm, d_a), row),
            pl.BlockSpec((tm, d_b), row),
            pl.BlockSpec(w_out.shape, const),
            pl.BlockSpec((tm, d), row),
            pl.BlockSpec((1, d), const),
        ],
        out_specs=pl.BlockSpec((tm, d), row),
        out_shape=jax.ShapeDtypeStruct((n, d), F32),
        compiler_params=_compiler_params(("parallel",)),
        name="out_proj",
    )(out_a, out_b, w_out, x2, final_g)


def _pad_rows(a, rows):
    return jnp.pad(a, ((0, rows - a.shape[0]), (0, 0)))


def _mixing_weights(w_s, b_s, seq):
    c = min(seq, GMLP_CHUNK)
    reps = GMLP_CHUNK // c
    w = w_s[:, :c, :c] * jnp.tril(jnp.ones((c, c), w_s.dtype))
    if reps > 1:
        w = jnp.einsum('rs,gab->grasb', jnp.eye(reps, dtype=w.dtype), w)
        w = w.reshape(w_s.shape[0], GMLP_CHUNK, GMLP_CHUNK)
    b = jnp.tile(b_s[:, :c], (1, reps))
    b_full = jnp.broadcast_to(b[:, :, None], (b.shape[0], GMLP_CHUNK, LANES))
    return w.astype(BF16), b_full.astype(F32)


def _split_w_in_kernel(wt_ref, qkv_ref, rest_ref, *, off_rest):
    wt = wt_ref[...]
    qkv_ref[...] = wt[:qkv_ref.shape[1], :].T.astype(BF16)
    rest_ref[...] = wt[off_rest:, :].T.astype(BF16)


def _split_w_in(w_in_t, off_f, off_rest, *, tc=W_SPLIT_COLS):
    d_in, d = w_in_t.shape
    return pl.pallas_call(
        functools.partial(_split_w_in_kernel, off_rest=off_rest),
        grid=(d // tc,),
        in_specs=[pl.BlockSpec((d_in, tc), lambda i: (0, i))],
        out_specs=[pl.BlockSpec((tc, off_f), lambda i: (i, 0)),
                   pl.BlockSpec((tc, d_in - off_rest), lambda i: (i, 0))],
        out_shape=[jax.ShapeDtypeStruct((d, off_f), BF16),
                   jax.ShapeDtypeStruct((d, d_in - off_rest), BF16)],
        compiler_params=_compiler_params(("parallel",)),
        name="split_w_in",
    )(w_in_t)


def _prepare_weights(params):
    norm_g, w_in, b_f, ln_g, ln_b, w_s, b_s, w_out = params
    n_heads = b_f.shape[0]
    off_f = 3 * n_heads * HEAD_DIM
    off_ga = off_f + n_heads
    w_in_t = w_in.T
    w_qkv, w_rest = _split_w_in(w_in_t, off_f, off_ga)
    return dict(
        norm_g=norm_g[None], b_f=b_f[:, None], ln_g=ln_g[None], ln_b=ln_b[None],
        w_qkv=w_qkv, w_rest=w_rest,
        w_ft=_pad_rows(w_in_t[off_f:off_ga], 16).astype(BF16),
        w_out=w_out.astype(BF16), w_s=w_s, b_s=b_s)


def _layer(x, w, caches, final_g, *, final_norm, tm):
    batch, seq, d = x.shape
    n = batch * seq
    n_heads = w['b_f'].shape[0]
    d_b = w['ln_g'].shape[1]

    x2 = x.reshape(n, d)
    w_mix, b_mix = _mixing_weights(w['w_s'], w['b_s'], seq)

    blk, kv_sub = ATTN_QUERY_BLOCK, ATTN_KEY_CHUNK
    h, q, logft = _norm_q(x2, w['norm_g'], w['w_qkv'], w['w_ft'], w['b_f'], tm=tm)
    k, kb, v, vb, sga = _proj_kv(h, w['w_qkv'], w['w_rest'], n_heads=n_heads, tm=tm,
                                 v_sub=V_T_CHUNK if caches is None else None)
    gm = _proj_gmlp(h, w['w_rest'], w['ln_g'], w['ln_b'], w_mix, b_mix,
                    tm=tm, emit_vn=caches is not None)

    logf = logft.T.reshape(batch, seq, n_heads)
    if caches is None:
        out_b = gm[0]
        vn = None
        group = seq // LANES
        cum = _cumsum_rows(logft.reshape(n_heads * batch * group, LANES), group)
        cum_t = cum.reshape(n_heads, batch, group, LANES).transpose(0, 1, 3, 2)
        out_a = _attn_prompt(q, kb, vb, cum_t, sga, batch=batch, seq=seq, blk=blk, kv_sub=kv_sub,
                             heads_per_step=ATTN_HEADS_PER_STEP)
    else:
        out_b, vn = gm
        cache_k, cache_v, cache_logf = caches
        past = cache_k.shape[1]
        chunk = CACHE_CHUNK
        total = past + seq
        group = -(-total // LANES)
        lf_all = jnp.concatenate(
            [cache_logf.astype(F32).transpose(0, 2, 1), logf.transpose(0, 2, 1)], axis=2)
        lf_all = jnp.pad(lf_all, ((0, 0), (0, 0), (0, group * LANES - total)))
        cum = _cumsum_rows(lf_all.reshape(batch * n_heads * group, LANES), group)
        cum = cum.reshape(batch, n_heads, group * LANES)
        cum_past = cum[:, :, :past].reshape(batch, n_heads, past // chunk, chunk)
        cum_past = cum_past.transpose(0, 2, 1, 3)
        cum_new = cum[:, :, past:total]
        out_a = _attn_sample(q, kb, vb, cache_k.reshape(batch * past * n_heads, HEAD_DIM),
                             cache_v.reshape(batch * past * n_heads, HEAD_DIM),
                             cum_past, cum_new, sga,
                             batch=batch, t_new=seq, past=past, chunk=chunk)

    y2 = _out_proj(out_a, out_b, w['w_out'], x2, final_g[None], tm=tm, final_norm=final_norm)
    y = y2.reshape(batch, seq, d)
    k4 = k.reshape(batch, seq, n_heads, HEAD_DIM)
    v4 = v.reshape(batch, seq, n_heads, HEAD_DIM)
    vn3 = None if vn is None else vn.reshape(batch, seq, d_b)
    return y, k4, v4, logf, vn3


def kernel(x_prompt, x_sample, cache_k, cache_v, cache_logf, norm_g, w_in, b_f, ln_g, ln_b,
           w_s, b_s, w_out, final_g):
    depth = norm_g.shape[0]
    hp, hs = x_prompt, x_sample
    kp, vp, fp, ksm, vsm, fsm, gsm = [], [], [], [], [], [], []
    n_sample = x_sample.shape[0] * x_sample.shape[1]
    for l in range(depth):
        w = _prepare_weights(
            (norm_g[l], w_in[l], b_f[l], ln_g[l], ln_b[l], w_s[l], b_s[l], w_out[l]))
        last = l == depth - 1
        hp, k1, v1, f1, _ = _layer(hp, w, None, final_g, final_norm=last, tm=ROW_TILE)
        hs, k2, v2, f2, g2 = _layer(hs, w, (cache_k[l], cache_v[l], cache_logf[l]), final_g,
                                    final_norm=last, tm=n_sample)
        kp.append(k1); vp.append(v1); fp.append(f1)
        ksm.append(k2); vsm.append(v2); fsm.append(f2); gsm.append(g2)
    return (hp, hs, jnp.stack(kp), jnp.stack(vp), jnp.stack(fp),
            jnp.stack(ksm), jnp.stack(vsm), jnp.stack(fsm), jnp.stack(gsm))
```

```python
import functools
import math

import jax
import jax.numpy as jnp
from jax import lax
from jax.experimental import pallas as pl
from jax.experimental.pallas import tpu as pltpu

F32 = jnp.float32
BF16 = jnp.bfloat16

HEAD_DIM = 128
GMLP_CHUNK = 128
RMS_EPS = 1e-6
LN_EPS = 1e-5
LOG2E = math.log2(math.e)
LANES = 128
VMEM_LIMIT_BYTES = 48 * 1024 * 1024

ROW_TILE = 512
WIDE_ROW_TILE = 1024
ATTN_QUERY_BLOCK = 1024
ATTN_KEY_CHUNK = 512
ATTN_HEADS_PER_STEP = 1
CACHE_CHUNK = 2048
W_SPLIT_COLS = 256

_NT = (((1,), (1,)), ((), ()))


def _compiler_params(semantics):
    return pltpu.CompilerParams(dimension_semantics=semantics,
                                vmem_limit_bytes=VMEM_LIMIT_BYTES)


def _rmsnorm_rows(x, g):
    return x * lax.rsqrt(jnp.mean(x * x, axis=-1, keepdims=True) + RMS_EPS) * g


def _gelu_tanh(x):
    c = math.sqrt(2.0 / math.pi)
    return 0.5 * x * (1.0 + jnp.tanh(c * (x + 0.044715 * (x * x * x))))


def _silu(x):
    return x * jax.nn.sigmoid(x)


def _store_heads_on_sublanes(ref, z, n_heads):
    rows = z.shape[0]
    for h in range(n_heads):
        ref[pl.ds(h, rows, stride=n_heads), :] = z[:, h * HEAD_DIM:(h + 1) * HEAD_DIM]


def _project(h, w_ref):
    return jnp.dot(h, w_ref[...], preferred_element_type=F32)


def _resident(shape, block_index):
    return pl.BlockSpec(shape, lambda i: block_index, pipeline_mode=pl.Buffered(1))


def _norm_q_kernel(x_ref, g_ref, wq_ref, wft_ref, bf_ref, h_ref, q_ref, logft_ref,
                   *, q_scale, n_heads):
    hb = _rmsnorm_rows(x_ref[...], g_ref[...]).astype(BF16)
    h_ref[...] = hb
    q_ref[...] = (_project(hb, wq_ref) * q_scale).astype(BF16)
    zf = lax.dot_general(wft_ref[...], hb, _NT, preferred_element_type=F32)
    t = zf[:n_heads] + bf_ref[...]
    logft_ref[...] = jnp.minimum(t, 0.0) - jnp.log1p(jnp.exp(-jnp.abs(t)))


def _norm_q(x2, norm_g, w_qkv, w_ft, b_f, *, tm):
    n, d = x2.shape
    n_heads = b_f.shape[0]
    d_a = n_heads * HEAD_DIM
    row = lambda i: (i, 0)
    return pl.pallas_call(
        functools.partial(_norm_q_kernel, q_scale=HEAD_DIM ** -0.5 * LOG2E, n_heads=n_heads),
        grid=(n // tm,),
        in_specs=[
            pl.BlockSpec((tm, d), row),
            _resident((1, d), (0, 0)),
            _resident((d, d_a), (0, 0)),
            _resident(w_ft.shape, (0, 0)),
            _resident((n_heads, 1), (0, 0)),
        ],
        out_specs=[
            pl.BlockSpec((tm, d), row),
            pl.BlockSpec((tm, d_a), row),
            pl.BlockSpec((n_heads, tm), lambda i: (0, i)),
        ],
        out_shape=[
            jax.ShapeDtypeStruct((n, d), BF16),
            jax.ShapeDtypeStruct((n, d_a), BF16),
            jax.ShapeDtypeStruct((n_heads, n), F32),
        ],
        compiler_params=_compiler_params(("parallel",)),
        name="norm_q",
    )(x2, norm_g, w_qkv, w_ft, b_f)


def _proj_kv_kernel(h_ref, wk_ref, wv_ref, wga_ref, k_ref, kb_ref, v_ref, vb_ref, sga_ref,
                    *, n_heads, v_sub):
    h = h_ref[...]
    z = _project(h, wk_ref)
    _store_heads_on_sublanes(k_ref, z, n_heads)
    kb_ref[...] = z.astype(BF16)
    z = _project(h, wv_ref)
    _store_heads_on_sublanes(v_ref, z, n_heads)
    if v_sub is None:
        vb_ref[...] = z.astype(BF16)
    else:
        for c in range(z.shape[0] // v_sub):
            vb_ref[c] = z[c * v_sub:(c + 1) * v_sub, :].T.astype(BF16)
    sga_ref[...] = _silu(_project(h, wga_ref)).astype(BF16)


def _proj_kv(h, w_qkv, w_rest, *, n_heads, tm, v_sub=None):
    n, d = h.shape
    d_a = n_heads * HEAD_DIM
    row = lambda i: (i, 0)
    if v_sub is None:
        vb_spec = pl.BlockSpec((tm, d_a), row)
        vb_shape = jax.ShapeDtypeStruct((n, d_a), BF16)
    else:
        vb_spec = pl.BlockSpec((tm // v_sub, d_a, v_sub), lambda i: (i, 0, 0))
        vb_shape = jax.ShapeDtypeStruct((n // v_sub, d_a, v_sub), BF16)
    return pl.pallas_call(
        functools.partial(_proj_kv_kernel, n_heads=n_heads, v_sub=v_sub),
        grid=(n // tm,),
        in_specs=[
            pl.BlockSpec((tm, d), row),
            _resident((d, d_a), (0, 1)),
            _resident((d, d_a), (0, 2)),
            _resident((d, d_a), (0, 0)),
        ],
        out_specs=[
            pl.BlockSpec((tm * n_heads, HEAD_DIM), row),
            pl.BlockSpec((tm, d_a), row),
            pl.BlockSpec((tm * n_heads, HEAD_DIM), row),
            vb_spec,
            pl.BlockSpec((tm, d_a), row),
        ],
        out_shape=[
            jax.ShapeDtypeStruct((n * n_heads, HEAD_DIM), F32),
            jax.ShapeDtypeStruct((n, d_a), BF16),
            jax.ShapeDtypeStruct((n * n_heads, HEAD_DIM), F32),
            vb_shape,
            jax.ShapeDtypeStruct((n, d_a), BF16),
        ],
        compiler_params=_compiler_params(("parallel",)),
        name="proj_kv",
    )(h, w_qkv, w_qkv, w_rest)


def _proj_gmlp_kernel(h_ref, wu_ref, wv_ref, wg_ref, lng_ref, lnb_ref, wmix_ref, bmix_ref,
                      ob_ref, *rest, n_groups, emit_vn):
    if emit_vn:
        vn_ref, mix_ref, gu_ref = rest
    else:
        mix_ref, gu_ref = rest
    h = h_ref[...]
    tm = h.shape[0]
    n_chunks = tm // GMLP_CHUNK
    d_g = wv_ref.shape[1] // n_groups

    a = _gelu_tanh(_project(h, wv_ref))
    gu_ref[...] = _gelu_tanh(_project(h, wu_ref))
    mu = jnp.mean(a, axis=-1, keepdims=True)
    ac = a - mu
    var = jnp.mean(ac * ac, axis=-1, keepdims=True)
    vn = ac * lax.rsqrt(var + LN_EPS) * lng_ref[...] + lnb_ref[...]
    if emit_vn:
        vn_ref[...] = vn
    vb = vn.astype(BF16)
    for g in range(n_groups):
        cols = slice(g * d_g, (g + 1) * d_g)
        rhs = jnp.concatenate(
            [vb[r * GMLP_CHUNK:(r + 1) * GMLP_CHUNK, cols] for r in range(n_chunks)], axis=1)
        mixed = jnp.dot(wmix_ref[g], rhs, preferred_element_type=F32)
        for r in range(n_chunks):
            mix_ref[r * GMLP_CHUNK:(r + 1) * GMLP_CHUNK, cols] = (
                mixed[:, r * d_g:(r + 1) * d_g] + bmix_ref[g])

    gate = _silu(_project(h, wg_ref))
    ob_ref[...] = (gu_ref[...] * mix_ref[...] * gate).astype(BF16)


def _proj_gmlp(h, w_rest, ln_g, ln_b, w_mix, b_mix, *, tm, emit_vn):
    n, d = h.shape
    d_b = ln_g.shape[1]
    n_groups = w_mix.shape[0]
    row = lambda i: (i, 0)
    out_specs = [pl.BlockSpec((tm, d_b), row)]
    out_shape = [jax.ShapeDtypeStruct((n, d_b), BF16)]
    if emit_vn:
        out_specs.append(pl.BlockSpec((tm, d_b), row))
        out_shape.append(jax.ShapeDtypeStruct((n, d_b), F32))
    return pl.pallas_call(
        functools.partial(_proj_gmlp_kernel, n_groups=n_groups, emit_vn=emit_vn),
        grid=(n // tm,),
        in_specs=[
            pl.BlockSpec((tm, d), row),
            _resident((d, d_b), (0, 1)),
            _resident((d, d_b), (0, 2)),
            _resident((d, d_b), (0, 3)),
            _resident((1, d_b), (0, 0)),
            _resident((1, d_b), (0, 0)),
            _resident(w_mix.shape, (0, 0, 0)),
            _resident(b_mix.shape, (0, 0, 0)),
        ],
        out_specs=out_specs,
        out_shape=out_shape,
        scratch_shapes=[pltpu.VMEM((tm, d_b), F32), pltpu.VMEM((tm, d_b), F32)],
        compiler_params=_compiler_params(("parallel",)),
        name="proj_gmlp",
    )(h, w_rest, w_rest, w_rest, ln_g, ln_b, w_mix, b_mix)


def _cumsum_kernel(x_ref, o_ref, *, group):
    x = x_ref[...]
    lane = lax.broadcasted_iota(jnp.int32, x.shape, 1)
    sh = 1
    while sh < LANES:
        x = x + jnp.where(lane >= sh, pltpu.roll(x, sh, axis=1), 0.0)
        sh *= 2
    sub = lax.broadcasted_iota(jnp.int32, x.shape, 0) % group
    tot = jnp.broadcast_to(x[:, LANES - 1:LANES], x.shape)
    carry = jnp.where(sub >= 1, pltpu.roll(tot, 1, axis=0), 0.0)
    sh = 1
    while sh < group:
        carry = carry + jnp.where(sub >= sh, pltpu.roll(carry, sh, axis=0), 0.0)
        sh *= 2
    o_ref[...] = x + carry


def _cumsum_rows(x2, group):
    return pl.pallas_call(
        functools.partial(_cumsum_kernel, group=group),
        out_shape=jax.ShapeDtypeStruct(x2.shape, F32),
        name="cumsum_rows",
    )(x2)


def _online_softmax_step(s, v_blk, m, l, acc):
    m_new = jnp.maximum(m, jnp.max(s, axis=1, keepdims=True))
    alpha = jnp.exp2(m - m_new)
    p = jnp.exp2(s - m_new)
    l = alpha * l + jnp.sum(p, axis=1, keepdims=True)
    acc = alpha * acc + jnp.dot(p.astype(BF16), v_blk, preferred_element_type=F32)
    return m_new, l, acc


QUERY_GROUP = 256
N_BIAS_PARTS = 3
SUM_ROWS = 16


def _attn_prompt_kernel(q_ref, kb_ref, vt_ref, ct_ref, sga_ref, o_ref,
                        ka_ref, s_ref, p_ref, acc_ref, qa_ref, *, blk):
    kv_sub = s_ref.shape[1]
    seq = q_ref.shape[0]
    n_heads = q_ref.shape[1] // HEAD_DIM
    lane = lax.broadcasted_iota(jnp.int32, (LANES, LANES), 1)

    def head_cols(hh):
        return slice(hh * HEAD_DIM, (hh + 1) * HEAD_DIM)

    def load_keys(c):
        for hh in range(n_heads):
            for sb in range(c * kv_sub // LANES, (c + 1) * kv_sub // LANES):
                rows = slice(sb * LANES, (sb + 1) * LANES)
                rest = jnp.broadcast_to(ct_ref[hh, :, sb:sb + 1] * (-LOG2E), (LANES, LANES))
                extra = jnp.zeros((LANES, LANES), F32)
                for part in range(N_BIAS_PARTS):
                    piece = rest.astype(BF16).astype(F32)
                    extra = jnp.where(lane == part, piece, extra)
                    rest = rest - piece
                ka_ref[hh, rows, :] = jnp.concatenate(
                    [kb_ref[rows, head_cols(hh)], extra.astype(BF16)], axis=1)

    ones_feat = jnp.where(lax.broadcasted_iota(jnp.int32, (blk, LANES), 1) < N_BIAS_PARTS,
                          1.0, 0.0).astype(BF16)
    ones_rows = jnp.ones((SUM_ROWS, kv_sub), BF16)

    def load_queries(qb):
        for hh in range(n_heads):
            qa_ref[hh * blk:(hh + 1) * blk, :] = jnp.concatenate(
                [q_ref[qb * blk:(qb + 1) * blk, head_cols(hh)], ones_feat], axis=1)

    groups_per_head = blk // QUERY_GROUP
    n_groups = n_heads * groups_per_head
    group = [slice(t * QUERY_GROUP, (t + 1) * QUERY_GROUP) for t in range(n_groups)]
    head_of = [t // groups_per_head for t in range(n_groups)]
    order = [hh * groups_per_head + g for g in range(groups_per_head) for hh in range(n_heads)]

    def scores(c, g, n_keys):
        keys = slice(c * kv_sub, c * kv_sub + n_keys)
        return lax.dot_general(ka_ref[head_of[g], keys, :], qa_ref[group[g], :],
                               _NT, preferred_element_type=F32)

    own_tile = (QUERY_GROUP, QUERY_GROUP)
    not_after = (lax.broadcasted_iota(jnp.int32, own_tile, 0)
                 <= lax.broadcasted_iota(jnp.int32, own_tile, 1))

    def causal_tail(s):
        n_before = s.shape[0] - QUERY_GROUP
        tail = jnp.where(not_after, s[n_before:], -jnp.inf)
        return tail if n_before == 0 else jnp.concatenate([s[:n_before], tail], axis=0)

    def softmax(s, m):
        m_new = jnp.maximum(m, jnp.max(s, axis=0, keepdims=True))
        return m_new, jnp.exp2(m - m_new), jnp.exp2(s - m_new).astype(BF16)

    def accumulate(c, g, p, alpha):
        n_keys = p.shape[0]
        vt_sum = jnp.concatenate([vt_ref[c, head_cols(head_of[g]), :][:, :n_keys],
                                  ones_rows[:, :n_keys]], axis=0)
        pv = jnp.dot(vt_sum, p, preferred_element_type=F32)
        acc_ref[:, group[g]] = pv if alpha is None else alpha * acc_ref[:, group[g]] + pv

    def finish(qb):
        rows = slice(qb * blk, (qb + 1) * blk)
        for hh in range(n_heads):
            cols = slice(hh * blk, (hh + 1) * blk)
            out = (acc_ref[:HEAD_DIM, cols] / acc_ref[HEAD_DIM:HEAD_DIM + 1, cols]).T
            o_ref[rows, head_cols(hh)] = (
                out * sga_ref[rows, head_cols(hh)].astype(F32)).astype(BF16)

    chunks_per_block = blk // kv_sub
    items = []
    for qb in range(seq // blk):
        for c in range((qb + 1) * chunks_per_block):
            first_key = c * kv_sub - qb * blk
            n_keys, own = [], []
            for g in range(n_groups):
                first_query = (g % groups_per_head) * QUERY_GROUP
                n_keys.append(min(max(first_query + QUERY_GROUP - first_key, 0), kv_sub))
                own.append(first_key <= first_query < first_key + kv_sub)
            items.append((qb, c, n_keys, own))

    m = [None] * n_groups
    alpha = {}
    for t in range(len(items) + 2):
        qk_item = items[t] if t < len(items) else None
        sm_item = items[t - 1] if 1 <= t <= len(items) else None
        pv_item = items[t - 2] if t >= 2 else None
        if qk_item is not None:
            if qk_item[1] // chunks_per_block == qk_item[0]:
                load_keys(qk_item[1])
            if qk_item[1] == 0:
                load_queries(qk_item[0])
        for g in order:
            if qk_item is not None and qk_item[2][g]:
                qb, c, n_keys, _ = qk_item
                s_ref[t % 2, :n_keys[g], group[g]] = scores(c, g, n_keys[g])
            if pv_item is not None and pv_item[2][g]:
                qb, c, n_keys, _ = pv_item
                accumulate(c, g, p_ref[t % 2, :n_keys[g], group[g]],
                           None if c == 0 else alpha[t - 2, g])
            if sm_item is not None and sm_item[2][g]:
                qb, c, n_keys, own = sm_item
                s = s_ref[(t - 1) % 2, :n_keys[g], group[g]]
                if own[g]:
                    s = causal_tail(s)
                m_old = jnp.full((1, QUERY_GROUP), -jnp.inf, F32) if c == 0 else m[g]
                m[g], alpha[t - 1, g], p_ref[(t - 1) % 2, :n_keys[g], group[g]] = softmax(s, m_old)
        if pv_item is not None and pv_item[1] == (pv_item[0] + 1) * chunks_per_block - 1:
            finish(pv_item[0])


def _attn_prompt(q, kb, vt, cum_t, sga, *, batch, seq, blk, kv_sub, heads_per_step):
    n, d_a = q.shape
    n_heads = d_a // HEAD_DIM
    assert blk % kv_sub == 0 and seq % blk == 0 and n_heads % heads_per_step == 0
    n_chunks = seq // kv_sub
    width = heads_per_step * HEAD_DIM
    wide = heads_per_step * blk
    per_seq = lambda b, h: (b, h)
    return pl.pallas_call(
        functools.partial(_attn_prompt_kernel, blk=blk),
        grid=(batch, n_heads // heads_per_step),
        in_specs=[
            pl.BlockSpec((seq, width), per_seq),
            pl.BlockSpec((seq, width), per_seq),
            pl.BlockSpec((n_chunks, width, kv_sub), lambda b, h: (b, h, 0)),
            pl.BlockSpec((heads_per_step, None, LANES, seq // LANES), lambda b, h: (h, b, 0, 0)),
            pl.BlockSpec((seq, width), per_seq),
        ],
        out_specs=pl.BlockSpec((seq, width), per_seq),
        out_shape=jax.ShapeDtypeStruct((n, d_a), BF16),
        scratch_shapes=[pltpu.VMEM((heads_per_step, seq, HEAD_DIM + LANES), BF16),
                        pltpu.VMEM((2, kv_sub, wide), F32),
                        pltpu.VMEM((2, kv_sub, wide), BF16),
                        pltpu.VMEM((HEAD_DIM + SUM_ROWS, wide), F32),
                        pltpu.VMEM((wide, HEAD_DIM + LANES), BF16)],
        compiler_params=_compiler_params(("parallel", "parallel")),
        name="attn_prompt",
    )(q, kb, vt, cum_t, sga)


def _attn_sample_kernel(q_ref, kn_ref, vn_ref, ck_ref, cv_ref, cp_ref, cn_ref, sga_ref, o_ref,
                        m_ref, l_ref, acc_ref, *, n_heads):
    c = pl.program_id(1)

    @pl.when(c == 0)
    def _():
        m_ref[...] = jnp.full(m_ref.shape, -jnp.inf, F32)
        l_ref[...] = jnp.zeros(l_ref.shape, F32)
        acc_ref[...] = jnp.zeros(acc_ref.shape, F32)

    def head_cols(h):
        return slice(h * HEAD_DIM, (h + 1) * HEAD_DIM)

    chunk = ck_ref.shape[0] // n_heads

    def cached_head(ref, h):
        return ref[pl.ds(h, chunk, stride=n_heads), :].astype(BF16)

    heads = range(n_heads)
    s = [lax.dot_general(q_ref[:, head_cols(h)], cached_head(ck_ref, h), _NT,
                         preferred_element_type=F32) + cp_ref[h:h + 1, :] * (-LOG2E)
         for h in heads]
    m_new = [jnp.maximum(m_ref[h], jnp.max(s[h], axis=1, keepdims=True)) for h in heads]
    p = [jnp.exp2(s[h] - m_new[h]) for h in heads]
    pv = [jnp.dot(p[h].astype(BF16), cached_head(cv_ref, h), preferred_element_type=F32)
          for h in heads]
    for h in heads:
        alpha = jnp.exp2(m_ref[h] - m_new[h])
        l_ref[h] = alpha * l_ref[h] + jnp.sum(p[h], axis=1, keepdims=True)
        acc_ref[h] = alpha * acc_ref[h] + pv[h]
        m_ref[h] = m_new[h]

    @pl.when(c == pl.num_programs(1) - 1)
    def _():
        for h in range(n_heads):
            q = q_ref[:, head_cols(h)]
            s = lax.dot_general(q, kn_ref[:, head_cols(h)], _NT, preferred_element_type=F32)
            s = s + cn_ref[h:h + 1, :] * (-LOG2E)
            rows = lax.broadcasted_iota(jnp.int32, s.shape, 0)
            cols = lax.broadcasted_iota(jnp.int32, s.shape, 1)
            s = jnp.where(cols <= rows, s, -jnp.inf)
            _, l, acc = _online_softmax_step(s, vn_ref[:, head_cols(h)],
                                             m_ref[h], l_ref[h], acc_ref[h])
            o_ref[:, head_cols(h)] = (
                acc / l * sga_ref[:, head_cols(h)].astype(F32)).astype(BF16)


def _attn_sample(q, k_new, v_new, cache_k2, cache_v2, cum_past, cum_new, sga, *,
                 batch, t_new, past, chunk):
    n, d_a = q.shape
    n_heads = d_a // HEAD_DIM
    n_chunks = past // chunk
    new_map = lambda b, c: (b, 0)
    cache_map = lambda b, c: (b * n_chunks + c, 0)
    return pl.pallas_call(
        functools.partial(_attn_sample_kernel, n_heads=n_heads),
        grid=(batch, n_chunks),
        in_specs=[
            pl.BlockSpec((t_new, d_a), new_map),
            pl.BlockSpec((t_new, d_a), new_map),
            pl.BlockSpec((t_new, d_a), new_map),
            pl.BlockSpec((chunk * n_heads, HEAD_DIM), cache_map),
            pl.BlockSpec((chunk * n_heads, HEAD_DIM), cache_map),
            pl.BlockSpec((None, None, n_heads, chunk), lambda b, c: (b, c, 0, 0)),
            pl.BlockSpec((None, n_heads, t_new), lambda b, c: (b, 0, 0)),
            pl.BlockSpec((t_new, d_a), new_map),
        ],
        out_specs=pl.BlockSpec((t_new, d_a), new_map),
        out_shape=jax.ShapeDtypeStruct((n, d_a), BF16),
        scratch_shapes=[pltpu.VMEM((n_heads, t_new, 1), F32),
                        pltpu.VMEM((n_heads, t_new, 1), F32),
                        pltpu.VMEM((n_heads, t_new, HEAD_DIM), F32)],
        compiler_params=_compiler_params(("parallel", "arbitrary")),
        name="attn_sample",
    )(q, k_new, v_new, cache_k2, cache_v2, cum_past, cum_new, sga)


def _out_proj_kernel(oa_ref, ob_ref, w_ref, x_ref, fg_ref, y_ref, *, final_norm):
    o = jnp.concatenate([oa_ref[...], ob_ref[...]], axis=1)
    y = x_ref[...] + jnp.dot(o, w_ref[...], preferred_element_type=F32)
    if final_norm:
        y = _rmsnorm_rows(y, fg_ref[...])
    y_ref[...] = y


def _out_proj(out_a, out_b, w_out, x2, final_g, *, tm, final_norm):
    n, d = x2.shape
    d_a = out_a.shape[1]
    d_b = out_b.shape[1]
    row = lambda i: (i, 0)
    const = lambda i: (0, 0)
    return pl.pallas_call(
        functools.partial(_out_proj_kernel, final_norm=final_norm),
        grid=(n // tm,),
        in_specs=[
            pl.BlockSpec((tm, d_a), row),
            pl.BlockSpec((tm, d_b), row),
            pl.BlockSpec(w_out.shape, const),
            pl.BlockSpec((tm, d), row),
            pl.BlockSpec((1, d), const),
        ],
        out_specs=pl.BlockSpec((tm, d), row),
        out_shape=jax.ShapeDtypeStruct((n, d), F32),
        compiler_params=_compiler_params(("parallel",)),
        name="out_proj",
    )(out_a, out_b, w_out, x2, final_g)


def _pad_rows(a, rows):
    return jnp.pad(a, ((0, rows - a.shape[0]), (0, 0)))


def _mixing_weights(w_s, b_s, seq):
    c = min(seq, GMLP_CHUNK)
    reps = GMLP_CHUNK // c
    w = w_s[:, :c, :c] * jnp.tril(jnp.ones((c, c), w_s.dtype))
    if reps > 1:
        w = jnp.einsum('rs,gab->grasb', jnp.eye(reps, dtype=w.dtype), w)
        w = w.reshape(w_s.shape[0], GMLP_CHUNK, GMLP_CHUNK)
    b = jnp.tile(b_s[:, :c], (1, reps))
    b_full = jnp.broadcast_to(b[:, :, None], (b.shape[0], GMLP_CHUNK, LANES))
    return w.astype(BF16), b_full.astype(F32)


def _split_w_in_kernel(wt_ref, qkv_ref, rest_ref, *, off_rest):
    wt = wt_ref[...]
    qkv_ref[...] = wt[:qkv_ref.shape[1], :].T.astype(BF16)
    rest_ref[...] = wt[off_rest:, :].T.astype(BF16)


def _split_w_in(w_in_t, off_f, off_rest, *, tc=W_SPLIT_COLS):
    d_in, d = w_in_t.shape
    return pl.pallas_call(
        functools.partial(_split_w_in_kernel, off_rest=off_rest),
        grid=(d // tc,),
        in_specs=[pl.BlockSpec((d_in, tc), lambda i: (0, i))],
        out_specs=[pl.BlockSpec((tc, off_f), lambda i: (i, 0)),
                   pl.BlockSpec((tc, d_in - off_rest), lambda i: (i, 0))],
        out_shape=[jax.ShapeDtypeStruct((d, off_f), BF16),
                   jax.ShapeDtypeStruct((d, d_in - off_rest), BF16)],
        compiler_params=_compiler_params(("parallel",)),
        name="split_w_in",
    )(w_in_t)


def _prepare_weights(params):
    norm_g, w_in, b_f, ln_g, ln_b, w_s, b_s, w_out = params
    n_heads = b_f.shape[0]
    off_f = 3 * n_heads * HEAD_DIM
    off_ga = off_f + n_heads
    w_in_t = w_in.T
    w_qkv, w_rest = _split_w_in(w_in_t, off_f, off_ga)
    return dict(
        norm_g=norm_g[None], b_f=b_f[:, None], ln_g=ln_g[None], ln_b=ln_b[None],
        w_qkv=w_qkv, w_rest=w_rest,
        w_ft=_pad_rows(w_in_t[off_f:off_ga], 16).astype(BF16),
        w_out=w_out.astype(BF16), w_s=w_s, b_s=b_s)


def _layer(x, w, caches, final_g, *, final_norm, tm):
    batch, seq, d = x.shape
    n = batch * seq
    n_heads = w['b_f'].shape[0]
    d_b = w['ln_g'].shape[1]

    x2 = x.reshape(n, d)
    w_mix, b_mix = _mixing_weights(w['w_s'], w['b_s'], seq)

    blk, kv_sub = ATTN_QUERY_BLOCK, ATTN_KEY_CHUNK
    tm_wide = min(n, WIDE_ROW_TILE)
    h, q, logft = _norm_q(x2, w['norm_g'], w['w_qkv'], w['w_ft'], w['b_f'], tm=tm_wide)
    k, kb, v, vb, sga = _proj_kv(h, w['w_qkv'], w['w_rest'], n_heads=n_heads, tm=tm,
                                 v_sub=kv_sub if caches is None else None)
    gm = _proj_gmlp(h, w['w_rest'], w['ln_g'], w['ln_b'], w_mix, b_mix,
                    tm=tm_wide, emit_vn=caches is not None)

    logf = logft.T.reshape(batch, seq, n_heads)
    if caches is None:
        out_b = gm[0]
        vn = None
        group = seq // LANES
        cum = _cumsum_rows(logft.reshape(n_heads * batch * group, LANES), group)
        cum_t = cum.reshape(n_heads, batch, group, LANES).transpose(0, 1, 3, 2)
        out_a = _attn_prompt(q, kb, vb, cum_t, sga, batch=batch, seq=seq, blk=blk, kv_sub=kv_sub,
                             heads_per_step=ATTN_HEADS_PER_STEP)
    else:
        out_b, vn = gm
        cache_k, cache_v, cache_logf = caches
        past = cache_k.shape[1]
        chunk = CACHE_CHUNK
        total = past + seq
        group = -(-total // LANES)
        lf_all = jnp.concatenate(
            [cache_logf.astype(F32).transpose(0, 2, 1), logf.transpose(0, 2, 1)], axis=2)
        lf_all = jnp.pad(lf_all, ((0, 0), (0, 0), (0, group * LANES - total)))
        cum = _cumsum_rows(lf_all.reshape(batch * n_heads * group, LANES), group)
        cum = cum.reshape(batch, n_heads, group * LANES)
        cum_past = cum[:, :, :past].reshape(batch, n_heads, past // chunk, chunk)
        cum_past = cum_past.transpose(0, 2, 1, 3)
        cum_new = cum[:, :, past:total]
        out_a = _attn_sample(q, kb, vb, cache_k.reshape(batch * past * n_heads, HEAD_DIM),
                             cache_v.reshape(batch * past * n_heads, HEAD_DIM),
                             cum_past, cum_new, sga,
                             batch=batch, t_new=seq, past=past, chunk=chunk)

    y2 = _out_proj(out_a, out_b, w['w_out'], x2, final_g[None], tm=tm, final_norm=final_norm)
    y = y2.reshape(batch, seq, d)
    k4 = k.reshape(batch, seq, n_heads, HEAD_DIM)
    v4 = v.reshape(batch, seq, n_heads, HEAD_DIM)
    vn3 = None if vn is None else vn.reshape(batch, seq, d_b)
    return y, k4, v4, logf, vn3


def kernel(x_prompt, x_sample, cache_k, cache_v, cache_logf, norm_g, w_in, b_f, ln_g, ln_b,
           w_s, b_s, w_out, final_g):
    depth = norm_g.shape[0]
    hp, hs = x_prompt, x_sample
    kp, vp, fp, ksm, vsm, fsm, gsm = [], [], [], [], [], [], []
    n_sample = x_sample.shape[0] * x_sample.shape[1]
    for l in range(depth):
        w = _prepare_weights(
            (norm_g[l], w_in[l], b_f[l], ln_g[l], ln_b[l], w_s[l], b_s[l], w_out[l]))
        last = l == depth - 1
        hp, k1, v1, f1, _ = _layer(hp, w, None, final_g, final_norm=last, tm=ROW_TILE)
        hs, k2, v2, f2, g2 = _layer(hs, w, (cache_k[l], cache_v[l], cache_logf[l]), final_g,
                                    final_norm=last, tm=n_sample)
        kp.append(k1); vp.append(v1); fp.append(f1)
        ksm.append(k2); vsm.append(v2); fsm.append(f2); gsm.append(g2)
    return (hp, hs, jnp.stack(kp), jnp.stack(vp), jnp.stack(fp),
            jnp.stack(ksm), jnp.stack(vsm), jnp.stack(fsm), jnp.stack(gsm))
```

```python
import functools
import math

import jax
import jax.numpy as jnp
from jax import lax
from jax.experimental import pallas as pl
from jax.experimental.pallas import tpu as pltpu

F32 = jnp.float32
BF16 = jnp.bfloat16

HEAD_DIM = 128
GMLP_CHUNK = 128
RMS_EPS = 1e-6
LN_EPS = 1e-5
LOG2E = math.log2(math.e)
LANES = 128
VMEM_LIMIT_BYTES = 48 * 1024 * 1024

KV_VMEM_LIMIT_BYTES = 58 * 1024 * 1024
ROW_TILE = 512
WIDE_ROW_TILE = 1024
ATTN_QUERY_BLOCK = 1024
ATTN_KEY_CHUNK = 512
ATTN_HEADS_PER_STEP = 1
CACHE_CHUNK = 2048
W_SPLIT_COLS = 256

_NT = (((1,), (1,)), ((), ()))


def _compiler_params(semantics, vmem_limit_bytes=VMEM_LIMIT_BYTES):
    return pltpu.CompilerParams(dimension_semantics=semantics,
                                vmem_limit_bytes=vmem_limit_bytes)


def _rmsnorm_rows(x, g):
    return x * lax.rsqrt(jnp.mean(x * x, axis=-1, keepdims=True) + RMS_EPS) * g


def _gelu_tanh(x):
    c = math.sqrt(2.0 / math.pi)
    return 0.5 * x * (1.0 + jnp.tanh(c * (x + 0.044715 * (x * x * x))))


def _silu(x):
    return x * jax.nn.sigmoid(x)


def _store_heads_on_sublanes(ref, z, n_heads):
    rows = z.shape[0]
    for h in range(n_heads):
        ref[pl.ds(h, rows, stride=n_heads), :] = z[:, h * HEAD_DIM:(h + 1) * HEAD_DIM]


def _project(h, w_ref):
    return jnp.dot(h, w_ref[...], preferred_element_type=F32)


def _resident(shape, block_index):
    return pl.BlockSpec(shape, lambda i: block_index, pipeline_mode=pl.Buffered(1))


def _norm_q_kernel(x_ref, g_ref, wq_ref, wft_ref, bf_ref, h_ref, q_ref, logft_ref,
                   *, q_scale, n_heads):
    hb = _rmsnorm_rows(x_ref[...], g_ref[...]).astype(BF16)
    h_ref[...] = hb
    q_ref[...] = (_project(hb, wq_ref) * q_scale).astype(BF16)
    zf = lax.dot_general(wft_ref[...], hb, _NT, preferred_element_type=F32)
    t = zf[:n_heads] + bf_ref[...]
    logft_ref[...] = jnp.minimum(t, 0.0) - jnp.log1p(jnp.exp(-jnp.abs(t)))


def _norm_q(x2, norm_g, w_qkv, w_ft, b_f, *, tm):
    n, d = x2.shape
    n_heads = b_f.shape[0]
    d_a = n_heads * HEAD_DIM
    row = lambda i: (i, 0)
    return pl.pallas_call(
        functools.partial(_norm_q_kernel, q_scale=HEAD_DIM ** -0.5 * LOG2E, n_heads=n_heads),
        grid=(n // tm,),
        in_specs=[
            pl.BlockSpec((tm, d), row),
            _resident((1, d), (0, 0)),
            _resident((d, d_a), (0, 0)),
            _resident(w_ft.shape, (0, 0)),
            _resident((n_heads, 1), (0, 0)),
        ],
        out_specs=[
            pl.BlockSpec((tm, d), row),
            pl.BlockSpec((tm, d_a), row),
            pl.BlockSpec((n_heads, tm), lambda i: (0, i)),
        ],
        out_shape=[
            jax.ShapeDtypeStruct((n, d), BF16),
            jax.ShapeDtypeStruct((n, d_a), BF16),
            jax.ShapeDtypeStruct((n_heads, n), F32),
        ],
        compiler_params=_compiler_params(("parallel",)),
        name="norm_q",
    )(x2, norm_g, w_qkv, w_ft, b_f)


def _proj_kv_kernel(h_ref, wk_ref, wv_ref, wga_ref, k_ref, kb_ref, v_ref, vb_ref, sga_ref,
                    *, n_heads, v_sub):
    h = h_ref[...]
    z = _project(h, wk_ref)
    _store_heads_on_sublanes(k_ref, z, n_heads)
    kb_ref[...] = z.astype(BF16)
    z = _project(h, wv_ref)
    _store_heads_on_sublanes(v_ref, z, n_heads)
    if v_sub is None:
        vb_ref[...] = z.astype(BF16)
    else:
        for c in range(z.shape[0] // v_sub):
            vb_ref[c] = z[c * v_sub:(c + 1) * v_sub, :].T.astype(BF16)
    sga_ref[...] = _silu(_project(h, wga_ref)).astype(BF16)


def _proj_kv(h, w_qkv, w_rest, *, n_heads, tm, v_sub=None):
    n, d = h.shape
    d_a = n_heads * HEAD_DIM
    row = lambda i: (i, 0)
    if v_sub is None:
        vb_spec = pl.BlockSpec((tm, d_a), row)
        vb_shape = jax.ShapeDtypeStruct((n, d_a), BF16)
    else:
        vb_spec = pl.BlockSpec((tm // v_sub, d_a, v_sub), lambda i: (i, 0, 0))
        vb_shape = jax.ShapeDtypeStruct((n // v_sub, d_a, v_sub), BF16)
    return pl.pallas_call(
        functools.partial(_proj_kv_kernel, n_heads=n_heads, v_sub=v_sub),
        grid=(n // tm,),
        in_specs=[
            pl.BlockSpec((tm, d), row),
            _resident((d, d_a), (0, 1)),
            _resident((d, d_a), (0, 2)),
            _resident((d, d_a), (0, 0)),
        ],
        out_specs=[
            pl.BlockSpec((tm * n_heads, HEAD_DIM), row),
            pl.BlockSpec((tm, d_a), row),
            pl.BlockSpec((tm * n_heads, HEAD_DIM), row),
            vb_spec,
            pl.BlockSpec((tm, d_a), row),
        ],
        out_shape=[
            jax.ShapeDtypeStruct((n * n_heads, HEAD_DIM), F32),
            jax.ShapeDtypeStruct((n, d_a), BF16),
            jax.ShapeDtypeStruct((n * n_heads, HEAD_DIM), F32),
            vb_shape,
            jax.ShapeDtypeStruct((n, d_a), BF16),
        ],
        compiler_params=_compiler_params(("parallel",), KV_VMEM_LIMIT_BYTES),
        name="proj_kv",
    )(h, w_qkv, w_qkv, w_rest)


def _proj_gmlp_kernel(h_ref, wu_ref, wv_ref, wg_ref, lng_ref, lnb_ref, wmix_ref, bmix_ref,
                      ob_ref, *rest, n_groups, emit_vn):
    if emit_vn:
        vn_ref, mix_ref, gu_ref = rest
    else:
        mix_ref, gu_ref = rest
    h = h_ref[...]
    tm = h.shape[0]
    n_chunks = tm // GMLP_CHUNK
    d_g = wv_ref.shape[1] // n_groups

    a = _gelu_tanh(_project(h, wv_ref))
    gu_ref[...] = _gelu_tanh(_project(h, wu_ref))
    mu = jnp.mean(a, axis=-1, keepdims=True)
    ac = a - mu
    var = jnp.mean(ac * ac, axis=-1, keepdims=True)
    vn = ac * lax.rsqrt(var + LN_EPS) * lng_ref[...] + lnb_ref[...]
    if emit_vn:
        vn_ref[...] = vn
    vb = vn.astype(BF16)
    for g in range(n_groups):
        cols = slice(g * d_g, (g + 1) * d_g)
        rhs = jnp.concatenate(
            [vb[r * GMLP_CHUNK:(r + 1) * GMLP_CHUNK, cols] for r in range(n_chunks)], axis=1)
        mixed = jnp.dot(wmix_ref[g], rhs, preferred_element_type=F32)
        for r in range(n_chunks):
            mix_ref[r * GMLP_CHUNK:(r + 1) * GMLP_CHUNK, cols] = (
                mixed[:, r * d_g:(r + 1) * d_g] + bmix_ref[g])

    gate = _silu(_project(h, wg_ref))
    ob_ref[...] = (gu_ref[...] * mix_ref[...] * gate).astype(BF16)


def _proj_gmlp(h, w_rest, ln_g, ln_b, w_mix, b_mix, *, tm, emit_vn):
    n, d = h.shape
    d_b = ln_g.shape[1]
    n_groups = w_mix.shape[0]
    row = lambda i: (i, 0)
    out_specs = [pl.BlockSpec((tm, d_b), row)]
    out_shape = [jax.ShapeDtypeStruct((n, d_b), BF16)]
    if emit_vn:
        out_specs.append(pl.BlockSpec((tm, d_b), row))
        out_shape.append(jax.ShapeDtypeStruct((n, d_b), F32))
    return pl.pallas_call(
        functools.partial(_proj_gmlp_kernel, n_groups=n_groups, emit_vn=emit_vn),
        grid=(n // tm,),
        in_specs=[
            pl.BlockSpec((tm, d), row),
            _resident((d, d_b), (0, 1)),
            _resident((d, d_b), (0, 2)),
            _resident((d, d_b), (0, 3)),
            _resident((1, d_b), (0, 0)),
            _resident((1, d_b), (0, 0)),
            _resident(w_mix.shape, (0, 0, 0)),
            _resident(b_mix.shape, (0, 0, 0)),
        ],
        out_specs=out_specs,
        out_shape=out_shape,
        scratch_shapes=[pltpu.VMEM((tm, d_b), F32), pltpu.VMEM((tm, d_b), F32)],
        compiler_params=_compiler_params(("parallel",)),
        name="proj_gmlp",
    )(h, w_rest, w_rest, w_rest, ln_g, ln_b, w_mix, b_mix)


def _cumsum_kernel(x_ref, o_ref, *, group):
    x = x_ref[...]
    lane = lax.broadcasted_iota(jnp.int32, x.shape, 1)
    sh = 1
    while sh < LANES:
        x = x + jnp.where(lane >= sh, pltpu.roll(x, sh, axis=1), 0.0)
        sh *= 2
    sub = lax.broadcasted_iota(jnp.int32, x.shape, 0) % group
    tot = jnp.broadcast_to(x[:, LANES - 1:LANES], x.shape)
    carry = jnp.where(sub >= 1, pltpu.roll(tot, 1, axis=0), 0.0)
    sh = 1
    while sh < group:
        carry = carry + jnp.where(sub >= sh, pltpu.roll(carry, sh, axis=0), 0.0)
        sh *= 2
    o_ref[...] = x + carry


def _cumsum_rows(x2, group):
    return pl.pallas_call(
        functools.partial(_cumsum_kernel, group=group),
        out_shape=jax.ShapeDtypeStruct(x2.shape, F32),
        name="cumsum_rows",
    )(x2)


def _online_softmax_step(s, v_blk, m, l, acc):
    m_new = jnp.maximum(m, jnp.max(s, axis=1, keepdims=True))
    alpha = jnp.exp2(m - m_new)
    p = jnp.exp2(s - m_new)
    l = alpha * l + jnp.sum(p, axis=1, keepdims=True)
    acc = alpha * acc + jnp.dot(p.astype(BF16), v_blk, preferred_element_type=F32)
    return m_new, l, acc


QUERY_GROUP = 256
N_BIAS_PARTS = 3
SUM_ROWS = 16


def _attn_prompt_kernel(q_ref, kb_ref, vt_ref, ct_ref, sga_ref, o_ref,
                        ka_ref, s_ref, p_ref, acc_ref, qa_ref, *, blk):
    kv_sub = s_ref.shape[1]
    seq = q_ref.shape[0]
    n_heads = q_ref.shape[1] // HEAD_DIM
    lane = lax.broadcasted_iota(jnp.int32, (LANES, LANES), 1)

    def head_cols(hh):
        return slice(hh * HEAD_DIM, (hh + 1) * HEAD_DIM)

    def load_keys(c):
        for hh in range(n_heads):
            for sb in range(c * kv_sub // LANES, (c + 1) * kv_sub // LANES):
                rows = slice(sb * LANES, (sb + 1) * LANES)
                rest = jnp.broadcast_to(ct_ref[hh, :, sb:sb + 1] * (-LOG2E), (LANES, LANES))
                extra = jnp.zeros((LANES, LANES), F32)
                for part in range(N_BIAS_PARTS):
                    piece = rest.astype(BF16).astype(F32)
                    extra = jnp.where(lane == part, piece, extra)
                    rest = rest - piece
                ka_ref[hh, rows, :] = jnp.concatenate(
                    [kb_ref[rows, head_cols(hh)], extra.astype(BF16)], axis=1)

    ones_feat = jnp.where(lax.broadcasted_iota(jnp.int32, (blk, LANES), 1) < N_BIAS_PARTS,
                          1.0, 0.0).astype(BF16)
    ones_rows = jnp.ones((SUM_ROWS, kv_sub), BF16)

    def load_queries(qb):
        for hh in range(n_heads):
            qa_ref[hh * blk:(hh + 1) * blk, :] = jnp.concatenate(
                [q_ref[qb * blk:(qb + 1) * blk, head_cols(hh)], ones_feat], axis=1)

    groups_per_head = blk // QUERY_GROUP
    n_groups = n_heads * groups_per_head
    group = [slice(t * QUERY_GROUP, (t + 1) * QUERY_GROUP) for t in range(n_groups)]
    head_of = [t // groups_per_head for t in range(n_groups)]
    order = [hh * groups_per_head + g for g in range(groups_per_head) for hh in range(n_heads)]

    def scores(c, g, n_keys):
        keys = slice(c * kv_sub, c * kv_sub + n_keys)
        return lax.dot_general(ka_ref[head_of[g], keys, :], qa_ref[group[g], :],
                               _NT, preferred_element_type=F32)

    own_tile = (QUERY_GROUP, QUERY_GROUP)
    not_after = (lax.broadcasted_iota(jnp.int32, own_tile, 0)
                 <= lax.broadcasted_iota(jnp.int32, own_tile, 1))

    def causal_tail(s):
        n_before = s.shape[0] - QUERY_GROUP
        tail = jnp.where(not_after, s[n_before:], -jnp.inf)
        return tail if n_before == 0 else jnp.concatenate([s[:n_before], tail], axis=0)

    def softmax(s, m):
        m_new = jnp.maximum(m, jnp.max(s, axis=0, keepdims=True))
        return m_new, jnp.exp2(m - m_new), jnp.exp2(s - m_new).astype(BF16)

    def accumulate(c, g, p, alpha):
        n_keys = p.shape[0]
        vt_sum = jnp.concatenate([vt_ref[c, head_cols(head_of[g]), :][:, :n_keys],
                                  ones_rows[:, :n_keys]], axis=0)
        pv = jnp.dot(vt_sum, p, preferred_element_type=F32)
        acc_ref[:, group[g]] = pv if alpha is None else alpha * acc_ref[:, group[g]] + pv

    def finish(qb):
        rows = slice(qb * blk, (qb + 1) * blk)
        for hh in range(n_heads):
            cols = slice(hh * blk, (hh + 1) * blk)
            out = (acc_ref[:HEAD_DIM, cols] / acc_ref[HEAD_DIM:HEAD_DIM + 1, cols]).T
            o_ref[rows, head_cols(hh)] = (
                out * sga_ref[rows, head_cols(hh)].astype(F32)).astype(BF16)

    chunks_per_block = blk // kv_sub
    items = []
    for qb in range(seq // blk):
        for c in range((qb + 1) * chunks_per_block):
            first_key = c * kv_sub - qb * blk
            n_keys, own = [], []
            for g in range(n_groups):
                first_query = (g % groups_per_head) * QUERY_GROUP
                n_keys.append(min(max(first_query + QUERY_GROUP - first_key, 0), kv_sub))
                own.append(first_key <= first_query < first_key + kv_sub)
            items.append((qb, c, n_keys, own))

    m = [None] * n_groups
    alpha = {}
    for t in range(len(items) + 2):
        qk_item = items[t] if t < len(items) else None
        sm_item = items[t - 1] if 1 <= t <= len(items) else None
        pv_item = items[t - 2] if t >= 2 else None
        if qk_item is not None:
            if qk_item[1] // chunks_per_block == qk_item[0]:
                load_keys(qk_item[1])
            if qk_item[1] == 0:
                load_queries(qk_item[0])
        for g in order:
            if qk_item is not None and qk_item[2][g]:
                qb, c, n_keys, _ = qk_item
                s_ref[t % 2, :n_keys[g], group[g]] = scores(c, g, n_keys[g])
            if pv_item is not None and pv_item[2][g]:
                qb, c, n_keys, _ = pv_item
                accumulate(c, g, p_ref[t % 2, :n_keys[g], group[g]],
                           None if c == 0 else alpha[t - 2, g])
            if sm_item is not None and sm_item[2][g]:
                qb, c, n_keys, own = sm_item
                s = s_ref[(t - 1) % 2, :n_keys[g], group[g]]
                if own[g]:
                    s = causal_tail(s)
                m_old = jnp.full((1, QUERY_GROUP), -jnp.inf, F32) if c == 0 else m[g]
                m[g], alpha[t - 1, g], p_ref[(t - 1) % 2, :n_keys[g], group[g]] = softmax(s, m_old)
        if pv_item is not None and pv_item[1] == (pv_item[0] + 1) * chunks_per_block - 1:
            finish(pv_item[0])


def _attn_prompt(q, kb, vt, cum_t, sga, *, batch, seq, blk, kv_sub, heads_per_step):
    n, d_a = q.shape
    n_heads = d_a // HEAD_DIM
    assert blk % kv_sub == 0 and seq % blk == 0 and n_heads % heads_per_step == 0
    n_chunks = seq // kv_sub
    width = heads_per_step * HEAD_DIM
    wide = heads_per_step * blk
    per_seq = lambda b, h: (b, h)
    return pl.pallas_call(
        functools.partial(_attn_prompt_kernel, blk=blk),
        grid=(batch, n_heads // heads_per_step),
        in_specs=[
            pl.BlockSpec((seq, width), per_seq),
            pl.BlockSpec((seq, width), per_seq),
            pl.BlockSpec((n_chunks, width, kv_sub), lambda b, h: (b, h, 0)),
            pl.BlockSpec((heads_per_step, None, LANES, seq // LANES), lambda b, h: (h, b, 0, 0)),
            pl.BlockSpec((seq, width), per_seq),
        ],
        out_specs=pl.BlockSpec((seq, width), per_seq),
        out_shape=jax.ShapeDtypeStruct((n, d_a), BF16),
        scratch_shapes=[pltpu.VMEM((heads_per_step, seq, HEAD_DIM + LANES), BF16),
                        pltpu.VMEM((2, kv_sub, wide), F32),
                        pltpu.VMEM((2, kv_sub, wide), BF16),
                        pltpu.VMEM((HEAD_DIM + SUM_ROWS, wide), F32),
                        pltpu.VMEM((wide, HEAD_DIM + LANES), BF16)],
        compiler_params=_compiler_params(("parallel", "parallel")),
        name="attn_prompt",
    )(q, kb, vt, cum_t, sga)


def _attn_sample_kernel(q_ref, kn_ref, vn_ref, ck_ref, cv_ref, cp_ref, cn_ref, sga_ref, o_ref,
                        m_ref, l_ref, acc_ref, *, n_heads):
    c = pl.program_id(1)

    @pl.when(c == 0)
    def _():
        m_ref[...] = jnp.full(m_ref.shape, -jnp.inf, F32)
        l_ref[...] = jnp.zeros(l_ref.shape, F32)
        acc_ref[...] = jnp.zeros(acc_ref.shape, F32)

    def head_cols(h):
        return slice(h * HEAD_DIM, (h + 1) * HEAD_DIM)

    chunk = ck_ref.shape[0] // n_heads

    def cached_head(ref, h):
        return ref[pl.ds(h, chunk, stride=n_heads), :].astype(BF16)

    heads = range(n_heads)
    s = [lax.dot_general(q_ref[:, head_cols(h)], cached_head(ck_ref, h), _NT,
                         preferred_element_type=F32) + cp_ref[h:h + 1, :] * (-LOG2E)
         for h in heads]
    m_new = [jnp.maximum(m_ref[h], jnp.max(s[h], axis=1, keepdims=True)) for h in heads]
    p = [jnp.exp2(s[h] - m_new[h]) for h in heads]
    pv = [jnp.dot(p[h].astype(BF16), cached_head(cv_ref, h), preferred_element_type=F32)
          for h in heads]
    for h in heads:
        alpha = jnp.exp2(m_ref[h] - m_new[h])
        l_ref[h] = alpha * l_ref[h] + jnp.sum(p[h], axis=1, keepdims=True)
        acc_ref[h] = alpha * acc_ref[h] + pv[h]
        m_ref[h] = m_new[h]

    @pl.when(c == pl.num_programs(1) - 1)
    def _():
        for h in range(n_heads):
            q = q_ref[:, head_cols(h)]
            s = lax.dot_general(q, kn_ref[:, head_cols(h)], _NT, preferred_element_type=F32)
            s = s + cn_ref[h:h + 1, :] * (-LOG2E)
            rows = lax.broadcasted_iota(jnp.int32, s.shape, 0)
            cols = lax.broadcasted_iota(jnp.int32, s.shape, 1)
            s = jnp.where(cols <= rows, s, -jnp.inf)
            _, l, acc = _online_softmax_step(s, vn_ref[:, head_cols(h)],
                                             m_ref[h], l_ref[h], acc_ref[h])
            o_ref[:, head_cols(h)] = (
                acc / l * sga_ref[:, head_cols(h)].astype(F32)).astype(BF16)


def _attn_sample(q, k_new, v_new, cache_k2, cache_v2, cum_past, cum_new, sga, *,
                 batch, t_new, past, chunk):
    n, d_a = q.shape
    n_heads = d_a // HEAD_DIM
    n_chunks = past // chunk
    new_map = lambda b, c: (b, 0)
    cache_map = lambda b, c: (b * n_chunks + c, 0)
    return pl.pallas_call(
        functools.partial(_attn_sample_kernel, n_heads=n_heads),
        grid=(batch, n_chunks),
        in_specs=[
            pl.BlockSpec((t_new, d_a), new_map),
            pl.BlockSpec((t_new, d_a), new_map),
            pl.BlockSpec((t_new, d_a), new_map),
            pl.BlockSpec((chunk * n_heads, HEAD_DIM), cache_map),
            pl.BlockSpec((chunk * n_heads, HEAD_DIM), cache_map),
            pl.BlockSpec((None, None, n_heads, chunk), lambda b, c: (b, c, 0, 0)),
            pl.BlockSpec((None, n_heads, t_new), lambda b, c: (b, 0, 0)),
            pl.BlockSpec((t_new, d_a), new_map),
        ],
        out_specs=pl.BlockSpec((t_new, d_a), new_map),
        out_shape=jax.ShapeDtypeStruct((n, d_a), BF16),
        scratch_shapes=[pltpu.VMEM((n_heads, t_new, 1), F32),
                        pltpu.VMEM((n_heads, t_new, 1), F32),
                        pltpu.VMEM((n_heads, t_new, HEAD_DIM), F32)],
        compiler_params=_compiler_params(("parallel", "arbitrary")),
        name="attn_sample",
    )(q, k_new, v_new, cache_k2, cache_v2, cum_past, cum_new, sga)


def _out_proj_kernel(oa_ref, ob_ref, w_ref, x_ref, fg_ref, y_ref, *, final_norm):
    o = jnp.concatenate([oa_ref[...], ob_ref[...]], axis=1)
    y = x_ref[...] + jnp.dot(o, w_ref[...], preferred_element_type=F32)
    if final_norm:
        y = _rmsnorm_rows(y, fg_ref[...])
    y_ref[...] = y


def _out_proj(out_a, out_b, w_out, x2, final_g, *, tm, final_norm):
    n, d = x2.shape
    d_a = out_a.shape[1]
    d_b = out_b.shape[1]
    row = lambda i: (i, 0)
    const = lambda i: (0, 0)
    return pl.pallas_call(
        functools.partial(_out_proj_kernel, final_norm=final_norm),
        grid=(n // tm,),
        in_specs=[
            pl.BlockSpec((tm, d_a), row),
            pl.BlockSpec((tm, d_b), row),
            pl.BlockSpec(w_out.shape, const),
            pl.BlockSpec((tm, d), row),
            pl.BlockSpec((1, d), const),
        ],
        out_specs=pl.BlockSpec((tm, d), row),
        out_shape=jax.ShapeDtypeStruct((n, d), F32),
        compiler_params=_compiler_params(("parallel",)),
        name="out_proj",
    )(out_a, out_b, w_out, x2, final_g)


def _pad_rows(a, rows):
    return jnp.pad(a, ((0, rows - a.shape[0]), (0, 0)))


def _mixing_weights(w_s, b_s, seq):
    c = min(seq, GMLP_CHUNK)
    reps = GMLP_CHUNK // c
    w = w_s[:, :c, :c] * jnp.tril(jnp.ones((c, c), w_s.dtype))
    if reps > 1:
        w = jnp.einsum('rs,gab->grasb', jnp.eye(reps, dtype=w.dtype), w)
        w = w.reshape(w_s.shape[0], GMLP_CHUNK, GMLP_CHUNK)
    b = jnp.tile(b_s[:, :c], (1, reps))
    b_full = jnp.broadcast_to(b[:, :, None], (b.shape[0], GMLP_CHUNK, LANES))
    return w.astype(BF16), b_full.astype(F32)


def _split_w_in_kernel(wt_ref, qkv_ref, rest_ref, *, off_rest):
    wt = wt_ref[...]
    qkv_ref[...] = wt[:qkv_ref.shape[1], :].T.astype(BF16)
    rest_ref[...] = wt[off_rest:, :].T.astype(BF16)


def _split_w_in(w_in_t, off_f, off_rest, *, tc=W_SPLIT_COLS):
    d_in, d = w_in_t.shape
    return pl.pallas_call(
        functools.partial(_split_w_in_kernel, off_rest=off_rest),
        grid=(d // tc,),
        in_specs=[pl.BlockSpec((d_in, tc), lambda i: (0, i))],
        out_specs=[pl.BlockSpec((tc, off_f), lambda i: (i, 0)),
                   pl.BlockSpec((tc, d_in - off_rest), lambda i: (i, 0))],
        out_shape=[jax.ShapeDtypeStruct((d, off_f), BF16),
                   jax.ShapeDtypeStruct((d, d_in - off_rest), BF16)],
        compiler_params=_compiler_params(("parallel",)),
        name="split_w_in",
    )(w_in_t)


def _prepare_weights(params):
    norm_g, w_in, b_f, ln_g, ln_b, w_s, b_s, w_out = params
    n_heads = b_f.shape[0]
    off_f = 3 * n_heads * HEAD_DIM
    off_ga = off_f + n_heads
    w_in_t = w_in.T
    w_qkv, w_rest = _split_w_in(w_in_t, off_f, off_ga)
    return dict(
        norm_g=norm_g[None], b_f=b_f[:, None], ln_g=ln_g[None], ln_b=ln_b[None],
        w_qkv=w_qkv, w_rest=w_rest,
        w_ft=_pad_rows(w_in_t[off_f:off_ga], 16).astype(BF16),
        w_out=w_out.astype(BF16), w_s=w_s, b_s=b_s)


def _layer(x, w, caches, final_g, *, final_norm, tm):
    batch, seq, d = x.shape
    n = batch * seq
    n_heads = w['b_f'].shape[0]
    d_b = w['ln_g'].shape[1]

    x2 = x.reshape(n, d)
    w_mix, b_mix = _mixing_weights(w['w_s'], w['b_s'], seq)

    blk, kv_sub = ATTN_QUERY_BLOCK, ATTN_KEY_CHUNK
    tm_wide = min(n, WIDE_ROW_TILE)
    h, q, logft = _norm_q(x2, w['norm_g'], w['w_qkv'], w['w_ft'], w['b_f'], tm=tm_wide)
    k, kb, v, vb, sga = _proj_kv(h, w['w_qkv'], w['w_rest'], n_heads=n_heads, tm=tm_wide,
                                 v_sub=kv_sub if caches is None else None)
    gm = _proj_gmlp(h, w['w_rest'], w['ln_g'], w['ln_b'], w_mix, b_mix,
                    tm=tm_wide, emit_vn=caches is not None)

    logf = logft.T.reshape(batch, seq, n_heads)
    if caches is None:
        out_b = gm[0]
        vn = None
        group = seq // LANES
        cum = _cumsum_rows(logft.reshape(n_heads * batch * group, LANES), group)
        cum_t = cum.reshape(n_heads, batch, group, LANES).transpose(0, 1, 3, 2)
        out_a = _attn_prompt(q, kb, vb, cum_t, sga, batch=batch, seq=seq, blk=blk, kv_sub=kv_sub,
                             heads_per_step=ATTN_HEADS_PER_STEP)
    else:
        out_b, vn = gm
        cache_k, cache_v, cache_logf = caches
        past = cache_k.shape[1]
        chunk = CACHE_CHUNK
        total = past + seq
        group = -(-total // LANES)
        lf_all = jnp.concatenate(
            [cache_logf.astype(F32).transpose(0, 2, 1), logf.transpose(0, 2, 1)], axis=2)
        lf_all = jnp.pad(lf_all, ((0, 0), (0, 0), (0, group * LANES - total)))
        cum = _cumsum_rows(lf_all.reshape(batch * n_heads * group, LANES), group)
        cum = cum.reshape(batch, n_heads, group * LANES)
        cum_past = cum[:, :, :past].reshape(batch, n_heads, past // chunk, chunk)
        cum_past = cum_past.transpose(0, 2, 1, 3)
        cum_new = cum[:, :, past:total]
        out_a = _attn_sample(q, kb, vb, cache_k.reshape(batch * past * n_heads, HEAD_DIM),
                             cache_v.reshape(batch * past * n_heads, HEAD_DIM),
                             cum_past, cum_new, sga,
                             batch=batch, t_new=seq, past=past, chunk=chunk)

    y2 = _out_proj(out_a, out_b, w['w_out'], x2, final_g[None], tm=tm, final_norm=final_norm)
    y = y2.reshape(batch, seq, d)
    k4 = k.reshape(batch, seq, n_heads, HEAD_DIM)
    v4 = v.reshape(batch, seq, n_heads, HEAD_DIM)
    vn3 = None if vn is None else vn.reshape(batch, seq, d_b)
    return y, k4, v4, logf, vn3


def kernel(x_prompt, x_sample, cache_k, cache_v, cache_logf, norm_g, w_in, b_f, ln_g, ln_b,
           w_s, b_s, w_out, final_g):
    depth = norm_g.shape[0]
    hp, hs = x_prompt, x_sample
    kp, vp, fp, ksm, vsm, fsm, gsm = [], [], [], [], [], [], []
    n_sample = x_sample.shape[0] * x_sample.shape[1]
    for l in range(depth):
        w = _prepare_weights(
            (norm_g[l], w_in[l], b_f[l], ln_g[l], ln_b[l], w_s[l], b_s[l], w_out[l]))
        last = l == depth - 1
        hp, k1, v1, f1, _ = _layer(hp, w, None, final_g, final_norm=last, tm=ROW_TILE)
        hs, k2, v2, f2, g2 = _layer(hs, w, (cache_k[l], cache_v[l], cache_logf[l]), final_g,
                                    final_norm=last, tm=n_sample)
        kp.append(k1); vp.append(v1); fp.append(f1)
        ksm.append(k2); vsm.append(v2); fsm.append(f2); gsm.append(g2)
    return (hp, hs, jnp.stack(kp), jnp.stack(vp), jnp.stack(fp),
            jnp.stack(ksm), jnp.stack(vsm), jnp.stack(fsm), jnp.stack(gsm))
```

```python
import functools
import math

import jax
import jax.numpy as jnp
from jax import lax
from jax.experimental import pallas as pl
from jax.experimental.pallas import tpu as pltpu

F32 = jnp.float32
BF16 = jnp.bfloat16

HEAD_DIM = 128
GMLP_CHUNK = 128
RMS_EPS = 1e-6
LN_EPS = 1e-5
LOG2E = math.log2(math.e)
LANES = 128
VMEM_LIMIT_BYTES = 48 * 1024 * 1024

ROW_TILE = 512
WIDE_ROW_TILE = 1024
ATTN_QUERY_BLOCK = 1024
ATTN_KEY_CHUNK = 512
ATTN_HEADS_PER_STEP = 1
CACHE_CHUNK = 1024
W_SPLIT_COLS = 256

_NT = (((1,), (1,)), ((), ()))


def _compiler_params(semantics):
    return pltpu.CompilerParams(dimension_semantics=semantics,
                                vmem_limit_bytes=VMEM_LIMIT_BYTES)


def _rmsnorm_rows(x, g):
    return x * lax.rsqrt(jnp.mean(x * x, axis=-1, keepdims=True) + RMS_EPS) * g


def _gelu_tanh(x):
    c = math.sqrt(2.0 / math.pi)
    return 0.5 * x * (1.0 + jnp.tanh(c * (x + 0.044715 * (x * x * x))))


def _silu(x):
    return x * jax.nn.sigmoid(x)


def _store_heads_on_sublanes(ref, z, n_heads):
    rows = z.shape[0]
    for h in range(n_heads):
        ref[pl.ds(h, rows, stride=n_heads), :] = z[:, h * HEAD_DIM:(h + 1) * HEAD_DIM]


def _project(h, w_ref):
    return jnp.dot(h, w_ref[...], preferred_element_type=F32)


def _resident(shape, block_index):
    return pl.BlockSpec(shape, lambda i: block_index, pipeline_mode=pl.Buffered(1))


def _norm_q_kernel(x_ref, g_ref, wq_ref, wft_ref, bf_ref, h_ref, q_ref, logft_ref,
                   *, q_scale, n_heads):
    hb = _rmsnorm_rows(x_ref[...], g_ref[...]).astype(BF16)
    h_ref[...] = hb
    q_ref[...] = (_project(hb, wq_ref) * q_scale).astype(BF16)
    zf = lax.dot_general(wft_ref[...], hb, _NT, preferred_element_type=F32)
    t = zf[:n_heads] + bf_ref[...]
    logft_ref[...] = jnp.minimum(t, 0.0) - jnp.log1p(jnp.exp(-jnp.abs(t)))


def _norm_q(x2, norm_g, w_qkv, w_ft, b_f, *, tm):
    n, d = x2.shape
    n_heads = b_f.shape[0]
    d_a = n_heads * HEAD_DIM
    row = lambda i: (i, 0)
    return pl.pallas_call(
        functools.partial(_norm_q_kernel, q_scale=HEAD_DIM ** -0.5 * LOG2E, n_heads=n_heads),
        grid=(n // tm,),
        in_specs=[
            pl.BlockSpec((tm, d), row),
            _resident((1, d), (0, 0)),
            _resident((d, d_a), (0, 0)),
            _resident(w_ft.shape, (0, 0)),
            _resident((n_heads, 1), (0, 0)),
        ],
        out_specs=[
            pl.BlockSpec((tm, d), row),
            pl.BlockSpec((tm, d_a), row),
            pl.BlockSpec((n_heads, tm), lambda i: (0, i)),
        ],
        out_shape=[
            jax.ShapeDtypeStruct((n, d), BF16),
            jax.ShapeDtypeStruct((n, d_a), BF16),
            jax.ShapeDtypeStruct((n_heads, n), F32),
        ],
        compiler_params=_compiler_params(("parallel",)),
        name="norm_q",
    )(x2, norm_g, w_qkv, w_ft, b_f)


def _proj_kv_kernel(h_ref, wk_ref, wv_ref, wga_ref, k_ref, kb_ref, v_ref, vb_ref, sga_ref,
                    *, n_heads, v_sub):
    h = h_ref[...]
    z = _project(h, wk_ref)
    _store_heads_on_sublanes(k_ref, z, n_heads)
    kb_ref[...] = z.astype(BF16)
    z = _project(h, wv_ref)
    _store_heads_on_sublanes(v_ref, z, n_heads)
    if v_sub is None:
        vb_ref[...] = z.astype(BF16)
    else:
        for c in range(z.shape[0] // v_sub):
            vb_ref[c] = z[c * v_sub:(c + 1) * v_sub, :].T.astype(BF16)
    sga_ref[...] = _silu(_project(h, wga_ref)).astype(BF16)


def _proj_kv(h, w_qkv, w_rest, *, n_heads, tm, v_sub=None):
    n, d = h.shape
    d_a = n_heads * HEAD_DIM
    row = lambda i: (i, 0)
    if v_sub is None:
        vb_spec = pl.BlockSpec((tm, d_a), row)
        vb_shape = jax.ShapeDtypeStruct((n, d_a), BF16)
    else:
        vb_spec = pl.BlockSpec((tm // v_sub, d_a, v_sub), lambda i: (i, 0, 0))
        vb_shape = jax.ShapeDtypeStruct((n // v_sub, d_a, v_sub), BF16)
    return pl.pallas_call(
        functools.partial(_proj_kv_kernel, n_heads=n_heads, v_sub=v_sub),
        grid=(n // tm,),
        in_specs=[
            pl.BlockSpec((tm, d), row),
            _resident((d, d_a), (0, 1)),
            _resident((d, d_a), (0, 2)),
            _resident((d, d_a), (0, 0)),
        ],
        out_specs=[
            pl.BlockSpec((tm * n_heads, HEAD_DIM), row),
            pl.BlockSpec((tm, d_a), row),
            pl.BlockSpec((tm * n_heads, HEAD_DIM), row),
            vb_spec,
            pl.BlockSpec((tm, d_a), row),
        ],
        out_shape=[
            jax.ShapeDtypeStruct((n * n_heads, HEAD_DIM), F32),
            jax.ShapeDtypeStruct((n, d_a), BF16),
            jax.ShapeDtypeStruct((n * n_heads, HEAD_DIM), F32),
            vb_shape,
            jax.ShapeDtypeStruct((n, d_a), BF16),
        ],
        compiler_params=_compiler_params(("parallel",)),
        name="proj_kv",
    )(h, w_qkv, w_qkv, w_rest)


def _proj_gmlp_kernel(h_ref, wu_ref, wv_ref, wg_ref, lng_ref, lnb_ref, wmix_ref, bmix_ref,
                      ob_ref, *rest, n_groups, emit_vn):
    if emit_vn:
        vn_ref, mix_ref, gu_ref = rest
    else:
        mix_ref, gu_ref = rest
    h = h_ref[...]
    tm = h.shape[0]
    n_chunks = tm // GMLP_CHUNK
    d_g = wv_ref.shape[1] // n_groups

    a = _gelu_tanh(_project(h, wv_ref))
    gu_ref[...] = _gelu_tanh(_project(h, wu_ref))
    mu = jnp.mean(a, axis=-1, keepdims=True)
    ac = a - mu
    var = jnp.mean(ac * ac, axis=-1, keepdims=True)
    vn = ac * lax.rsqrt(var + LN_EPS) * lng_ref[...] + lnb_ref[...]
    if emit_vn:
        vn_ref[...] = vn
    vb = vn.astype(BF16)
    for g in range(n_groups):
        cols = slice(g * d_g, (g + 1) * d_g)
        rhs = jnp.concatenate(
            [vb[r * GMLP_CHUNK:(r + 1) * GMLP_CHUNK, cols] for r in range(n_chunks)], axis=1)
        mixed = jnp.dot(wmix_ref[g], rhs, preferred_element_type=F32)
        for r in range(n_chunks):
            mix_ref[r * GMLP_CHUNK:(r + 1) * GMLP_CHUNK, cols] = (
                mixed[:, r * d_g:(r + 1) * d_g] + bmix_ref[g])

    gate = _silu(_project(h, wg_ref))
    ob_ref[...] = (gu_ref[...] * mix_ref[...] * gate).astype(BF16)


def _proj_gmlp(h, w_rest, ln_g, ln_b, w_mix, b_mix, *, tm, emit_vn):
    n, d = h.shape
    d_b = ln_g.shape[1]
    n_groups = w_mix.shape[0]
    row = lambda i: (i, 0)
    out_specs = [pl.BlockSpec((tm, d_b), row)]
    out_shape = [jax.ShapeDtypeStruct((n, d_b), BF16)]
    if emit_vn:
        out_specs.append(pl.BlockSpec((tm, d_b), row))
        out_shape.append(jax.ShapeDtypeStruct((n, d_b), F32))
    return pl.pallas_call(
        functools.partial(_proj_gmlp_kernel, n_groups=n_groups, emit_vn=emit_vn),
        grid=(n // tm,),
        in_specs=[
            pl.BlockSpec((tm, d), row),
            _resident((d, d_b), (0, 1)),
            _resident((d, d_b), (0, 2)),
            _resident((d, d_b), (0, 3)),
            _resident((1, d_b), (0, 0)),
            _resident((1, d_b), (0, 0)),
            _resident(w_mix.shape, (0, 0, 0)),
            _resident(b_mix.shape, (0, 0, 0)),
        ],
        out_specs=out_specs,
        out_shape=out_shape,
        scratch_shapes=[pltpu.VMEM((tm, d_b), F32), pltpu.VMEM((tm, d_b), F32)],
        compiler_params=_compiler_params(("parallel",)),
        name="proj_gmlp",
    )(h, w_rest, w_rest, w_rest, ln_g, ln_b, w_mix, b_mix)


def _cumsum_kernel(x_ref, o_ref, *, group):
    x = x_ref[...]
    lane = lax.broadcasted_iota(jnp.int32, x.shape, 1)
    sh = 1
    while sh < LANES:
        x = x + jnp.where(lane >= sh, pltpu.roll(x, sh, axis=1), 0.0)
        sh *= 2
    sub = lax.broadcasted_iota(jnp.int32, x.shape, 0) % group
    tot = jnp.broadcast_to(x[:, LANES - 1:LANES], x.shape)
    carry = jnp.where(sub >= 1, pltpu.roll(tot, 1, axis=0), 0.0)
    sh = 1
    while sh < group:
        carry = carry + jnp.where(sub >= sh, pltpu.roll(carry, sh, axis=0), 0.0)
        sh *= 2
    o_ref[...] = x + carry


def _cumsum_rows(x2, group):
    return pl.pallas_call(
        functools.partial(_cumsum_kernel, group=group),
        out_shape=jax.ShapeDtypeStruct(x2.shape, F32),
        name="cumsum_rows",
    )(x2)


def _online_softmax_step(s, v_blk, m, l, acc):
    m_new = jnp.maximum(m, jnp.max(s, axis=1, keepdims=True))
    alpha = jnp.exp2(m - m_new)
    p = jnp.exp2(s - m_new)
    l = alpha * l + jnp.sum(p, axis=1, keepdims=True)
    acc = alpha * acc + jnp.dot(p.astype(BF16), v_blk, preferred_element_type=F32)
    return m_new, l, acc


QUERY_GROUP = 256
N_BIAS_PARTS = 3
SUM_ROWS = 16


def _attn_prompt_kernel(q_ref, kb_ref, vt_ref, ct_ref, sga_ref, o_ref,
                        ka_ref, s_ref, p_ref, acc_ref, qa_ref, *, blk):
    kv_sub = s_ref.shape[1]
    seq = q_ref.shape[0]
    n_heads = q_ref.shape[1] // HEAD_DIM
    lane = lax.broadcasted_iota(jnp.int32, (LANES, LANES), 1)

    def head_cols(hh):
        return slice(hh * HEAD_DIM, (hh + 1) * HEAD_DIM)

    def load_keys(c):
        for hh in range(n_heads):
            for sb in range(c * kv_sub // LANES, (c + 1) * kv_sub // LANES):
                rows = slice(sb * LANES, (sb + 1) * LANES)
                rest = jnp.broadcast_to(ct_ref[hh, :, sb:sb + 1] * (-LOG2E), (LANES, LANES))
                extra = jnp.zeros((LANES, LANES), F32)
                for part in range(N_BIAS_PARTS):
                    piece = rest.astype(BF16).astype(F32)
                    extra = jnp.where(lane == part, piece, extra)
                    rest = rest - piece
                ka_ref[hh, rows, :] = jnp.concatenate(
                    [kb_ref[rows, head_cols(hh)], extra.astype(BF16)], axis=1)

    ones_feat = jnp.where(lax.broadcasted_iota(jnp.int32, (blk, LANES), 1) < N_BIAS_PARTS,
                          1.0, 0.0).astype(BF16)
    ones_rows = jnp.ones((SUM_ROWS, kv_sub), BF16)

    def load_queries(qb):
        for hh in range(n_heads):
            qa_ref[hh * blk:(hh + 1) * blk, :] = jnp.concatenate(
                [q_ref[qb * blk:(qb + 1) * blk, head_cols(hh)], ones_feat], axis=1)

    groups_per_head = blk // QUERY_GROUP
    n_groups = n_heads * groups_per_head
    group = [slice(t * QUERY_GROUP, (t + 1) * QUERY_GROUP) for t in range(n_groups)]
    head_of = [t // groups_per_head for t in range(n_groups)]
    order = [hh * groups_per_head + g for g in range(groups_per_head) for hh in range(n_heads)]

    def scores(c, g, n_keys):
        keys = slice(c * kv_sub, c * kv_sub + n_keys)
        return lax.dot_general(ka_ref[head_of[g], keys, :], qa_ref[group[g], :],
                               _NT, preferred_element_type=F32)

    own_tile = (QUERY_GROUP, QUERY_GROUP)
    not_after = (lax.broadcasted_iota(jnp.int32, own_tile, 0)
                 <= lax.broadcasted_iota(jnp.int32, own_tile, 1))

    def causal_tail(s):
        n_before = s.shape[0] - QUERY_GROUP
        tail = jnp.where(not_after, s[n_before:], -jnp.inf)
        return tail if n_before == 0 else jnp.concatenate([s[:n_before], tail], axis=0)

    def softmax(s, m):
        m_new = jnp.maximum(m, jnp.max(s, axis=0, keepdims=True))
        return m_new, jnp.exp2(m - m_new), jnp.exp2(s - m_new).astype(BF16)

    def accumulate(c, g, p, alpha):
        n_keys = p.shape[0]
        vt_sum = jnp.concatenate([vt_ref[c, head_cols(head_of[g]), :][:, :n_keys],
                                  ones_rows[:, :n_keys]], axis=0)
        pv = jnp.dot(vt_sum, p, preferred_element_type=F32)
        acc_ref[:, group[g]] = pv if alpha is None else alpha * acc_ref[:, group[g]] + pv

    def finish(qb):
        rows = slice(qb * blk, (qb + 1) * blk)
        for hh in range(n_heads):
            cols = slice(hh * blk, (hh + 1) * blk)
            out = (acc_ref[:HEAD_DIM, cols] / acc_ref[HEAD_DIM:HEAD_DIM + 1, cols]).T
            o_ref[rows, head_cols(hh)] = (
                out * sga_ref[rows, head_cols(hh)].astype(F32)).astype(BF16)

    chunks_per_block = blk // kv_sub
    items = []
    for qb in range(seq // blk):
        for c in range((qb + 1) * chunks_per_block):
            first_key = c * kv_sub - qb * blk
            n_keys, own = [], []
            for g in range(n_groups):
                first_query = (g % groups_per_head) * QUERY_GROUP
                n_keys.append(min(max(first_query + QUERY_GROUP - first_key, 0), kv_sub))
                own.append(first_key <= first_query < first_key + kv_sub)
            items.append((qb, c, n_keys, own))

    m = [None] * n_groups
    alpha = {}
    for t in range(len(items) + 2):
        qk_item = items[t] if t < len(items) else None
        sm_item = items[t - 1] if 1 <= t <= len(items) else None
        pv_item = items[t - 2] if t >= 2 else None
        if qk_item is not None:
            if qk_item[1] // chunks_per_block == qk_item[0]:
                load_keys(qk_item[1])
            if qk_item[1] == 0:
                load_queries(qk_item[0])
        for g in order:
            if qk_item is not None and qk_item[2][g]:
                qb, c, n_keys, _ = qk_item
                s_ref[t % 2, :n_keys[g], group[g]] = scores(c, g, n_keys[g])
            if pv_item is not None and pv_item[2][g]:
                qb, c, n_keys, _ = pv_item
                accumulate(c, g, p_ref[t % 2, :n_keys[g], group[g]],
                           None if c == 0 else alpha[t - 2, g])
            if sm_item is not None and sm_item[2][g]:
                qb, c, n_keys, own = sm_item
                s = s_ref[(t - 1) % 2, :n_keys[g], group[g]]
                if own[g]:
                    s = causal_tail(s)
                m_old = jnp.full((1, QUERY_GROUP), -jnp.inf, F32) if c == 0 else m[g]
                m[g], alpha[t - 1, g], p_ref[(t - 1) % 2, :n_keys[g], group[g]] = softmax(s, m_old)
        if pv_item is not None and pv_item[1] == (pv_item[0] + 1) * chunks_per_block - 1:
            finish(pv_item[0])


def _attn_prompt(q, kb, vt, cum_t, sga, *, batch, seq, blk, kv_sub, heads_per_step):
    n, d_a = q.shape
    n_heads = d_a // HEAD_DIM
    assert blk % kv_sub == 0 and seq % blk == 0 and n_heads % heads_per_step == 0
    n_chunks = seq // kv_sub
    width = heads_per_step * HEAD_DIM
    wide = heads_per_step * blk
    per_seq = lambda b, h: (b, h)
    return pl.pallas_call(
        functools.partial(_attn_prompt_kernel, blk=blk),
        grid=(batch, n_heads // heads_per_step),
        in_specs=[
            pl.BlockSpec((seq, width), per_seq),
            pl.BlockSpec((seq, width), per_seq),
            pl.BlockSpec((n_chunks, width, kv_sub), lambda b, h: (b, h, 0)),
            pl.BlockSpec((heads_per_step, None, LANES, seq // LANES), lambda b, h: (h, b, 0, 0)),
            pl.BlockSpec((seq, width), per_seq),
        ],
        out_specs=pl.BlockSpec((seq, width), per_seq),
        out_shape=jax.ShapeDtypeStruct((n, d_a), BF16),
        scratch_shapes=[pltpu.VMEM((heads_per_step, seq, HEAD_DIM + LANES), BF16),
                        pltpu.VMEM((2, kv_sub, wide), F32),
                        pltpu.VMEM((2, kv_sub, wide), BF16),
                        pltpu.VMEM((HEAD_DIM + SUM_ROWS, wide), F32),
                        pltpu.VMEM((wide, HEAD_DIM + LANES), BF16)],
        compiler_params=_compiler_params(("parallel", "parallel")),
        name="attn_prompt",
    )(q, kb, vt, cum_t, sga)


def _attn_sample_kernel(q_ref, kn_ref, vn_ref, ck_ref, cv_ref, cp_ref, cn_ref, sga_ref, o_ref,
                        m_ref, l_ref, acc_ref, *, n_heads):
    c = pl.program_id(1)

    @pl.when(c == 0)
    def _():
        m_ref[...] = jnp.full(m_ref.shape, -jnp.inf, F32)
        l_ref[...] = jnp.zeros(l_ref.shape, F32)
        acc_ref[...] = jnp.zeros(acc_ref.shape, F32)

    def head_cols(h):
        return slice(h * HEAD_DIM, (h + 1) * HEAD_DIM)

    chunk = ck_ref.shape[0] // n_heads

    def cached_head(ref, h):
        return ref[pl.ds(h, chunk, stride=n_heads), :].astype(BF16)

    heads = range(n_heads)
    s = [lax.dot_general(q_ref[:, head_cols(h)], cached_head(ck_ref, h), _NT,
                         preferred_element_type=F32) + cp_ref[h:h + 1, :] * (-LOG2E)
         for h in heads]
    m_new = [jnp.maximum(m_ref[h], jnp.max(s[h], axis=1, keepdims=True)) for h in heads]
    p = [jnp.exp2(s[h] - m_new[h]) for h in heads]
    pv = [jnp.dot(p[h].astype(BF16), cached_head(cv_ref, h), preferred_element_type=F32)
          for h in heads]
    for h in heads:
        alpha = jnp.exp2(m_ref[h] - m_new[h])
        l_ref[h] = alpha * l_ref[h] + jnp.sum(p[h], axis=1, keepdims=True)
        acc_ref[h] = alpha * acc_ref[h] + pv[h]
        m_ref[h] = m_new[h]

    @pl.when(c == pl.num_programs(1) - 1)
    def _():
        for h in range(n_heads):
            q = q_ref[:, head_cols(h)]
            s = lax.dot_general(q, kn_ref[:, head_cols(h)], _NT, preferred_element_type=F32)
            s = s + cn_ref[h:h + 1, :] * (-LOG2E)
            rows = lax.broadcasted_iota(jnp.int32, s.shape, 0)
            cols = lax.broadcasted_iota(jnp.int32, s.shape, 1)
            s = jnp.where(cols <= rows, s, -jnp.inf)
            _, l, acc = _online_softmax_step(s, vn_ref[:, head_cols(h)],
                                             m_ref[h], l_ref[h], acc_ref[h])
            o_ref[:, head_cols(h)] = (
                acc / l * sga_ref[:, head_cols(h)].astype(F32)).astype(BF16)


def _attn_sample(q, k_new, v_new, cache_k2, cache_v2, cum_past, cum_new, sga, *,
                 batch, t_new, past, chunk):
    n, d_a = q.shape
    n_heads = d_a // HEAD_DIM
    n_chunks = past // chunk
    new_map = lambda b, c: (b, 0)
    cache_map = lambda b, c: (b * n_chunks + c, 0)
    return pl.pallas_call(
        functools.partial(_attn_sample_kernel, n_heads=n_heads),
        grid=(batch, n_chunks),
        in_specs=[
            pl.BlockSpec((t_new, d_a), new_map),
            pl.BlockSpec((t_new, d_a), new_map),
            pl.BlockSpec((t_new, d_a), new_map),
            pl.BlockSpec((chunk * n_heads, HEAD_DIM), cache_map),
            pl.BlockSpec((chunk * n_heads, HEAD_DIM), cache_map),
            pl.BlockSpec((None, None, n_heads, chunk), lambda b, c: (b, c, 0, 0)),
            pl.BlockSpec((None, n_heads, t_new), lambda b, c: (b, 0, 0)),
            pl.BlockSpec((t_new, d_a), new_map),
        ],
        out_specs=pl.BlockSpec((t_new, d_a), new_map),
        out_shape=jax.ShapeDtypeStruct((n, d_a), BF16),
        scratch_shapes=[pltpu.VMEM((n_heads, t_new, 1), F32),
                        pltpu.VMEM((n_heads, t_new, 1), F32),
                        pltpu.VMEM((n_heads, t_new, HEAD_DIM), F32)],
        compiler_params=_compiler_params(("parallel", "arbitrary")),
        name="attn_sample",
    )(q, k_new, v_new, cache_k2, cache_v2, cum_past, cum_new, sga)


def _out_proj_kernel(oa_ref, ob_ref, w_ref, x_ref, fg_ref, y_ref, *, final_norm):
    o = jnp.concatenate([oa_ref[...], ob_ref[...]], axis=1)
    y = x_ref[...] + jnp.dot(o, w_ref[...], preferred_element_type=F32)
    if final_norm:
        y = _rmsnorm_rows(y, fg_ref[...])
    y_ref[...] = y


def _out_proj(out_a, out_b, w_out, x2, final_g, *, tm, final_norm):
    n, d = x2.shape
    d_a = out_a.shape[1]
    d_b = out_b.shape[1]
    row = lambda i: (i, 0)
    const = lambda i: (0, 0)
    return pl.pallas_call(
        functools.partial(_out_proj_kernel, final_norm=final_norm),
        grid=(n // tm,),
        in_specs=[
            pl.BlockSpec((tm, d_a), row),
            pl.BlockSpec((tm, d_b), row),
            pl.BlockSpec(w_out.shape, const),
            pl.BlockSpec((tm, d), row),
            pl.BlockSpec((1, d), const),
        ],
        out_specs=pl.BlockSpec((tm, d), row),
        out_shape=jax.ShapeDtypeStruct((n, d), F32),
        compiler_params=_compiler_params(("parallel",)),
        name="out_proj",
    )(out_a, out_b, w_out, x2, final_g)


def _pad_rows(a, rows):
    return jnp.pad(a, ((0, rows - a.shape[0]), (0, 0)))


def _mixing_weights(w_s, b_s, seq):
    c = min(seq, GMLP_CHUNK)
    reps = GMLP_CHUNK // c
    w = w_s[:, :c, :c] * jnp.tril(jnp.ones((c, c), w_s.dtype))
    if reps > 1:
        w = jnp.einsum('rs,gab->grasb', jnp.eye(reps, dtype=w.dtype), w)
        w = w.reshape(w_s.shape[0], GMLP_CHUNK, GMLP_CHUNK)
    b = jnp.tile(b_s[:, :c], (1, reps))
    b_full = jnp.broadcast_to(b[:, :, None], (b.shape[0], GMLP_CHUNK, LANES))
    return w.astype(BF16), b_full.astype(F32)


def _split_w_in_kernel(wt_ref, qkv_ref, rest_ref, *, off_rest):
    wt = wt_ref[...]
    qkv_ref[...] = wt[:qkv_ref.shape[1], :].T.astype(BF16)
    rest_ref[...] = wt[off_rest:, :].T.astype(BF16)


def _split_w_in(w_in_t, off_f, off_rest, *, tc=W_SPLIT_COLS):
    d_in, d = w_in_t.shape
    return pl.pallas_call(
        functools.partial(_split_w_in_kernel, off_rest=off_rest),
        grid=(d // tc,),
        in_specs=[pl.BlockSpec((d_in, tc), lambda i: (0, i))],
        out_specs=[pl.BlockSpec((tc, off_f), lambda i: (i, 0)),
                   pl.BlockSpec((tc, d_in - off_rest), lambda i: (i, 0))],
        out_shape=[jax.ShapeDtypeStruct((d, off_f), BF16),
                   jax.ShapeDtypeStruct((d, d_in - off_rest), BF16)],
        compiler_params=_compiler_params(("parallel",)),
        name="split_w_in",
    )(w_in_t)


def _prepare_weights(params):
    norm_g, w_in, b_f, ln_g, ln_b, w_s, b_s, w_out = params
    n_heads = b_f.shape[0]
    off_f = 3 * n_heads * HEAD_DIM
    off_ga = off_f + n_heads
    w_in_t = w_in.T
    w_qkv, w_rest = _split_w_in(w_in_t, off_f, off_ga)
    return dict(
        norm_g=norm_g[None], b_f=b_f[:, None], ln_g=ln_g[None], ln_b=ln_b[None],
        w_qkv=w_qkv, w_rest=w_rest,
        w_ft=_pad_rows(w_in_t[off_f:off_ga], 16).astype(BF16),
        w_out=w_out.astype(BF16), w_s=w_s, b_s=b_s)


def _layer(x, w, caches, final_g, *, final_norm, tm):
    batch, seq, d = x.shape
    n = batch * seq
    n_heads = w['b_f'].shape[0]
    d_b = w['ln_g'].shape[1]

    x2 = x.reshape(n, d)
    w_mix, b_mix = _mixing_weights(w['w_s'], w['b_s'], seq)

    blk, kv_sub = ATTN_QUERY_BLOCK, ATTN_KEY_CHUNK
    tm_wide = min(n, WIDE_ROW_TILE)
    h, q, logft = _norm_q(x2, w['norm_g'], w['w_qkv'], w['w_ft'], w['b_f'], tm=tm_wide)
    k, kb, v, vb, sga = _proj_kv(h, w['w_qkv'], w['w_rest'], n_heads=n_heads, tm=tm,
                                 v_sub=kv_sub if caches is None else None)
    gm = _proj_gmlp(h, w['w_rest'], w['ln_g'], w['ln_b'], w_mix, b_mix,
                    tm=tm_wide, emit_vn=caches is not None)

    logf = logft.T.reshape(batch, seq, n_heads)
    if caches is None:
        out_b = gm[0]
        vn = None
        group = seq // LANES
        cum = _cumsum_rows(logft.reshape(n_heads * batch * group, LANES), group)
        cum_t = cum.reshape(n_heads, batch, group, LANES).transpose(0, 1, 3, 2)
        out_a = _attn_prompt(q, kb, vb, cum_t, sga, batch=batch, seq=seq, blk=blk, kv_sub=kv_sub,
                             heads_per_step=ATTN_HEADS_PER_STEP)
    else:
        out_b, vn = gm
        cache_k, cache_v, cache_logf = caches
        past = cache_k.shape[1]
        chunk = CACHE_CHUNK
        total = past + seq
        group = -(-total // LANES)
        lf_all = jnp.concatenate(
            [cache_logf.astype(F32).transpose(0, 2, 1), logf.transpose(0, 2, 1)], axis=2)
        lf_all = jnp.pad(lf_all, ((0, 0), (0, 0), (0, group * LANES - total)))
        cum = _cumsum_rows(lf_all.reshape(batch * n_heads * group, LANES), group)
        cum = cum.reshape(batch, n_heads, group * LANES)
        cum_past = cum[:, :, :past].reshape(batch, n_heads, past // chunk, chunk)
        cum_past = cum_past.transpose(0, 2, 1, 3)
        cum_new = cum[:, :, past:total]
        out_a = _attn_sample(q, kb, vb, cache_k.reshape(batch * past * n_heads, HEAD_DIM),
                             cache_v.reshape(batch * past * n_heads, HEAD_DIM),
                             cum_past, cum_new, sga,
                             batch=batch, t_new=seq, past=past, chunk=chunk)

    y2 = _out_proj(out_a, out_b, w['w_out'], x2, final_g[None], tm=tm, final_norm=final_norm)
    y = y2.reshape(batch, seq, d)
    k4 = k.reshape(batch, seq, n_heads, HEAD_DIM)
    v4 = v.reshape(batch, seq, n_heads, HEAD_DIM)
    vn3 = None if vn is None else vn.reshape(batch, seq, d_b)
    return y, k4, v4, logf, vn3


def kernel(x_prompt, x_sample, cache_k, cache_v, cache_logf, norm_g, w_in, b_f, ln_g, ln_b,
           w_s, b_s, w_out, final_g):
    depth = norm_g.shape[0]
    hp, hs = x_prompt, x_sample
    kp, vp, fp, ksm, vsm, fsm, gsm = [], [], [], [], [], [], []
    n_sample = x_sample.shape[0] * x_sample.shape[1]
    for l in range(depth):
        w = _prepare_weights(
            (norm_g[l], w_in[l], b_f[l], ln_g[l], ln_b[l], w_s[l], b_s[l], w_out[l]))
        last = l == depth - 1
        hp, k1, v1, f1, _ = _layer(hp, w, None, final_g, final_norm=last, tm=ROW_TILE)
        hs, k2, v2, f2, g2 = _layer(hs, w, (cache_k[l], cache_v[l], cache_logf[l]), final_g,
                                    final_norm=last, tm=n_sample)
        kp.append(k1); vp.append(v1); fp.append(f1)
        ksm.append(k2); vsm.append(v2); fsm.append(f2); gsm.append(g2)
    return (hp, hs, jnp.stack(kp), jnp.stack(vp), jnp.stack(fp),
            jnp.stack(ksm), jnp.stack(vsm), jnp.stack(fsm), jnp.stack(gsm))
```

```python
import functools
import math

import jax
import jax.numpy as jnp
from jax import lax
from jax.experimental import pallas as pl
from jax.experimental.pallas import tpu as pltpu

F32 = jnp.float32
BF16 = jnp.bfloat16

HEAD_DIM = 128
GMLP_CHUNK = 128
RMS_EPS = 1e-6
LN_EPS = 1e-5
LOG2E = math.log2(math.e)
LANES = 128
VMEM_LIMIT_BYTES = 48 * 1024 * 1024

ROW_TILE = 512
WIDE_ROW_TILE = 1024
ATTN_QUERY_BLOCK = 1024
ATTN_KEY_CHUNK = 512
ATTN_HEADS_PER_STEP = 1
CACHE_CHUNK = 2048
W_SPLIT_COLS = 256

_NT = (((1,), (1,)), ((), ()))


def _compiler_params(semantics):
    return pltpu.CompilerParams(dimension_semantics=semantics,
                                vmem_limit_bytes=VMEM_LIMIT_BYTES)


def _rmsnorm_rows(x, g):
    return x * lax.rsqrt(jnp.mean(x * x, axis=-1, keepdims=True) + RMS_EPS) * g


def _gelu_tanh(x):
    c = math.sqrt(2.0 / math.pi)
    return 0.5 * x * (1.0 + jnp.tanh(c * (x + 0.044715 * (x * x * x))))


def _silu(x):
    return x * jax.nn.sigmoid(x)


def _store_heads_on_sublanes(ref, z, n_heads):
    rows = z.shape[0]
    for h in range(n_heads):
        ref[pl.ds(h, rows, stride=n_heads), :] = z[:, h * HEAD_DIM:(h + 1) * HEAD_DIM]


def _project(h, w_ref):
    return jnp.dot(h, w_ref[...], preferred_element_type=F32)


def _resident(shape, block_index):
    return pl.BlockSpec(shape, lambda i: block_index, pipeline_mode=pl.Buffered(1))


def _norm_q_kernel(x_ref, g_ref, wq_ref, wft_ref, bf_ref, h_ref, q_ref, logft_ref,
                   *, q_scale, n_heads):
    hb = _rmsnorm_rows(x_ref[...], g_ref[...]).astype(BF16)
    h_ref[...] = hb
    q_ref[...] = (_project(hb, wq_ref) * q_scale).astype(BF16)
    zf = lax.dot_general(wft_ref[...], hb, _NT, preferred_element_type=F32)
    t = zf[:n_heads] + bf_ref[...]
    logft_ref[...] = jnp.minimum(t, 0.0) - jnp.log1p(jnp.exp(-jnp.abs(t)))


def _norm_q(x2, norm_g, w_qkv, w_ft, b_f, *, tm):
    n, d = x2.shape
    n_heads = b_f.shape[0]
    d_a = n_heads * HEAD_DIM
    row = lambda i: (i, 0)
    return pl.pallas_call(
        functools.partial(_norm_q_kernel, q_scale=HEAD_DIM ** -0.5 * LOG2E, n_heads=n_heads),
        grid=(n // tm,),
        in_specs=[
            pl.BlockSpec((tm, d), row),
            _resident((1, d), (0, 0)),
            _resident((d, d_a), (0, 0)),
            _resident(w_ft.shape, (0, 0)),
            _resident((n_heads, 1), (0, 0)),
        ],
        out_specs=[
            pl.BlockSpec((tm, d), row),
            pl.BlockSpec((tm, d_a), row),
            pl.BlockSpec((n_heads, tm), lambda i: (0, i)),
        ],
        out_shape=[
            jax.ShapeDtypeStruct((n, d), BF16),
            jax.ShapeDtypeStruct((n, d_a), BF16),
            jax.ShapeDtypeStruct((n_heads, n), F32),
        ],
        compiler_params=_compiler_params(("parallel",)),
        name="norm_q",
    )(x2, norm_g, w_qkv, w_ft, b_f)


def _proj_kv_kernel(h_ref, wk_ref, wv_ref, wga_ref, k_ref, kb_ref, v_ref, vb_ref, sga_ref,
                    *, n_heads, v_sub):
    h = h_ref[...]
    z = _project(h, wk_ref)
    _store_heads_on_sublanes(k_ref, z, n_heads)
    kb_ref[...] = z.astype(BF16)
    z = _project(h, wv_ref)
    _store_heads_on_sublanes(v_ref, z, n_heads)
    if v_sub is None:
        vb_ref[...] = z.astype(BF16)
    else:
        for c in range(z.shape[0] // v_sub):
            vb_ref[c] = z[c * v_sub:(c + 1) * v_sub, :].T.astype(BF16)
    sga_ref[...] = _silu(_project(h, wga_ref)).astype(BF16)


def _proj_kv(h, w_qkv, w_rest, *, n_heads, tm, v_sub=None):
    n, d = h.shape
    d_a = n_heads * HEAD_DIM
    row = lambda i: (i, 0)
    if v_sub is None:
        vb_spec = pl.BlockSpec((tm, d_a), row)
        vb_shape = jax.ShapeDtypeStruct((n, d_a), BF16)
    else:
        vb_spec = pl.BlockSpec((tm // v_sub, d_a, v_sub), lambda i: (i, 0, 0))
        vb_shape = jax.ShapeDtypeStruct((n // v_sub, d_a, v_sub), BF16)
    return pl.pallas_call(
        functools.partial(_proj_kv_kernel, n_heads=n_heads, v_sub=v_sub),
        grid=(n // tm,),
        in_specs=[
            pl.BlockSpec((tm, d), row),
            _resident((d, d_a), (0, 1)),
            _resident((d, d_a), (0, 2)),
            _resident((d, d_a), (0, 0)),
        ],
        out_specs=[
            pl.BlockSpec((tm * n_heads, HEAD_DIM), row),
            pl.BlockSpec((tm, d_a), row),
            pl.BlockSpec((tm * n_heads, HEAD_DIM), row),
            vb_spec,
            pl.BlockSpec((tm, d_a), row),
        ],
        out_shape=[
            jax.ShapeDtypeStruct((n * n_heads, HEAD_DIM), F32),
            jax.ShapeDtypeStruct((n, d_a), BF16),
            jax.ShapeDtypeStruct((n * n_heads, HEAD_DIM), F32),
            vb_shape,
            jax.ShapeDtypeStruct((n, d_a), BF16),
        ],
        compiler_params=_compiler_params(("parallel",)),
        name="proj_kv",
    )(h, w_qkv, w_qkv, w_rest)


def _proj_gmlp_kernel(h_ref, wu_ref, wv_ref, wg_ref, lng_ref, lnb_ref, wmix_ref, bmix_ref,
                      ob_ref, *rest, n_groups, emit_vn):
    if emit_vn:
        vn_ref, mix_ref, gu_ref = rest
    else:
        mix_ref, gu_ref = rest
    h = h_ref[...]
    tm = h.shape[0]
    n_chunks = tm // GMLP_CHUNK
    d_g = wv_ref.shape[1] // n_groups

    a = _gelu_tanh(_project(h, wv_ref))
    gu_ref[...] = _gelu_tanh(_project(h, wu_ref))
    mu = jnp.mean(a, axis=-1, keepdims=True)
    ac = a - mu
    var = jnp.mean(ac * ac, axis=-1, keepdims=True)
    vn = ac * lax.rsqrt(var + LN_EPS) * lng_ref[...] + lnb_ref[...]
    if emit_vn:
        vn_ref[...] = vn
    vb = vn.astype(BF16)
    for g in range(n_groups):
        cols = slice(g * d_g, (g + 1) * d_g)
        rhs = jnp.concatenate(
            [vb[r * GMLP_CHUNK:(r + 1) * GMLP_CHUNK, cols] for r in range(n_chunks)], axis=1)
        mixed = jnp.dot(wmix_ref[g], rhs, preferred_element_type=F32)
        for r in range(n_chunks):
            mix_ref[r * GMLP_CHUNK:(r + 1) * GMLP_CHUNK, cols] = (
                mixed[:, r * d_g:(r + 1) * d_g] + bmix_ref[g])

    gate = _silu(_project(h, wg_ref))
    ob_ref[...] = (gu_ref[...] * mix_ref[...] * gate).astype(BF16)


def _proj_gmlp(h, w_rest, ln_g, ln_b, w_mix, b_mix, *, tm, emit_vn):
    n, d = h.shape
    d_b = ln_g.shape[1]
    n_groups = w_mix.shape[0]
    row = lambda i: (i, 0)
    out_specs = [pl.BlockSpec((tm, d_b), row)]
    out_shape = [jax.ShapeDtypeStruct((n, d_b), BF16)]
    if emit_vn:
        out_specs.append(pl.BlockSpec((tm, d_b), row))
        out_shape.append(jax.ShapeDtypeStruct((n, d_b), F32))
    return pl.pallas_call(
        functools.partial(_proj_gmlp_kernel, n_groups=n_groups, emit_vn=emit_vn),
        grid=(n // tm,),
        in_specs=[
            pl.BlockSpec((tm, d), row),
            _resident((d, d_b), (0, 1)),
            _resident((d, d_b), (0, 2)),
            _resident((d, d_b), (0, 3)),
            _resident((1, d_b), (0, 0)),
            _resident((1, d_b), (0, 0)),
            _resident(w_mix.shape, (0, 0, 0)),
            _resident(b_mix.shape, (0, 0, 0)),
        ],
        out_specs=out_specs,
        out_shape=out_shape,
        scratch_shapes=[pltpu.VMEM((tm, d_b), F32), pltpu.VMEM((tm, d_b), F32)],
        compiler_params=_compiler_params(("parallel",)),
        name="proj_gmlp",
    )(h, w_rest, w_rest, w_rest, ln_g, ln_b, w_mix, b_mix)


def _cumsum_kernel(x_ref, o_ref, *, group):
    x = x_ref[...]
    lane = lax.broadcasted_iota(jnp.int32, x.shape, 1)
    sh = 1
    while sh < LANES:
        x = x + jnp.where(lane >= sh, pltpu.roll(x, sh, axis=1), 0.0)
        sh *= 2
    sub = lax.broadcasted_iota(jnp.int32, x.shape, 0) % group
    tot = jnp.broadcast_to(x[:, LANES - 1:LANES], x.shape)
    carry = jnp.where(sub >= 1, pltpu.roll(tot, 1, axis=0), 0.0)
    sh = 1
    while sh < group:
        carry = carry + jnp.where(sub >= sh, pltpu.roll(carry, sh, axis=0), 0.0)
        sh *= 2
    o_ref[...] = x + carry


def _cumsum_rows(x2, group):
    return pl.pallas_call(
        functools.partial(_cumsum_kernel, group=group),
        out_shape=jax.ShapeDtypeStruct(x2.shape, F32),
        name="cumsum_rows",
    )(x2)


def _online_softmax_step(s, v_blk, m, l, acc):
    m_new = jnp.maximum(m, jnp.max(s, axis=1, keepdims=True))
    alpha = jnp.exp2(m - m_new)
    p = jnp.exp2(s - m_new)
    l = alpha * l + jnp.sum(p, axis=1, keepdims=True)
    acc = alpha * acc + jnp.dot(p.astype(BF16), v_blk, preferred_element_type=F32)
    return m_new, l, acc


QUERY_GROUP = 256
N_BIAS_PARTS = 3
SUM_ROWS = 16


def _attn_prompt_kernel(q_ref, kb_ref, vt_ref, ct_ref, sga_ref, o_ref,
                        ka_ref, s_ref, p_ref, acc_ref, qa_ref, *, blk):
    kv_sub = s_ref.shape[1]
    seq = q_ref.shape[0]
    n_heads = q_ref.shape[1] // HEAD_DIM
    lane = lax.broadcasted_iota(jnp.int32, (LANES, LANES), 1)

    def head_cols(hh):
        return slice(hh * HEAD_DIM, (hh + 1) * HEAD_DIM)

    def load_keys(c):
        for hh in range(n_heads):
            for sb in range(c * kv_sub // LANES, (c + 1) * kv_sub // LANES):
                rows = slice(sb * LANES, (sb + 1) * LANES)
                rest = jnp.broadcast_to(ct_ref[hh, :, sb:sb + 1] * (-LOG2E), (LANES, LANES))
                extra = jnp.zeros((LANES, LANES), F32)
                for part in range(N_BIAS_PARTS):
                    piece = rest.astype(BF16).astype(F32)
                    extra = jnp.where(lane == part, piece, extra)
                    rest = rest - piece
                ka_ref[hh, rows, :] = jnp.concatenate(
                    [kb_ref[rows, head_cols(hh)], extra.astype(BF16)], axis=1)

    ones_feat = jnp.where(lax.broadcasted_iota(jnp.int32, (blk, LANES), 1) < N_BIAS_PARTS,
                          1.0, 0.0).astype(BF16)

    def load_queries(qb):
        for hh in range(n_heads):
            qa_ref[hh * blk:(hh + 1) * blk, :] = jnp.concatenate(
                [q_ref[qb * blk:(qb + 1) * blk, head_cols(hh)], ones_feat], axis=1)

    groups_per_head = blk // QUERY_GROUP
    n_groups = n_heads * groups_per_head
    group = [slice(t * QUERY_GROUP, (t + 1) * QUERY_GROUP) for t in range(n_groups)]
    head_of = [t // groups_per_head for t in range(n_groups)]
    order = [hh * groups_per_head + g for g in range(groups_per_head) for hh in range(n_heads)]

    def scores(c, g, n_keys):
        keys = slice(c * kv_sub, c * kv_sub + n_keys)
        return lax.dot_general(ka_ref[head_of[g], keys, :], qa_ref[group[g], :],
                               _NT, preferred_element_type=F32)

    own_tile = (QUERY_GROUP, QUERY_GROUP)
    not_after = (lax.broadcasted_iota(jnp.int32, own_tile, 0)
                 <= lax.broadcasted_iota(jnp.int32, own_tile, 1))

    def causal_tail(s):
        n_before = s.shape[0] - QUERY_GROUP
        tail = jnp.where(not_after, s[n_before:], -jnp.inf)
        return tail if n_before == 0 else jnp.concatenate([s[:n_before], tail], axis=0)

    def softmax(s, m):
        m_new = jnp.maximum(m, jnp.max(s, axis=0, keepdims=True))
        p = jnp.exp2(s - m_new)
        return m_new, jnp.exp2(m - m_new), p.astype(BF16), jnp.sum(p, axis=0, keepdims=True)

    def accumulate(c, g, p, alpha):
        n_keys = p.shape[0]
        pv = jnp.dot(vt_ref[c, head_cols(head_of[g]), :][:, :n_keys], p,
                     preferred_element_type=F32)
        acc_ref[:, group[g]] = pv if alpha is None else alpha * acc_ref[:, group[g]] + pv

    def finish(qb):
        rows = slice(qb * blk, (qb + 1) * blk)
        for hh in range(n_heads):
            cols = slice(hh * blk, (hh + 1) * blk)
            denom = jnp.concatenate(
                [l[qb, hh * groups_per_head + g] for g in range(groups_per_head)], axis=1)
            out = (acc_ref[:, cols] / denom).T
            o_ref[rows, head_cols(hh)] = (
                out * sga_ref[rows, head_cols(hh)].astype(F32)).astype(BF16)

    chunks_per_block = blk // kv_sub
    items = []
    for qb in range(seq // blk):
        for c in range((qb + 1) * chunks_per_block):
            first_key = c * kv_sub - qb * blk
            n_keys, own = [], []
            for g in range(n_groups):
                first_query = (g % groups_per_head) * QUERY_GROUP
                n_keys.append(min(max(first_query + QUERY_GROUP - first_key, 0), kv_sub))
                own.append(first_key <= first_query < first_key + kv_sub)
            items.append((qb, c, n_keys, own))

    m = [None] * n_groups
    l = {}
    alpha = {}
    for t in range(len(items) + 2):
        qk_item = items[t] if t < len(items) else None
        sm_item = items[t - 1] if 1 <= t <= len(items) else None
        pv_item = items[t - 2] if t >= 2 else None
        if qk_item is not None:
            if qk_item[1] // chunks_per_block == qk_item[0]:
                load_keys(qk_item[1])
            if qk_item[1] == 0:
                load_queries(qk_item[0])
        for g in order:
            if qk_item is not None and qk_item[2][g]:
                qb, c, n_keys, _ = qk_item
                s_ref[t % 2, :n_keys[g], group[g]] = scores(c, g, n_keys[g])
            if pv_item is not None and pv_item[2][g]:
                qb, c, n_keys, _ = pv_item
                accumulate(c, g, p_ref[t % 2, :n_keys[g], group[g]],
                           None if c == 0 else alpha[t - 2, g])
            if sm_item is not None and sm_item[2][g]:
                qb, c, n_keys, own = sm_item
                s = s_ref[(t - 1) % 2, :n_keys[g], group[g]]
                if own[g]:
                    s = causal_tail(s)
                m_old = jnp.full((1, QUERY_GROUP), -jnp.inf, F32) if c == 0 else m[g]
                m[g], a, p_ref[(t - 1) % 2, :n_keys[g], group[g]], row_sum = softmax(s, m_old)
                alpha[t - 1, g] = a
                l[qb, g] = row_sum if c == 0 else a * l[qb, g] + row_sum
        if pv_item is not None and pv_item[1] == (pv_item[0] + 1) * chunks_per_block - 1:
            finish(pv_item[0])


def _attn_prompt(q, kb, vt, cum_t, sga, *, batch, seq, blk, kv_sub, heads_per_step):
    n, d_a = q.shape
    n_heads = d_a // HEAD_DIM
    assert blk % kv_sub == 0 and seq % blk == 0 and n_heads % heads_per_step == 0
    n_chunks = seq // kv_sub
    width = heads_per_step * HEAD_DIM
    wide = heads_per_step * blk
    per_seq = lambda b, h: (b, h)
    return pl.pallas_call(
        functools.partial(_attn_prompt_kernel, blk=blk),
        grid=(batch, n_heads // heads_per_step),
        in_specs=[
            pl.BlockSpec((seq, width), per_seq),
            pl.BlockSpec((seq, width), per_seq),
            pl.BlockSpec((n_chunks, width, kv_sub), lambda b, h: (b, h, 0)),
            pl.BlockSpec((heads_per_step, None, LANES, seq // LANES), lambda b, h: (h, b, 0, 0)),
            pl.BlockSpec((seq, width), per_seq),
        ],
        out_specs=pl.BlockSpec((seq, width), per_seq),
        out_shape=jax.ShapeDtypeStruct((n, d_a), BF16),
        scratch_shapes=[pltpu.VMEM((heads_per_step, seq, HEAD_DIM + LANES), BF16),
                        pltpu.VMEM((2, kv_sub, wide), F32),
                        pltpu.VMEM((2, kv_sub, wide), BF16),
                        pltpu.VMEM((HEAD_DIM, wide), F32),
                        pltpu.VMEM((wide, HEAD_DIM + LANES), BF16)],
        compiler_params=_compiler_params(("parallel", "parallel")),
        name="attn_prompt",
    )(q, kb, vt, cum_t, sga)


def _attn_sample_kernel(q_ref, kn_ref, vn_ref, ck_ref, cv_ref, cp_ref, cn_ref, sga_ref, o_ref,
                        m_ref, l_ref, acc_ref, *, n_heads):
    c = pl.program_id(1)

    @pl.when(c == 0)
    def _():
        m_ref[...] = jnp.full(m_ref.shape, -jnp.inf, F32)
        l_ref[...] = jnp.zeros(l_ref.shape, F32)
        acc_ref[...] = jnp.zeros(acc_ref.shape, F32)

    def head_cols(h):
        return slice(h * HEAD_DIM, (h + 1) * HEAD_DIM)

    chunk = ck_ref.shape[0] // n_heads

    def cached_head(ref, h):
        return ref[pl.ds(h, chunk, stride=n_heads), :].astype(BF16)

    heads = range(n_heads)
    s = [lax.dot_general(q_ref[:, head_cols(h)], cached_head(ck_ref, h), _NT,
                         preferred_element_type=F32) + cp_ref[h:h + 1, :] * (-LOG2E)
         for h in heads]
    m_new = [jnp.maximum(m_ref[h], jnp.max(s[h], axis=1, keepdims=True)) for h in heads]
    p = [jnp.exp2(s[h] - m_new[h]) for h in heads]
    pv = [jnp.dot(p[h].astype(BF16), cached_head(cv_ref, h), preferred_element_type=F32)
          for h in heads]
    for h in heads:
        alpha = jnp.exp2(m_ref[h] - m_new[h])
        l_ref[h] = alpha * l_ref[h] + jnp.sum(p[h], axis=1, keepdims=True)
        acc_ref[h] = alpha * acc_ref[h] + pv[h]
        m_ref[h] = m_new[h]

    @pl.when(c == pl.num_programs(1) - 1)
    def _():
        for h in range(n_heads):
            q = q_ref[:, head_cols(h)]
            s = lax.dot_general(q, kn_ref[:, head_cols(h)], _NT, preferred_element_type=F32)
            s = s + cn_ref[h:h + 1, :] * (-LOG2E)
            rows = lax.broadcasted_iota(jnp.int32, s.shape, 0)
            cols = lax.broadcasted_iota(jnp.int32, s.shape, 1)
            s = jnp.where(cols <= rows, s, -jnp.inf)
            _, l, acc = _online_softmax_step(s, vn_ref[:, head_cols(h)],
                                             m_ref[h], l_ref[h], acc_ref[h])
            o_ref[:, head_cols(h)] = (
                acc / l * sga_ref[:, head_cols(h)].astype(F32)).astype(BF16)


def _attn_sample(q, k_new, v_new, cache_k2, cache_v2, cum_past, cum_new, sga, *,
                 batch, t_new, past, chunk):
    n, d_a = q.shape
    n_heads = d_a // HEAD_DIM
    n_chunks = past // chunk
    new_map = lambda b, c: (b, 0)
    cache_map = lambda b, c: (b * n_chunks + c, 0)
    return pl.pallas_call(
        functools.partial(_attn_sample_kernel, n_heads=n_heads),
        grid=(batch, n_chunks),
        in_specs=[
            pl.BlockSpec((t_new, d_a), new_map),
            pl.BlockSpec((t_new, d_a), new_map),
            pl.BlockSpec((t_new, d_a), new_map),
            pl.BlockSpec((chunk * n_heads, HEAD_DIM), cache_map),
            pl.BlockSpec((chunk * n_heads, HEAD_DIM), cache_map),
            pl.BlockSpec((None, None, n_heads, chunk), lambda b, c: (b, c, 0, 0)),
            pl.BlockSpec((None, n_heads, t_new), lambda b, c: (b, 0, 0)),
            pl.BlockSpec((t_new, d_a), new_map),
        ],
        out_specs=pl.BlockSpec((t_new, d_a), new_map),
        out_shape=jax.ShapeDtypeStruct((n, d_a), BF16),
        scratch_shapes=[pltpu.VMEM((n_heads, t_new, 1), F32),
                        pltpu.VMEM((n_heads, t_new, 1), F32),
                        pltpu.VMEM((n_heads, t_new, HEAD_DIM), F32)],
        compiler_params=_compiler_params(("parallel", "arbitrary")),
        name="attn_sample",
    )(q, k_new, v_new, cache_k2, cache_v2, cum_past, cum_new, sga)


def _out_proj_kernel(oa_ref, ob_ref, w_ref, x_ref, fg_ref, y_ref, *, final_norm):
    o = jnp.concatenate([oa_ref[...], ob_ref[...]], axis=1)
    y = x_ref[...] + jnp.dot(o, w_ref[...], preferred_element_type=F32)
    if final_norm:
        y = _rmsnorm_rows(y, fg_ref[...])
    y_ref[...] = y


def _out_proj(out_a, out_b, w_out, x2, final_g, *, tm, final_norm):
    n, d = x2.shape
    d_a = out_a.shape[1]
    d_b = out_b.shape[1]
    row = lambda i: (i, 0)
    const = lambda i: (0, 0)
    return pl.pallas_call(
        functools.partial(_out_proj_kernel, final_norm=final_norm),
        grid=(n // tm,),
        in_specs=[
            pl.BlockSpec((tm, d_a), row),
            pl.BlockSpec((tm, d_b), row),
            pl.BlockSpec(w_out.shape, const),
            pl.BlockSpec((tm, d), row),
            pl.BlockSpec((1, d), const),
        ],
        out_specs=pl.BlockSpec((tm, d), row),
        out_shape=jax.ShapeDtypeStruct((n, d), F32),
        compiler_params=_compiler_params(("parallel",)),
        name="out_proj",
    )(out_a, out_b, w_out, x2, final_g)


def _pad_rows(a, rows):
    return jnp.pad(a, ((0, rows - a.shape[0]), (0, 0)))


def _mixing_weights(w_s, b_s, seq):
    c = min(seq, GMLP_CHUNK)
    reps = GMLP_CHUNK // c
    w = w_s[:, :c, :c] * jnp.tril(jnp.ones((c, c), w_s.dtype))
    if reps > 1:
        w = jnp.einsum('rs,gab->grasb', jnp.eye(reps, dtype=w.dtype), w)
        w = w.reshape(w_s.shape[0], GMLP_CHUNK, GMLP_CHUNK)
    b = jnp.tile(b_s[:, :c], (1, reps))
    b_full = jnp.broadcast_to(b[:, :, None], (b.shape[0], GMLP_CHUNK, LANES))
    return w.astype(BF16), b_full.astype(F32)


def _split_w_in_kernel(wt_ref, qkv_ref, rest_ref, *, off_rest):
    wt = wt_ref[...]
    qkv_ref[...] = wt[:qkv_ref.shape[1], :].T.astype(BF16)
    rest_ref[...] = wt[off_rest:, :].T.astype(BF16)


def _split_w_in(w_in_t, off_f, off_rest, *, tc=W_SPLIT_COLS):
    d_in, d = w_in_t.shape
    return pl.pallas_call(
        functools.partial(_split_w_in_kernel, off_rest=off_rest),
        grid=(d // tc,),
        in_specs=[pl.BlockSpec((d_in, tc), lambda i: (0, i))],
        out_specs=[pl.BlockSpec((tc, off_f), lambda i: (i, 0)),
                   pl.BlockSpec((tc, d_in - off_rest), lambda i: (i, 0))],
        out_shape=[jax.ShapeDtypeStruct((d, off_f), BF16),
                   jax.ShapeDtypeStruct((d, d_in - off_rest), BF16)],
        compiler_params=_compiler_params(("parallel",)),
        name="split_w_in",
    )(w_in_t)


def _prepare_weights(params):
    norm_g, w_in, b_f, ln_g, ln_b, w_s, b_s, w_out = params
    n_heads = b_f.shape[0]
    off_f = 3 * n_heads * HEAD_DIM
    off_ga = off_f + n_heads
    w_in_t = w_in.T
    w_qkv, w_rest = _split_w_in(w_in_t, off_f, off_ga)
    return dict(
        norm_g=norm_g[None], b_f=b_f[:, None], ln_g=ln_g[None], ln_b=ln_b[None],
        w_qkv=w_qkv, w_rest=w_rest,
        w_ft=_pad_rows(w_in_t[off_f:off_ga], 16).astype(BF16),
        w_out=w_out.astype(BF16), w_s=w_s, b_s=b_s)


def _layer(x, w, caches, final_g, *, final_norm, tm):
    batch, seq, d = x.shape
    n = batch * seq
    n_heads = w['b_f'].shape[0]
    d_b = w['ln_g'].shape[1]

    x2 = x.reshape(n, d)
    w_mix, b_mix = _mixing_weights(w['w_s'], w['b_s'], seq)

    blk, kv_sub = ATTN_QUERY_BLOCK, ATTN_KEY_CHUNK
    tm_wide = min(n, WIDE_ROW_TILE)
    h, q, logft = _norm_q(x2, w['norm_g'], w['w_qkv'], w['w_ft'], w['b_f'], tm=tm_wide)
    k, kb, v, vb, sga = _proj_kv(h, w['w_qkv'], w['w_rest'], n_heads=n_heads, tm=tm,
                                 v_sub=kv_sub if caches is None else None)
    gm = _proj_gmlp(h, w['w_rest'], w['ln_g'], w['ln_b'], w_mix, b_mix,
                    tm=tm_wide, emit_vn=caches is not None)

    logf = logft.T.reshape(batch, seq, n_heads)
    if caches is None:
        out_b = gm[0]
        vn = None
        group = seq // LANES
        cum = _cumsum_rows(logft.reshape(n_heads * batch * group, LANES), group)
        cum_t = cum.reshape(n_heads, batch, group, LANES).transpose(0, 1, 3, 2)
        out_a = _attn_prompt(q, kb, vb, cum_t, sga, batch=batch, seq=seq, blk=blk, kv_sub=kv_sub,
                             heads_per_step=ATTN_HEADS_PER_STEP)
    else:
        out_b, vn = gm
        cache_k, cache_v, cache_logf = caches
        past = cache_k.shape[1]
        chunk = CACHE_CHUNK
        total = past + seq
        group = -(-total // LANES)
        lf_all = jnp.concatenate(
            [cache_logf.astype(F32).transpose(0, 2, 1), logf.transpose(0, 2, 1)], axis=2)
        lf_all = jnp.pad(lf_all, ((0, 0), (0, 0), (0, group * LANES - total)))
        cum = _cumsum_rows(lf_all.reshape(batch * n_heads * group, LANES), group)
        cum = cum.reshape(batch, n_heads, group * LANES)
        cum_past = cum[:, :, :past].reshape(batch, n_heads, past // chunk, chunk)
        cum_past = cum_past.transpose(0, 2, 1, 3)
        cum_new = cum[:, :, past:total]
        out_a = _attn_sample(q, kb, vb, cache_k.reshape(batch * past * n_heads, HEAD_DIM),
                             cache_v.reshape(batch * past * n_heads, HEAD_DIM),
                             cum_past, cum_new, sga,
                             batch=batch, t_new=seq, past=past, chunk=chunk)

    y2 = _out_proj(out_a, out_b, w['w_out'], x2, final_g[None], tm=tm, final_norm=final_norm)
    y = y2.reshape(batch, seq, d)
    k4 = k.reshape(batch, seq, n_heads, HEAD_DIM)
    v4 = v.reshape(batch, seq, n_heads, HEAD_DIM)
    vn3 = None if vn is None else vn.reshape(batch, seq, d_b)
    return y, k4, v4, logf, vn3


def kernel(x_prompt, x_sample, cache_k, cache_v, cache_logf, norm_g, w_in, b_f, ln_g, ln_b,
           w_s, b_s, w_out, final_g):
    depth = norm_g.shape[0]
    hp, hs = x_prompt, x_sample
    kp, vp, fp, ksm, vsm, fsm, gsm = [], [], [], [], [], [], []
    n_sample = x_sample.shape[0] * x_sample.shape[1]
    for l in range(depth):
        w = _prepare_weights(
            (norm_g[l], w_in[l], b_f[l], ln_g[l], ln_b[l], w_s[l], b_s[l], w_out[l]))
        last = l == depth - 1
        hp, k1, v1, f1, _ = _layer(hp, w, None, final_g, final_norm=last, tm=ROW_TILE)
        hs, k2, v2, f2, g2 = _layer(hs, w, (cache_k[l], cache_v[l], cache_logf[l]), final_g,
                                    final_norm=last, tm=n_sample)
        kp.append(k1); vp.append(v1); fp.append(f1)
        ksm.append(k2); vsm.append(v2); fsm.append(f2); gsm.append(g2)
    return (hp, hs, jnp.stack(kp), jnp.stack(vp), jnp.stack(fp),
            jnp.stack(ksm), jnp.stack(vsm), jnp.stack(fsm), jnp.stack(gsm))
```

```python
import functools
import math

import jax
import jax.numpy as jnp
from jax import lax
from jax.experimental import pallas as pl
from jax.experimental.pallas import tpu as pltpu

F32 = jnp.float32
BF16 = jnp.bfloat16

HEAD_DIM = 128
GMLP_CHUNK = 128
RMS_EPS = 1e-6
LN_EPS = 1e-5
LOG2E = math.log2(math.e)
LANES = 128
VMEM_LIMIT_BYTES = 48 * 1024 * 1024

ROW_TILE = 512
WIDE_ROW_TILE = 1024
ATTN_QUERY_BLOCK = 1024
ATTN_KEY_CHUNK = 512
ATTN_HEADS_PER_STEP = 1
CACHE_CHUNK = 2048
W_SPLIT_COLS = 256

_NT = (((1,), (1,)), ((), ()))


def _compiler_params(semantics):
    return pltpu.CompilerParams(dimension_semantics=semantics,
                                vmem_limit_bytes=VMEM_LIMIT_BYTES)


def _rmsnorm_rows(x, g):
    return x * lax.rsqrt(jnp.mean(x * x, axis=-1, keepdims=True) + RMS_EPS) * g


def _gelu_tanh(x):
    c = math.sqrt(2.0 / math.pi)
    return 0.5 * x * (1.0 + jnp.tanh(c * (x + 0.044715 * (x * x * x))))


def _silu(x):
    return x * jax.nn.sigmoid(x)


def _store_heads_on_sublanes(ref, z, n_heads):
    rows = z.shape[0]
    for h in range(n_heads):
        ref[pl.ds(h, rows, stride=n_heads), :] = z[:, h * HEAD_DIM:(h + 1) * HEAD_DIM]


def _project(h, w_ref):
    return jnp.dot(h, w_ref[...], preferred_element_type=F32)


def _resident(shape, block_index):
    return pl.BlockSpec(shape, lambda i: block_index, pipeline_mode=pl.Buffered(1))


def _norm_q_kernel(x_ref, g_ref, wq_ref, wft_ref, bf_ref, h_ref, q_ref, logft_ref,
                   *, q_scale, n_heads):
    hb = _rmsnorm_rows(x_ref[...], g_ref[...]).astype(BF16)
    h_ref[...] = hb
    q_ref[...] = (_project(hb, wq_ref) * q_scale).astype(BF16)
    zf = lax.dot_general(wft_ref[...], hb, _NT, preferred_element_type=F32)
    t = zf[:n_heads] + bf_ref[...]
    logft_ref[...] = jnp.minimum(t, 0.0) - jnp.log1p(jnp.exp(-jnp.abs(t)))


def _norm_q(x2, norm_g, w_qkv, w_ft, b_f, *, tm):
    n, d = x2.shape
    n_heads = b_f.shape[0]
    d_a = n_heads * HEAD_DIM
    row = lambda i: (i, 0)
    return pl.pallas_call(
        functools.partial(_norm_q_kernel, q_scale=HEAD_DIM ** -0.5 * LOG2E, n_heads=n_heads),
        grid=(n // tm,),
        in_specs=[
            pl.BlockSpec((tm, d), row),
            _resident((1, d), (0, 0)),
            _resident((d, d_a), (0, 0)),
            _resident(w_ft.shape, (0, 0)),
            _resident((n_heads, 1), (0, 0)),
        ],
        out_specs=[
            pl.BlockSpec((tm, d), row),
            pl.BlockSpec((tm, d_a), row),
            pl.BlockSpec((n_heads, tm), lambda i: (0, i)),
        ],
        out_shape=[
            jax.ShapeDtypeStruct((n, d), BF16),
            jax.ShapeDtypeStruct((n, d_a), BF16),
            jax.ShapeDtypeStruct((n_heads, n), F32),
        ],
        compiler_params=_compiler_params(("parallel",)),
        name="norm_q",
    )(x2, norm_g, w_qkv, w_ft, b_f)


def _proj_kv_kernel(h_ref, wk_ref, wv_ref, wga_ref, k_ref, kb_ref, v_ref, vb_ref, sga_ref,
                    *, n_heads, v_sub):
    h = h_ref[...]
    z = _project(h, wk_ref)
    _store_heads_on_sublanes(k_ref, z, n_heads)
    kb_ref[...] = z.astype(BF16)
    z = _project(h, wv_ref)
    _store_heads_on_sublanes(v_ref, z, n_heads)
    if v_sub is None:
        vb_ref[...] = z.astype(BF16)
    else:
        for c in range(z.shape[0] // v_sub):
            vb_ref[c] = z[c * v_sub:(c + 1) * v_sub, :].T.astype(BF16)
    sga_ref[...] = _silu(_project(h, wga_ref)).astype(BF16)


def _proj_kv(h, w_qkv, w_rest, *, n_heads, tm, v_sub=None):
    n, d = h.shape
    d_a = n_heads * HEAD_DIM
    row = lambda i: (i, 0)
    if v_sub is None:
        vb_spec = pl.BlockSpec((tm, d_a), row)
        vb_shape = jax.ShapeDtypeStruct((n, d_a), BF16)
    else:
        vb_spec = pl.BlockSpec((tm // v_sub, d_a, v_sub), lambda i: (i, 0, 0))
        vb_shape = jax.ShapeDtypeStruct((n // v_sub, d_a, v_sub), BF16)
    return pl.pallas_call(
        functools.partial(_proj_kv_kernel, n_heads=n_heads, v_sub=v_sub),
        grid=(n // tm,),
        in_specs=[
            pl.BlockSpec((tm, d), row),
            _resident((d, d_a), (0, 1)),
            _resident((d, d_a), (0, 2)),
            _resident((d, d_a), (0, 0)),
        ],
        out_specs=[
            pl.BlockSpec((tm * n_heads, HEAD_DIM), row),
            pl.BlockSpec((tm, d_a), row),
            pl.BlockSpec((tm * n_heads, HEAD_DIM), row),
            vb_spec,
            pl.BlockSpec((tm, d_a), row),
        ],
        out_shape=[
            jax.ShapeDtypeStruct((n * n_heads, HEAD_DIM), F32),
            jax.ShapeDtypeStruct((n, d_a), BF16),
            jax.ShapeDtypeStruct((n * n_heads, HEAD_DIM), F32),
            vb_shape,
            jax.ShapeDtypeStruct((n, d_a), BF16),
        ],
        compiler_params=_compiler_params(("parallel",)),
        name="proj_kv",
    )(h, w_qkv, w_qkv, w_rest)


def _proj_gmlp_kernel(h_ref, wu_ref, wv_ref, wg_ref, lng_ref, lnb_ref, wmix_ref, bmix_ref,
                      ob_ref, *rest, n_groups, emit_vn):
    if emit_vn:
        vn_ref, mix_ref, gu_ref = rest
    else:
        mix_ref, gu_ref = rest
    h = h_ref[...]
    tm = h.shape[0]
    n_chunks = tm // GMLP_CHUNK
    d_g = wv_ref.shape[1] // n_groups

    a = _gelu_tanh(_project(h, wv_ref))
    gu_ref[...] = _gelu_tanh(_project(h, wu_ref))
    mu = jnp.mean(a, axis=-1, keepdims=True)
    ac = a - mu
    var = jnp.mean(ac * ac, axis=-1, keepdims=True)
    vn = ac * lax.rsqrt(var + LN_EPS) * lng_ref[...] + lnb_ref[...]
    if emit_vn:
        vn_ref[...] = vn
    vb = vn.astype(BF16)
    for g in range(n_groups):
        cols = slice(g * d_g, (g + 1) * d_g)
        rhs = jnp.concatenate(
            [vb[r * GMLP_CHUNK:(r + 1) * GMLP_CHUNK, cols] for r in range(n_chunks)], axis=1)
        mixed = jnp.dot(wmix_ref[g], rhs, preferred_element_type=F32)
        for r in range(n_chunks):
            mix_ref[r * GMLP_CHUNK:(r + 1) * GMLP_CHUNK, cols] = (
                mixed[:, r * d_g:(r + 1) * d_g] + bmix_ref[g])

    gate = _silu(_project(h, wg_ref))
    ob_ref[...] = (gu_ref[...] * mix_ref[...] * gate).astype(BF16)


def _proj_gmlp(h, w_rest, ln_g, ln_b, w_mix, b_mix, *, tm, emit_vn):
    n, d = h.shape
    d_b = ln_g.shape[1]
    n_groups = w_mix.shape[0]
    row = lambda i: (i, 0)
    out_specs = [pl.BlockSpec((tm, d_b), row)]
    out_shape = [jax.ShapeDtypeStruct((n, d_b), BF16)]
    if emit_vn:
        out_specs.append(pl.BlockSpec((tm, d_b), row))
        out_shape.append(jax.ShapeDtypeStruct((n, d_b), F32))
    return pl.pallas_call(
        functools.partial(_proj_gmlp_kernel, n_groups=n_groups, emit_vn=emit_vn),
        grid=(n // tm,),
        in_specs=[
            pl.BlockSpec((tm, d), row),
            _resident((d, d_b), (0, 1)),
            _resident((d, d_b), (0, 2)),
            _resident((d, d_b), (0, 3)),
            _resident((1, d_b), (0, 0)),
            _resident((1, d_b), (0, 0)),
            _resident(w_mix.shape, (0, 0, 0)),
            _resident(b_mix.shape, (0, 0, 0)),
        ],
        out_specs=out_specs,
        out_shape=out_shape,
        scratch_shapes=[pltpu.VMEM((tm, d_b), F32), pltpu.VMEM((tm, d_b), F32)],
        compiler_params=_compiler_params(("parallel",)),
        name="proj_gmlp",
    )(h, w_rest, w_rest, w_rest, ln_g, ln_b, w_mix, b_mix)


def _cumsum_kernel(x_ref, o_ref, *, group):
    x = x_ref[...]
    lane = lax.broadcasted_iota(jnp.int32, x.shape, 1)
    sh = 1
    while sh < LANES:
        x = x + jnp.where(lane >= sh, pltpu.roll(x, sh, axis=1), 0.0)
        sh *= 2
    sub = lax.broadcasted_iota(jnp.int32, x.shape, 0) % group
    tot = jnp.broadcast_to(x[:, LANES - 1:LANES], x.shape)
    carry = jnp.where(sub >= 1, pltpu.roll(tot, 1, axis=0), 0.0)
    sh = 1
    while sh < group:
        carry = carry + jnp.where(sub >= sh, pltpu.roll(carry, sh, axis=0), 0.0)
        sh *= 2
    o_ref[...] = x + carry


def _cumsum_rows(x2, group):
    return pl.pallas_call(
        functools.partial(_cumsum_kernel, group=group),
        out_shape=jax.ShapeDtypeStruct(x2.shape, F32),
        name="cumsum_rows",
    )(x2)


def _online_softmax_step(s, v_blk, m, l, acc):
    m_new = jnp.maximum(m, jnp.max(s, axis=1, keepdims=True))
    alpha = jnp.exp2(m - m_new)
    p = jnp.exp2(s - m_new)
    l = alpha * l + jnp.sum(p, axis=1, keepdims=True)
    acc = alpha * acc + jnp.dot(p.astype(BF16), v_blk, preferred_element_type=F32)
    return m_new, l, acc


QUERY_GROUP = 256
N_BIAS_PARTS = 3
SUM_ROWS = 16


def _attn_prompt_kernel(q_ref, kb_ref, vt_ref, ct_ref, sga_ref, o_ref,
                        ka_ref, s_ref, p_ref, acc_ref, qa_ref, *, blk):
    kv_sub = s_ref.shape[1]
    seq = q_ref.shape[0]
    n_heads = q_ref.shape[1] // HEAD_DIM
    lane = lax.broadcasted_iota(jnp.int32, (LANES, LANES), 1)

    def head_cols(hh):
        return slice(hh * HEAD_DIM, (hh + 1) * HEAD_DIM)

    def load_keys(c):
        subs = slice(c * kv_sub // LANES, (c + 1) * kv_sub // LANES)
        for hh in range(n_heads):
            rest = ct_ref[hh, :, subs] * (-LOG2E)
            pieces = []
            for _ in range(N_BIAS_PARTS):
                pieces.append(rest.astype(BF16).astype(F32))
                rest = rest - pieces[-1]
            for j, sb in enumerate(range(subs.start, subs.stop)):
                rows = slice(sb * LANES, (sb + 1) * LANES)
                extra = jnp.zeros((LANES, LANES), F32)
                for part, piece in enumerate(pieces):
                    col = jnp.broadcast_to(piece[:, j:j + 1], (LANES, LANES))
                    extra = jnp.where(lane == part, col, extra)
                ka_ref[hh, rows, :] = jnp.concatenate(
                    [kb_ref[rows, head_cols(hh)], extra.astype(BF16)], axis=1)

    ones_feat = jnp.where(lax.broadcasted_iota(jnp.int32, (blk, LANES), 1) < N_BIAS_PARTS,
                          1.0, 0.0).astype(BF16)
    ones_rows = jnp.ones((SUM_ROWS, kv_sub), BF16)

    def load_queries(qb):
        for hh in range(n_heads):
            qa_ref[hh * blk:(hh + 1) * blk, :] = jnp.concatenate(
                [q_ref[qb * blk:(qb + 1) * blk, head_cols(hh)], ones_feat], axis=1)

    groups_per_head = blk // QUERY_GROUP
    n_groups = n_heads * groups_per_head
    group = [slice(t * QUERY_GROUP, (t + 1) * QUERY_GROUP) for t in range(n_groups)]
    head_of = [t // groups_per_head for t in range(n_groups)]
    order = [hh * groups_per_head + g for g in range(groups_per_head) for hh in range(n_heads)]

    def scores(c, g, n_keys):
        keys = slice(c * kv_sub, c * kv_sub + n_keys)
        return lax.dot_general(ka_ref[head_of[g], keys, :], qa_ref[group[g], :],
                               _NT, preferred_element_type=F32)

    own_tile = (QUERY_GROUP, QUERY_GROUP)
    not_after = (lax.broadcasted_iota(jnp.int32, own_tile, 0)
                 <= lax.broadcasted_iota(jnp.int32, own_tile, 1))

    def causal_tail(s):
        n_before = s.shape[0] - QUERY_GROUP
        tail = jnp.where(not_after, s[n_before:], -jnp.inf)
        return tail if n_before == 0 else jnp.concatenate([s[:n_before], tail], axis=0)

    def softmax(s, m):
        m_new = jnp.maximum(m, jnp.max(s, axis=0, keepdims=True))
        return m_new, jnp.exp2(m - m_new), jnp.exp2(s - m_new).astype(BF16)

    def accumulate(c, g, p, alpha):
        n_keys = p.shape[0]
        vt_sum = jnp.concatenate([vt_ref[c, head_cols(head_of[g]), :][:, :n_keys],
                                  ones_rows[:, :n_keys]], axis=0)
        pv = jnp.dot(vt_sum, p, preferred_element_type=F32)
        acc_ref[:, group[g]] = pv if alpha is None else alpha * acc_ref[:, group[g]] + pv

    def finish(qb):
        rows = slice(qb * blk, (qb + 1) * blk)
        for hh in range(n_heads):
            cols = slice(hh * blk, (hh + 1) * blk)
            out = (acc_ref[:HEAD_DIM, cols] / acc_ref[HEAD_DIM:HEAD_DIM + 1, cols]).T
            o_ref[rows, head_cols(hh)] = (
                out * sga_ref[rows, head_cols(hh)].astype(F32)).astype(BF16)

    chunks_per_block = blk // kv_sub
    items = []
    for qb in range(seq // blk):
        for c in range((qb + 1) * chunks_per_block):
            first_key = c * kv_sub - qb * blk
            n_keys, own = [], []
            for g in range(n_groups):
                first_query = (g % groups_per_head) * QUERY_GROUP
                n_keys.append(min(max(first_query + QUERY_GROUP - first_key, 0), kv_sub))
                own.append(first_key <= first_query < first_key + kv_sub)
            items.append((qb, c, n_keys, own))

    m = [None] * n_groups
    alpha = {}
    for t in range(len(items) + 2):
        qk_item = items[t] if t < len(items) else None
        sm_item = items[t - 1] if 1 <= t <= len(items) else None
        pv_item = items[t - 2] if t >= 2 else None
        if qk_item is not None:
            if qk_item[1] // chunks_per_block == qk_item[0]:
                load_keys(qk_item[1])
            if qk_item[1] == 0:
                load_queries(qk_item[0])
        for g in order:
            if qk_item is not None and qk_item[2][g]:
                qb, c, n_keys, _ = qk_item
                s_ref[t % 2, :n_keys[g], group[g]] = scores(c, g, n_keys[g])
            if pv_item is not None and pv_item[2][g]:
                qb, c, n_keys, _ = pv_item
                accumulate(c, g, p_ref[t % 2, :n_keys[g], group[g]],
                           None if c == 0 else alpha[t - 2, g])
            if sm_item is not None and sm_item[2][g]:
                qb, c, n_keys, own = sm_item
                s = s_ref[(t - 1) % 2, :n_keys[g], group[g]]
                if own[g]:
                    s = causal_tail(s)
                m_old = jnp.full((1, QUERY_GROUP), -jnp.inf, F32) if c == 0 else m[g]
                m[g], alpha[t - 1, g], p_ref[(t - 1) % 2, :n_keys[g], group[g]] = softmax(s, m_old)
        if pv_item is not None and pv_item[1] == (pv_item[0] + 1) * chunks_per_block - 1:
            finish(pv_item[0])


def _attn_prompt(q, kb, vt, cum_t, sga, *, batch, seq, blk, kv_sub, heads_per_step):
    n, d_a = q.shape
    n_heads = d_a // HEAD_DIM
    assert blk % kv_sub == 0 and seq % blk == 0 and n_heads % heads_per_step == 0
    n_chunks = seq // kv_sub
    width = heads_per_step * HEAD_DIM
    wide = heads_per_step * blk
    per_seq = lambda b, h: (b, h)
    return pl.pallas_call(
        functools.partial(_attn_prompt_kernel, blk=blk),
        grid=(batch, n_heads // heads_per_step),
        in_specs=[
            pl.BlockSpec((seq, width), per_seq),
            pl.BlockSpec((seq, width), per_seq),
            pl.BlockSpec((n_chunks, width, kv_sub), lambda b, h: (b, h, 0)),
            pl.BlockSpec((heads_per_step, None, LANES, seq // LANES), lambda b, h: (h, b, 0, 0)),
            pl.BlockSpec((seq, width), per_seq),
        ],
        out_specs=pl.BlockSpec((seq, width), per_seq),
        out_shape=jax.ShapeDtypeStruct((n, d_a), BF16),
        scratch_shapes=[pltpu.VMEM((heads_per_step, seq, HEAD_DIM + LANES), BF16),
                        pltpu.VMEM((2, kv_sub, wide), F32),
                        pltpu.VMEM((2, kv_sub, wide), BF16),
                        pltpu.VMEM((HEAD_DIM + SUM_ROWS, wide), F32),
                        pltpu.VMEM((wide, HEAD_DIM + LANES), BF16)],
        compiler_params=_compiler_params(("parallel", "parallel")),
        name="attn_prompt",
    )(q, kb, vt, cum_t, sga)


def _attn_sample_kernel(q_ref, kn_ref, vn_ref, ck_ref, cv_ref, cp_ref, cn_ref, sga_ref, o_ref,
                        m_ref, l_ref, acc_ref, *, n_heads):
    c = pl.program_id(1)

    @pl.when(c == 0)
    def _():
        m_ref[...] = jnp.full(m_ref.shape, -jnp.inf, F32)
        l_ref[...] = jnp.zeros(l_ref.shape, F32)
        acc_ref[...] = jnp.zeros(acc_ref.shape, F32)

    def head_cols(h):
        return slice(h * HEAD_DIM, (h + 1) * HEAD_DIM)

    chunk = ck_ref.shape[0] // n_heads

    def cached_head(ref, h):
        return ref[pl.ds(h, chunk, stride=n_heads), :].astype(BF16)

    heads = range(n_heads)
    s = [lax.dot_general(q_ref[:, head_cols(h)], cached_head(ck_ref, h), _NT,
                         preferred_element_type=F32) + cp_ref[h:h + 1, :] * (-LOG2E)
         for h in heads]
    m_new = [jnp.maximum(m_ref[h], jnp.max(s[h], axis=1, keepdims=True)) for h in heads]
    p = [jnp.exp2(s[h] - m_new[h]) for h in heads]
    pv = [jnp.dot(p[h].astype(BF16), cached_head(cv_ref, h), preferred_element_type=F32)
          for h in heads]
    for h in heads:
        alpha = jnp.exp2(m_ref[h] - m_new[h])
        l_ref[h] = alpha * l_ref[h] + jnp.sum(p[h], axis=1, keepdims=True)
        acc_ref[h] = alpha * acc_ref[h] + pv[h]
        m_ref[h] = m_new[h]

    @pl.when(c == pl.num_programs(1) - 1)
    def _():
        for h in range(n_heads):
            q = q_ref[:, head_cols(h)]
            s = lax.dot_general(q, kn_ref[:, head_cols(h)], _NT, preferred_element_type=F32)
            s = s + cn_ref[h:h + 1, :] * (-LOG2E)
            rows = lax.broadcasted_iota(jnp.int32, s.shape, 0)
            cols = lax.broadcasted_iota(jnp.int32, s.shape, 1)
            s = jnp.where(cols <= rows, s, -jnp.inf)
            _, l, acc = _online_softmax_step(s, vn_ref[:, head_cols(h)],
                                             m_ref[h], l_ref[h], acc_ref[h])
            o_ref[:, head_cols(h)] = (
                acc / l * sga_ref[:, head_cols(h)].astype(F32)).astype(BF16)


def _attn_sample(q, k_new, v_new, cache_k2, cache_v2, cum_past, cum_new, sga, *,
                 batch, t_new, past, chunk):
    n, d_a = q.shape
    n_heads = d_a // HEAD_DIM
    n_chunks = past // chunk
    new_map = lambda b, c: (b, 0)
    cache_map = lambda b, c: (b * n_chunks + c, 0)
    return pl.pallas_call(
        functools.partial(_attn_sample_kernel, n_heads=n_heads),
        grid=(batch, n_chunks),
        in_specs=[
            pl.BlockSpec((t_new, d_a), new_map),
            pl.BlockSpec((t_new, d_a), new_map),
            pl.BlockSpec((t_new, d_a), new_map),
            pl.BlockSpec((chunk * n_heads, HEAD_DIM), cache_map),
            pl.BlockSpec((chunk * n_heads, HEAD_DIM), cache_map),
            pl.BlockSpec((None, None, n_heads, chunk), lambda b, c: (b, c, 0, 0)),
            pl.BlockSpec((None, n_heads, t_new), lambda b, c: (b, 0, 0)),
            pl.BlockSpec((t_new, d_a), new_map),
        ],
        out_specs=pl.BlockSpec((t_new, d_a), new_map),
        out_shape=jax.ShapeDtypeStruct((n, d_a), BF16),
        scratch_shapes=[pltpu.VMEM((n_heads, t_new, 1), F32),
                        pltpu.VMEM((n_heads, t_new, 1), F32),
                        pltpu.VMEM((n_heads, t_new, HEAD_DIM), F32)],
        compiler_params=_compiler_params(("parallel", "arbitrary")),
        name="attn_sample",
    )(q, k_new, v_new, cache_k2, cache_v2, cum_past, cum_new, sga)


def _out_proj_kernel(oa_ref, ob_ref, w_ref, x_ref, fg_ref, y_ref, *, final_norm):
    o = jnp.concatenate([oa_ref[...], ob_ref[...]], axis=1)
    y = x_ref[...] + jnp.dot(o, w_ref[...], preferred_element_type=F32)
    if final_norm:
        y = _rmsnorm_rows(y, fg_ref[...])
    y_ref[...] = y


def _out_proj(out_a, out_b, w_out, x2, final_g, *, tm, final_norm):
    n, d = x2.shape
    d_a = out_a.shape[1]
    d_b = out_b.shape[1]
    row = lambda i: (i, 0)
    const = lambda i: (0, 0)
    return pl.pallas_call(
        functools.partial(_out_proj_kernel, final_norm=final_norm),
        grid=(n // tm,),
        in_specs=[
            pl.BlockSpec((tm, d_a), row),
            pl.BlockSpec((tm, d_b), row),
            pl.BlockSpec(w_out.shape, const),
            pl.BlockSpec((tm, d), row),
            pl.BlockSpec((1, d), const),
        ],
        out_specs=pl.BlockSpec((tm, d), row),
        out_shape=jax.ShapeDtypeStruct((n, d), F32),
        compiler_params=_compiler_params(("parallel",)),
        name="out_proj",
    )(out_a, out_b, w_out, x2, final_g)


def _pad_rows(a, rows):
    return jnp.pad(a, ((0, rows - a.shape[0]), (0, 0)))


def _mixing_weights(w_s, b_s, seq):
    c = min(seq, GMLP_CHUNK)
    reps = GMLP_CHUNK // c
    w = w_s[:, :c, :c] * jnp.tril(jnp.ones((c, c), w_s.dtype))
    if reps > 1:
        w = jnp.einsum('rs,gab->grasb', jnp.eye(reps, dtype=w.dtype), w)
        w = w.reshape(w_s.shape[0], GMLP_CHUNK, GMLP_CHUNK)
    b = jnp.tile(b_s[:, :c], (1, reps))
    b_full = jnp.broadcast_to(b[:, :, None], (b.shape[0], GMLP_CHUNK, LANES))
    return w.astype(BF16), b_full.astype(F32)


def _split_w_in_kernel(wt_ref, qkv_ref, rest_ref, *, off_rest):
    wt = wt_ref[...]
    qkv_ref[...] = wt[:qkv_ref.shape[1], :].T.astype(BF16)
    rest_ref[...] = wt[off_rest:, :].T.astype(BF16)


def _split_w_in(w_in_t, off_f, off_rest, *, tc=W_SPLIT_COLS):
    d_in, d = w_in_t.shape
    return pl.pallas_call(
        functools.partial(_split_w_in_kernel, off_rest=off_rest),
        grid=(d // tc,),
        in_specs=[pl.BlockSpec((d_in, tc), lambda i: (0, i))],
        out_specs=[pl.BlockSpec((tc, off_f), lambda i: (i, 0)),
                   pl.BlockSpec((tc, d_in - off_rest), lambda i: (i, 0))],
        out_shape=[jax.ShapeDtypeStruct((d, off_f), BF16),
                   jax.ShapeDtypeStruct((d, d_in - off_rest), BF16)],
        compiler_params=_compiler_params(("parallel",)),
        name="split_w_in",
    )(w_in_t)


def _prepare_weights(params):
    norm_g, w_in, b_f, ln_g, ln_b, w_s, b_s, w_out = params
    n_heads = b_f.shape[0]
    off_f = 3 * n_heads * HEAD_DIM
    off_ga = off_f + n_heads
    w_in_t = w_in.T
    w_qkv, w_rest = _split_w_in(w_in_t, off_f, off_ga)
    return dict(
        norm_g=norm_g[None], b_f=b_f[:, None], ln_g=ln_g[None], ln_b=ln_b[None],
        w_qkv=w_qkv, w_rest=w_rest,
        w_ft=_pad_rows(w_in_t[off_f:off_ga], 16).astype(BF16),
        w_out=w_out.astype(BF16), w_s=w_s, b_s=b_s)


def _layer(x, w, caches, final_g, *, final_norm, tm):
    batch, seq, d = x.shape
    n = batch * seq
    n_heads = w['b_f'].shape[0]
    d_b = w['ln_g'].shape[1]

    x2 = x.reshape(n, d)
    w_mix, b_mix = _mixing_weights(w['w_s'], w['b_s'], seq)

    blk, kv_sub = ATTN_QUERY_BLOCK, ATTN_KEY_CHUNK
    tm_wide = min(n, WIDE_ROW_TILE)
    h, q, logft = _norm_q(x2, w['norm_g'], w['w_qkv'], w['w_ft'], w['b_f'], tm=tm_wide)
    k, kb, v, vb, sga = _proj_kv(h, w['w_qkv'], w['w_rest'], n_heads=n_heads, tm=tm,
                                 v_sub=kv_sub if caches is None else None)
    gm = _proj_gmlp(h, w['w_rest'], w['ln_g'], w['ln_b'], w_mix, b_mix,
                    tm=tm_wide, emit_vn=caches is not None)

    logf = logft.T.reshape(batch, seq, n_heads)
    if caches is None:
        out_b = gm[0]
        vn = None
        group = seq // LANES
        cum = _cumsum_rows(logft.reshape(n_heads * batch * group, LANES), group)
        cum_t = cum.reshape(n_heads, batch, group, LANES).transpose(0, 1, 3, 2)
        out_a = _attn_prompt(q, kb, vb, cum_t, sga, batch=batch, seq=seq, blk=blk, kv_sub=kv_sub,
                             heads_per_step=ATTN_HEADS_PER_STEP)
    else:
        out_b, vn = gm
        cache_k, cache_v, cache_logf = caches
        past = cache_k.shape[1]
        chunk = CACHE_CHUNK
        total = past + seq
        group = -(-total // LANES)
        lf_all = jnp.concatenate(
            [cache_logf.astype(F32).transpose(0, 2, 1), logf.transpose(0, 2, 1)], axis=2)
        lf_all = jnp.pad(lf_all, ((0, 0), (0, 0), (0, group * LANES - total)))
        cum = _cumsum_rows(lf_all.reshape(batch * n_heads * group, LANES), group)
        cum = cum.reshape(batch, n_heads, group * LANES)
        cum_past = cum[:, :, :past].reshape(batch, n_heads, past // chunk, chunk)
        cum_past = cum_past.transpose(0, 2, 1, 3)
        cum_new = cum[:, :, past:total]
        out_a = _attn_sample(q, kb, vb, cache_k.reshape(batch * past * n_heads, HEAD_DIM),
                             cache_v.reshape(batch * past * n_heads, HEAD_DIM),
                             cum_past, cum_new, sga,
                             batch=batch, t_new=seq, past=past, chunk=chunk)

    y2 = _out_proj(out_a, out_b, w['w_out'], x2, final_g[None], tm=tm, final_norm=final_norm)
    y = y2.reshape(batch, seq, d)
    k4 = k.reshape(batch, seq, n_heads, HEAD_DIM)
    v4 = v.reshape(batch, seq, n_heads, HEAD_DIM)
    vn3 = None if vn is None else vn.reshape(batch, seq, d_b)
    return y, k4, v4, logf, vn3


def kernel(x_prompt, x_sample, cache_k, cache_v, cache_logf, norm_g, w_in, b_f, ln_g, ln_b,
           w_s, b_s, w_out, final_g):
    depth = norm_g.shape[0]
    hp, hs = x_prompt, x_sample
    kp, vp, fp, ksm, vsm, fsm, gsm = [], [], [], [], [], [], []
    n_sample = x_sample.shape[0] * x_sample.shape[1]
    for l in range(depth):
        w = _prepare_weights(
            (norm_g[l], w_in[l], b_f[l], ln_g[l], ln_b[l], w_s[l], b_s[l], w_out[l]))
        last = l == depth - 1
        hp, k1, v1, f1, _ = _layer(hp, w, None, final_g, final_norm=last, tm=ROW_TILE)
        hs, k2, v2, f2, g2 = _layer(hs, w, (cache_k[l], cache_v[l], cache_logf[l]), final_g,
                                    final_norm=last, tm=n_sample)
        kp.append(k1); vp.append(v1); fp.append(f1)
        ksm.append(k2); vsm.append(v2); fsm.append(f2); gsm.append(g2)
    return (hp, hs, jnp.stack(kp), jnp.stack(vp), jnp.stack(fp),
            jnp.stack(ksm), jnp.stack(vsm), jnp.stack(fsm), jnp.stack(gsm))
```

```python
import functools
import math

import jax
import jax.numpy as jnp
from jax import lax
from jax.experimental import pallas as pl
from jax.experimental.pallas import tpu as pltpu

F32 = jnp.float32
BF16 = jnp.bfloat16

HEAD_DIM = 128
GMLP_CHUNK = 128
RMS_EPS = 1e-6
LN_EPS = 1e-5
LOG2E = math.log2(math.e)
LANES = 128
VMEM_LIMIT_BYTES = 48 * 1024 * 1024

ROW_TILE = 512
WIDE_ROW_TILE = 1024
ATTN_QUERY_BLOCK = 1024
ATTN_KEY_CHUNK = 512
ATTN_HEADS_PER_STEP = 1
CACHE_CHUNK = 2048
W_SPLIT_COLS = 256

_NT = (((1,), (1,)), ((), ()))


def _compiler_params(semantics):
    return pltpu.CompilerParams(dimension_semantics=semantics,
                                vmem_limit_bytes=VMEM_LIMIT_BYTES)


def _rmsnorm_rows(x, g):
    return x * lax.rsqrt(jnp.mean(x * x, axis=-1, keepdims=True) + RMS_EPS) * g


def _gelu_tanh(x):
    c = math.sqrt(2.0 / math.pi)
    return 0.5 * x * (1.0 + jnp.tanh(c * (x + 0.044715 * (x * x * x))))


def _silu(x):
    return x * jax.nn.sigmoid(x)


def _store_heads_on_sublanes(ref, z, n_heads):
    rows = z.shape[0]
    for h in range(n_heads):
        ref[pl.ds(h, rows, stride=n_heads), :] = z[:, h * HEAD_DIM:(h + 1) * HEAD_DIM]


def _project(h, w_ref):
    return jnp.dot(h, w_ref[...], preferred_element_type=F32)


def _resident(shape, block_index):
    return pl.BlockSpec(shape, lambda i: block_index, pipeline_mode=pl.Buffered(1))


def _norm_q_kernel(x_ref, g_ref, wq_ref, wft_ref, bf_ref, h_ref, q_ref, logft_ref,
                   *, q_scale, n_heads):
    hb = _rmsnorm_rows(x_ref[...], g_ref[...]).astype(BF16)
    h_ref[...] = hb
    q_ref[...] = (_project(hb, wq_ref) * q_scale).astype(BF16)
    zf = lax.dot_general(wft_ref[...], hb, _NT, preferred_element_type=F32)
    t = zf[:n_heads] + bf_ref[...]
    logft_ref[...] = jnp.minimum(t, 0.0) - jnp.log1p(jnp.exp(-jnp.abs(t)))


def _norm_q(x2, norm_g, w_qkv, w_ft, b_f, *, tm):
    n, d = x2.shape
    n_heads = b_f.shape[0]
    d_a = n_heads * HEAD_DIM
    row = lambda i: (i, 0)
    return pl.pallas_call(
        functools.partial(_norm_q_kernel, q_scale=HEAD_DIM ** -0.5 * LOG2E, n_heads=n_heads),
        grid=(n // tm,),
        in_specs=[
            pl.BlockSpec((tm, d), row),
            _resident((1, d), (0, 0)),
            _resident((d, d_a), (0, 0)),
            _resident(w_ft.shape, (0, 0)),
            _resident((n_heads, 1), (0, 0)),
        ],
        out_specs=[
            pl.BlockSpec((tm, d), row),
            pl.BlockSpec((tm, d_a), row),
            pl.BlockSpec((n_heads, tm), lambda i: (0, i)),
        ],
        out_shape=[
            jax.ShapeDtypeStruct((n, d), BF16),
            jax.ShapeDtypeStruct((n, d_a), BF16),
            jax.ShapeDtypeStruct((n_heads, n), F32),
        ],
        compiler_params=_compiler_params(("parallel",)),
        name="norm_q",
    )(x2, norm_g, w_qkv, w_ft, b_f)


def _proj_kv_kernel(h_ref, wk_ref, wv_ref, wga_ref, k_ref, kb_ref, v_ref, vb_ref, sga_ref,
                    *, n_heads, v_sub):
    h = h_ref[...]
    z = _project(h, wk_ref)
    _store_heads_on_sublanes(k_ref, z, n_heads)
    kb_ref[...] = z.astype(BF16)
    z = _project(h, wv_ref)
    _store_heads_on_sublanes(v_ref, z, n_heads)
    if v_sub is None:
        vb_ref[...] = z.astype(BF16)
    else:
        for c in range(z.shape[0] // v_sub):
            vb_ref[c] = z[c * v_sub:(c + 1) * v_sub, :].T.astype(BF16)
    sga_ref[...] = _silu(_project(h, wga_ref)).astype(BF16)


def _proj_kv(h, w_qkv, w_rest, *, n_heads, tm, v_sub=None):
    n, d = h.shape
    d_a = n_heads * HEAD_DIM
    row = lambda i: (i, 0)
    if v_sub is None:
        vb_spec = pl.BlockSpec((tm, d_a), row)
        vb_shape = jax.ShapeDtypeStruct((n, d_a), BF16)
    else:
        vb_spec = pl.BlockSpec((tm // v_sub, d_a, v_sub), lambda i: (i, 0, 0))
        vb_shape = jax.ShapeDtypeStruct((n // v_sub, d_a, v_sub), BF16)
    return pl.pallas_call(
        functools.partial(_proj_kv_kernel, n_heads=n_heads, v_sub=v_sub),
        grid=(n // tm,),
        in_specs=[
            pl.BlockSpec((tm, d), row),
            _resident((d, d_a), (0, 1)),
            _resident((d, d_a), (0, 2)),
            _resident((d, d_a), (0, 0)),
        ],
        out_specs=[
            pl.BlockSpec((tm * n_heads, HEAD_DIM), row),
            pl.BlockSpec((tm, d_a), row),
            pl.BlockSpec((tm * n_heads, HEAD_DIM), row),
            vb_spec,
            pl.BlockSpec((tm, d_a), row),
        ],
        out_shape=[
            jax.ShapeDtypeStruct((n * n_heads, HEAD_DIM), F32),
            jax.ShapeDtypeStruct((n, d_a), BF16),
            jax.ShapeDtypeStruct((n * n_heads, HEAD_DIM), F32),
            vb_shape,
            jax.ShapeDtypeStruct((n, d_a), BF16),
        ],
        compiler_params=_compiler_params(("parallel",)),
        name="proj_kv",
    )(h, w_qkv, w_qkv, w_rest)


def _proj_gmlp_kernel(h_ref, wu_ref, wv_ref, wg_ref, lng_ref, lnb_ref, wmix_ref, bmix_ref,
                      ob_ref, *rest, n_groups, emit_vn):
    if emit_vn:
        vn_ref, mix_ref, gu_ref = rest
    else:
        mix_ref, gu_ref = rest
    h = h_ref[...]
    tm = h.shape[0]
    n_chunks = tm // GMLP_CHUNK
    d_g = wv_ref.shape[1] // n_groups

    a = _gelu_tanh(_project(h, wv_ref))
    gu_ref[...] = _gelu_tanh(_project(h, wu_ref))
    mu = jnp.mean(a, axis=-1, keepdims=True)
    ac = a - mu
    var = jnp.mean(ac * ac, axis=-1, keepdims=True)
    vn = ac * lax.rsqrt(var + LN_EPS) * lng_ref[...] + lnb_ref[...]
    if emit_vn:
        vn_ref[...] = vn
    vb = vn.astype(BF16)
    for g in range(n_groups):
        cols = slice(g * d_g, (g + 1) * d_g)
        rhs = jnp.concatenate(
            [vb[r * GMLP_CHUNK:(r + 1) * GMLP_CHUNK, cols] for r in range(n_chunks)], axis=1)
        mixed = jnp.dot(wmix_ref[g], rhs, preferred_element_type=F32)
        for r in range(n_chunks):
            mix_ref[r * GMLP_CHUNK:(r + 1) * GMLP_CHUNK, cols] = (
                mixed[:, r * d_g:(r + 1) * d_g] + bmix_ref[g])

    gate = _silu(_project(h, wg_ref))
    ob_ref[...] = (gu_ref[...] * mix_ref[...] * gate).astype(BF16)


def _proj_gmlp(h, w_rest, ln_g, ln_b, w_mix, b_mix, *, tm, emit_vn):
    n, d = h.shape
    d_b = ln_g.shape[1]
    n_groups = w_mix.shape[0]
    row = lambda i: (i, 0)
    out_specs = [pl.BlockSpec((tm, d_b), row)]
    out_shape = [jax.ShapeDtypeStruct((n, d_b), BF16)]
    if emit_vn:
        out_specs.append(pl.BlockSpec((tm, d_b), row))
        out_shape.append(jax.ShapeDtypeStruct((n, d_b), F32))
    return pl.pallas_call(
        functools.partial(_proj_gmlp_kernel, n_groups=n_groups, emit_vn=emit_vn),
        grid=(n // tm,),
        in_specs=[
            pl.BlockSpec((tm, d), row),
            _resident((d, d_b), (0, 1)),
            _resident((d, d_b), (0, 2)),
            _resident((d, d_b), (0, 3)),
            _resident((1, d_b), (0, 0)),
            _resident((1, d_b), (0, 0)),
            _resident(w_mix.shape, (0, 0, 0)),
            _resident(b_mix.shape, (0, 0, 0)),
        ],
        out_specs=out_specs,
        out_shape=out_shape,
        scratch_shapes=[pltpu.VMEM((tm, d_b), F32), pltpu.VMEM((tm, d_b), F32)],
        compiler_params=_compiler_params(("parallel",)),
        name="proj_gmlp",
    )(h, w_rest, w_rest, w_rest, ln_g, ln_b, w_mix, b_mix)


def _cumsum_kernel(x_ref, o_ref, *, group):
    x = x_ref[...]
    lane = lax.broadcasted_iota(jnp.int32, x.shape, 1)
    sh = 1
    while sh < LANES:
        x = x + jnp.where(lane >= sh, pltpu.roll(x, sh, axis=1), 0.0)
        sh *= 2
    sub = lax.broadcasted_iota(jnp.int32, x.shape, 0) % group
    tot = jnp.broadcast_to(x[:, LANES - 1:LANES], x.shape)
    carry = jnp.where(sub >= 1, pltpu.roll(tot, 1, axis=0), 0.0)
    sh = 1
    while sh < group:
        carry = carry + jnp.where(sub >= sh, pltpu.roll(carry, sh, axis=0), 0.0)
        sh *= 2
    o_ref[...] = x + carry


def _cumsum_rows(x2, group):
    return pl.pallas_call(
        functools.partial(_cumsum_kernel, group=group),
        out_shape=jax.ShapeDtypeStruct(x2.shape, F32),
        name="cumsum_rows",
    )(x2)


def _online_softmax_step(s, v_blk, m, l, acc):
    m_new = jnp.maximum(m, jnp.max(s, axis=1, keepdims=True))
    alpha = jnp.exp2(m - m_new)
    p = jnp.exp2(s - m_new)
    l = alpha * l + jnp.sum(p, axis=1, keepdims=True)
    acc = alpha * acc + jnp.dot(p.astype(BF16), v_blk, preferred_element_type=F32)
    return m_new, l, acc


QUERY_GROUP = 256
N_BIAS_PARTS = 3
SUM_ROWS = 16


def _attn_prompt_kernel(q_ref, kb_ref, vt_ref, ct_ref, sga_ref, o_ref,
                        ka_ref, s_ref, p_ref, acc_ref, qa_ref, *, blk):
    kv_sub = s_ref.shape[1]
    seq = q_ref.shape[0]
    n_heads = q_ref.shape[1] // HEAD_DIM
    lane = lax.broadcasted_iota(jnp.int32, (LANES, LANES), 1)

    def head_cols(hh):
        return slice(hh * HEAD_DIM, (hh + 1) * HEAD_DIM)

    def load_keys(c):
        subs = slice(c * kv_sub // LANES, (c + 1) * kv_sub // LANES)
        for hh in range(n_heads):
            rest = ct_ref[hh, :, subs] * (-LOG2E)
            pieces = []
            for _ in range(N_BIAS_PARTS):
                pieces.append(rest.astype(BF16).astype(F32))
                rest = rest - pieces[-1]
            for j, sb in enumerate(range(subs.start, subs.stop)):
                rows = slice(sb * LANES, (sb + 1) * LANES)
                extra = jnp.zeros((LANES, LANES), F32)
                for part, piece in enumerate(pieces):
                    col = jnp.broadcast_to(piece[:, j:j + 1], (LANES, LANES))
                    extra = jnp.where(lane == part, col, extra)
                ka_ref[hh, rows, :] = jnp.concatenate(
                    [kb_ref[rows, head_cols(hh)], extra.astype(BF16)], axis=1)

    ones_feat = jnp.where(lax.broadcasted_iota(jnp.int32, (blk, LANES), 1) < N_BIAS_PARTS,
                          1.0, 0.0).astype(BF16)
    ones_rows = jnp.ones((SUM_ROWS, kv_sub), BF16)

    def load_queries(qb):
        for hh in range(n_heads):
            qa_ref[hh * blk:(hh + 1) * blk, :] = jnp.concatenate(
                [q_ref[qb * blk:(qb + 1) * blk, head_cols(hh)], ones_feat], axis=1)

    groups_per_head = blk // QUERY_GROUP
    n_groups = n_heads * groups_per_head
    group = [slice(t * QUERY_GROUP, (t + 1) * QUERY_GROUP) for t in range(n_groups)]
    head_of = [t // groups_per_head for t in range(n_groups)]
    order = [hh * groups_per_head + g for g in range(groups_per_head) for hh in range(n_heads)]

    def scores(c, g, n_keys):
        keys = slice(c * kv_sub, c * kv_sub + n_keys)
        return lax.dot_general(ka_ref[head_of[g], keys, :], qa_ref[group[g], :],
                               _NT, preferred_element_type=F32)

    own_tile = (QUERY_GROUP, QUERY_GROUP)
    not_after = (lax.broadcasted_iota(jnp.int32, own_tile, 0)
                 <= lax.broadcasted_iota(jnp.int32, own_tile, 1))

    def causal_tail(s):
        n_before = s.shape[0] - QUERY_GROUP
        tail = jnp.where(not_after, s[n_before:], -jnp.inf)
        return tail if n_before == 0 else jnp.concatenate([s[:n_before], tail], axis=0)

    def softmax(s, m):
        m_new = jnp.maximum(m, jnp.max(s, axis=0, keepdims=True))
        return m_new, jnp.exp2(m - m_new), jnp.exp2(s - m_new).astype(BF16)

    def accumulate(c, g, p, alpha):
        n_keys = p.shape[0]
        vt_sum = jnp.concatenate([vt_ref[c, head_cols(head_of[g]), :][:, :n_keys],
                                  ones_rows[:, :n_keys]], axis=0)
        pv = jnp.dot(vt_sum, p, preferred_element_type=F32)
        acc_ref[:, group[g]] = pv if alpha is None else alpha * acc_ref[:, group[g]] + pv

    def finish(qb):
        rows = slice(qb * blk, (qb + 1) * blk)
        for hh in range(n_heads):
            cols = slice(hh * blk, (hh + 1) * blk)
            out = (acc_ref[:HEAD_DIM, cols] / acc_ref[HEAD_DIM:HEAD_DIM + 1, cols]).T
            o_ref[rows, head_cols(hh)] = (
                out * sga_ref[rows, head_cols(hh)].astype(F32)).astype(BF16)

    chunks_per_block = blk // kv_sub
    items = []
    for qb in range(seq // blk):
        for c in range((qb + 1) * chunks_per_block):
            first_key = c * kv_sub - qb * blk
            n_keys, own = [], []
            for g in range(n_groups):
                first_query = (g % groups_per_head) * QUERY_GROUP
                n_keys.append(min(max(first_query + QUERY_GROUP - first_key, 0), kv_sub))
                own.append(first_key <= first_query < first_key + kv_sub)
            items.append((qb, c, n_keys, own))

    m = [None] * n_groups
    alpha = {}
    for t in range(len(items) + 2):
        qk_item = items[t] if t < len(items) else None
        sm_item = items[t - 1] if 1 <= t <= len(items) else None
        pv_item = items[t - 2] if t >= 2 else None
        if qk_item is not None:
            if qk_item[1] // chunks_per_block == qk_item[0]:
                load_keys(qk_item[1])
            if qk_item[1] == 0:
                load_queries(qk_item[0])
        for g in order:
            if qk_item is not None and qk_item[2][g]:
                qb, c, n_keys, _ = qk_item
                s_ref[t % 2, :n_keys[g], group[g]] = scores(c, g, n_keys[g])
            if pv_item is not None and pv_item[2][g]:
                qb, c, n_keys, _ = pv_item
                accumulate(c, g, p_ref[t % 2, :n_keys[g], group[g]],
                           None if c == 0 else alpha[t - 2, g])
            if sm_item is not None and sm_item[2][g]:
                qb, c, n_keys, own = sm_item
                s = s_ref[(t - 1) % 2, :n_keys[g], group[g]]
                if own[g]:
                    s = causal_tail(s)
                m_old = jnp.full((1, QUERY_GROUP), -jnp.inf, F32) if c == 0 else m[g]
                m[g], alpha[t - 1, g], p_ref[(t - 1) % 2, :n_keys[g], group[g]] = softmax(s, m_old)
        if pv_item is not None and pv_item[1] == (pv_item[0] + 1) * chunks_per_block - 1:
            finish(pv_item[0])


def _attn_prompt(q, kb, vt, cum_t, sga, *, batch, seq, blk, kv_sub, heads_per_step):
    n, d_a = q.shape
    n_heads = d_a // HEAD_DIM
    assert blk % kv_sub == 0 and seq % blk == 0 and n_heads % heads_per_step == 0
    n_chunks = seq // kv_sub
    width = heads_per_step * HEAD_DIM
    wide = heads_per_step * blk
    per_seq = lambda b, h: (b, h)
    return pl.pallas_call(
        functools.partial(_attn_prompt_kernel, blk=blk),
        grid=(batch, n_heads // heads_per_step),
        in_specs=[
            pl.BlockSpec((seq, width), per_seq),
            pl.BlockSpec((seq, width), per_seq),
            pl.BlockSpec((n_chunks, width, kv_sub), lambda b, h: (b, h, 0)),
            pl.BlockSpec((heads_per_step, None, LANES, seq // LANES), lambda b, h: (h, b, 0, 0)),
            pl.BlockSpec((seq, width), per_seq),
        ],
        out_specs=pl.BlockSpec((seq, width), per_seq),
        out_shape=jax.ShapeDtypeStruct((n, d_a), BF16),
        scratch_shapes=[pltpu.VMEM((heads_per_step, seq, HEAD_DIM + LANES), BF16),
                        pltpu.VMEM((2, kv_sub, wide), F32),
                        pltpu.VMEM((2, kv_sub, wide), BF16),
                        pltpu.VMEM((HEAD_DIM + SUM_ROWS, wide), F32),
                        pltpu.VMEM((wide, HEAD_DIM + LANES), BF16)],
        compiler_params=_compiler_params(("parallel", "parallel")),
        name="attn_prompt",
    )(q, kb, vt, cum_t, sga)


def _attn_sample_kernel(q_ref, kn_ref, vn_ref, ck_ref, cv_ref, cp_ref, cn_ref, sga_ref, o_ref,
                        m_ref, l_ref, acc_ref, *, n_heads):
    c = pl.program_id(1)

    @pl.when(c == 0)
    def _():
        m_ref[...] = jnp.full(m_ref.shape, -jnp.inf, F32)
        l_ref[...] = jnp.zeros(l_ref.shape, F32)
        acc_ref[...] = jnp.zeros(acc_ref.shape, F32)

    def head_cols(h):
        return slice(h * HEAD_DIM, (h + 1) * HEAD_DIM)

    chunk = ck_ref.shape[0] // n_heads

    def cached_head(ref, h):
        return ref[pl.ds(h, chunk, stride=n_heads), :].astype(BF16)

    heads = range(n_heads)
    s = [lax.dot_general(q_ref[:, head_cols(h)], cached_head(ck_ref, h), _NT,
                         preferred_element_type=F32) + cp_ref[h:h + 1, :] * (-LOG2E)
         for h in heads]
    m_new = [jnp.maximum(m_ref[h], jnp.max(s[h], axis=1, keepdims=True)) for h in heads]
    p = [jnp.exp2(s[h] - m_new[h]) for h in heads]
    pv = [jnp.dot(p[h].astype(BF16), cached_head(cv_ref, h), preferred_element_type=F32)
          for h in heads]
    for h in heads:
        alpha = jnp.exp2(m_ref[h] - m_new[h])
        l_ref[h] = alpha * l_ref[h] + jnp.sum(p[h], axis=1, keepdims=True)
        acc_ref[h] = alpha * acc_ref[h] + pv[h]
        m_ref[h] = m_new[h]

    @pl.when(c == pl.num_programs(1) - 1)
    def _():
        for h in range(n_heads):
            q = q_ref[:, head_cols(h)]
            s = lax.dot_general(q, kn_ref[:, head_cols(h)], _NT, preferred_element_type=F32)
            s = s + cn_ref[h:h + 1, :] * (-LOG2E)
            rows = lax.broadcasted_iota(jnp.int32, s.shape, 0)
            cols = lax.broadcasted_iota(jnp.int32, s.shape, 1)
            s = jnp.where(cols <= rows, s, -jnp.inf)
            _, l, acc = _online_softmax_step(s, vn_ref[:, head_cols(h)],
                                             m_ref[h], l_ref[h], acc_ref[h])
            o_ref[:, head_cols(h)] = (
                acc / l * sga_ref[:, head_cols(h)].astype(F32)).astype(BF16)


def _attn_sample(q, k_new, v_new, cache_k2, cache_v2, cum_past, cum_new, sga, *,
                 batch, t_new, past, chunk):
    n, d_a = q.shape
    n_heads = d_a // HEAD_DIM
    n_chunks = past // chunk
    new_map = lambda b, c: (b, 0)
    cache_map = lambda b, c: (b * n_chunks + c, 0)
    return pl.pallas_call(
        functools.partial(_attn_sample_kernel, n_heads=n_heads),
        grid=(batch, n_chunks),
        in_specs=[
            pl.BlockSpec((t_new, d_a), new_map),
            pl.BlockSpec((t_new, d_a), new_map),
            pl.BlockSpec((t_new, d_a), new_map),
            pl.BlockSpec((chunk * n_heads, HEAD_DIM), cache_map),
            pl.BlockSpec((chunk * n_heads, HEAD_DIM), cache_map),
            pl.BlockSpec((None, None, n_heads, chunk), lambda b, c: (b, c, 0, 0)),
            pl.BlockSpec((None, n_heads, t_new), lambda b, c: (b, 0, 0)),
            pl.BlockSpec((t_new, d_a), new_map),
        ],
        out_specs=pl.BlockSpec((t_new, d_a), new_map),
        out_shape=jax.ShapeDtypeStruct((n, d_a), BF16),
        scratch_shapes=[pltpu.VMEM((n_heads, t_new, 1), F32),
                        pltpu.VMEM((n_heads, t_new, 1), F32),
                        pltpu.VMEM((n_heads, t_new, HEAD_DIM), F32)],
        compiler_params=_compiler_params(("parallel", "arbitrary")),
        name="attn_sample",
    )(q, k_new, v_new, cache_k2, cache_v2, cum_past, cum_new, sga)


def _out_proj_kernel(oa_ref, ob_ref, w_ref, x_ref, fg_ref, y_ref, *, final_norm):
    o = jnp.concatenate([oa_ref[...], ob_ref[...]], axis=1)
    y = x_ref[...] + jnp.dot(o, w_ref[...], preferred_element_type=F32)
    if final_norm:
        y = _rmsnorm_rows(y, fg_ref[...])
    y_ref[...] = y


def _out_proj(out_a, out_b, w_out, x2, final_g, *, tm, final_norm):
    n, d = x2.shape
    d_a = out_a.shape[1]
    d_b = out_b.shape[1]
    row = lambda i: (i, 0)

    def outer(oa_hbm, ob_hbm, w_ref, x_hbm, fg_ref, y_hbm):
        def step(oa_ref, ob_ref, x_ref, y_ref):
            _out_proj_kernel(oa_ref, ob_ref, w_ref, x_ref, fg_ref, y_ref, final_norm=final_norm)

        pltpu.emit_pipeline(
            step,
            grid=(n // tm,),
            in_specs=[pl.BlockSpec((tm, d_a), row), pl.BlockSpec((tm, d_b), row),
                      pl.BlockSpec((tm, d), row)],
            out_specs=[pl.BlockSpec((tm, d), row)],
        )(oa_hbm, ob_hbm, x_hbm, y_hbm)

    any_space = pl.BlockSpec(memory_space=pl.ANY)
    vmem = pl.BlockSpec(memory_space=pltpu.VMEM)
    return pl.pallas_call(
        outer,
        in_specs=[any_space, any_space, vmem, any_space, vmem],
        out_specs=any_space,
        out_shape=jax.ShapeDtypeStruct((n, d), F32),
        compiler_params=pltpu.CompilerParams(vmem_limit_bytes=VMEM_LIMIT_BYTES),
        name="out_proj",
    )(out_a, out_b, w_out, x2, final_g)


def _pad_rows(a, rows):
    return jnp.pad(a, ((0, rows - a.shape[0]), (0, 0)))


def _mixing_weights(w_s, b_s, seq):
    c = min(seq, GMLP_CHUNK)
    reps = GMLP_CHUNK // c
    w = w_s[:, :c, :c] * jnp.tril(jnp.ones((c, c), w_s.dtype))
    if reps > 1:
        w = jnp.einsum('rs,gab->grasb', jnp.eye(reps, dtype=w.dtype), w)
        w = w.reshape(w_s.shape[0], GMLP_CHUNK, GMLP_CHUNK)
    b = jnp.tile(b_s[:, :c], (1, reps))
    b_full = jnp.broadcast_to(b[:, :, None], (b.shape[0], GMLP_CHUNK, LANES))
    return w.astype(BF16), b_full.astype(F32)


def _split_w_in_kernel(wt_ref, qkv_ref, rest_ref, *, off_rest):
    wt = wt_ref[...]
    qkv_ref[...] = wt[:qkv_ref.shape[1], :].T.astype(BF16)
    rest_ref[...] = wt[off_rest:, :].T.astype(BF16)


def _split_w_in(w_in_t, off_f, off_rest, *, tc=W_SPLIT_COLS):
    d_in, d = w_in_t.shape
    return pl.pallas_call(
        functools.partial(_split_w_in_kernel, off_rest=off_rest),
        grid=(d // tc,),
        in_specs=[pl.BlockSpec((d_in, tc), lambda i: (0, i))],
        out_specs=[pl.BlockSpec((tc, off_f), lambda i: (i, 0)),
                   pl.BlockSpec((tc, d_in - off_rest), lambda i: (i, 0))],
        out_shape=[jax.ShapeDtypeStruct((d, off_f), BF16),
                   jax.ShapeDtypeStruct((d, d_in - off_rest), BF16)],
        compiler_params=_compiler_params(("parallel",)),
        name="split_w_in",
    )(w_in_t)


def _prepare_weights(params):
    norm_g, w_in, b_f, ln_g, ln_b, w_s, b_s, w_out = params
    n_heads = b_f.shape[0]
    off_f = 3 * n_heads * HEAD_DIM
    off_ga = off_f + n_heads
    w_in_t = w_in.T
    w_qkv, w_rest = _split_w_in(w_in_t, off_f, off_ga)
    return dict(
        norm_g=norm_g[None], b_f=b_f[:, None], ln_g=ln_g[None], ln_b=ln_b[None],
        w_qkv=w_qkv, w_rest=w_rest,
        w_ft=_pad_rows(w_in_t[off_f:off_ga], 16).astype(BF16),
        w_out=w_out.astype(BF16), w_s=w_s, b_s=b_s)


def _layer(x, w, caches, final_g, *, final_norm, tm):
    batch, seq, d = x.shape
    n = batch * seq
    n_heads = w['b_f'].shape[0]
    d_b = w['ln_g'].shape[1]

    x2 = x.reshape(n, d)
    w_mix, b_mix = _mixing_weights(w['w_s'], w['b_s'], seq)

    blk, kv_sub = ATTN_QUERY_BLOCK, ATTN_KEY_CHUNK
    tm_wide = min(n, WIDE_ROW_TILE)
    h, q, logft = _norm_q(x2, w['norm_g'], w['w_qkv'], w['w_ft'], w['b_f'], tm=tm_wide)
    k, kb, v, vb, sga = _proj_kv(h, w['w_qkv'], w['w_rest'], n_heads=n_heads, tm=tm,
                                 v_sub=kv_sub if caches is None else None)
    gm = _proj_gmlp(h, w['w_rest'], w['ln_g'], w['ln_b'], w_mix, b_mix,
                    tm=tm_wide, emit_vn=caches is not None)

    logf = logft.T.reshape(batch, seq, n_heads)
    if caches is None:
        out_b = gm[0]
        vn = None
        group = seq // LANES
        cum = _cumsum_rows(logft.reshape(n_heads * batch * group, LANES), group)
        cum_t = cum.reshape(n_heads, batch, group, LANES).transpose(0, 1, 3, 2)
        out_a = _attn_prompt(q, kb, vb, cum_t, sga, batch=batch, seq=seq, blk=blk, kv_sub=kv_sub,
                             heads_per_step=ATTN_HEADS_PER_STEP)
    else:
        out_b, vn = gm
        cache_k, cache_v, cache_logf = caches
        past = cache_k.shape[1]
        chunk = CACHE_CHUNK
        total = past + seq
        group = -(-total // LANES)
        lf_all = jnp.concatenate(
            [cache_logf.astype(F32).transpose(0, 2, 1), logf.transpose(0, 2, 1)], axis=2)
        lf_all = jnp.pad(lf_all, ((0, 0), (0, 0), (0, group * LANES - total)))
        cum = _cumsum_rows(lf_all.reshape(batch * n_heads * group, LANES), group)
        cum = cum.reshape(batch, n_heads, group * LANES)
        cum_past = cum[:, :, :past].reshape(batch, n_heads, past // chunk, chunk)
        cum_past = cum_past.transpose(0, 2, 1, 3)
        cum_new = cum[:, :, past:total]
        out_a = _attn_sample(q, kb, vb, cache_k.reshape(batch * past * n_heads, HEAD_DIM),
                             cache_v.reshape(batch * past * n_heads, HEAD_DIM),
                             cum_past, cum_new, sga,
                             batch=batch, t_new=seq, past=past, chunk=chunk)

    y2 = _out_proj(out_a, out_b, w['w_out'], x2, final_g[None], tm=tm, final_norm=final_norm)
    y = y2.reshape(batch, seq, d)
    k4 = k.reshape(batch, seq, n_heads, HEAD_DIM)
    v4 = v.reshape(batch, seq, n_heads, HEAD_DIM)
    vn3 = None if vn is None else vn.reshape(batch, seq, d_b)
    return y, k4, v4, logf, vn3


def kernel(x_prompt, x_sample, cache_k, cache_v, cache_logf, norm_g, w_in, b_f, ln_g, ln_b,
           w_s, b_s, w_out, final_g):
    depth = norm_g.shape[0]
    hp, hs = x_prompt, x_sample
    kp, vp, fp, ksm, vsm, fsm, gsm = [], [], [], [], [], [], []
    n_sample = x_sample.shape[0] * x_sample.shape[1]
    for l in range(depth):
        w = _prepare_weights(
            (norm_g[l], w_in[l], b_f[l], ln_g[l], ln_b[l], w_s[l], b_s[l], w_out[l]))
        last = l == depth - 1
        hp, k1, v1, f1, _ = _layer(hp, w, None, final_g, final_norm=last, tm=ROW_TILE)
        hs, k2, v2, f2, g2 = _layer(hs, w, (cache_k[l], cache_v[l], cache_logf[l]), final_g,
                                    final_norm=last, tm=n_sample)
        kp.append(k1); vp.append(v1); fp.append(f1)
        ksm.append(k2); vsm.append(v2); fsm.append(f2); gsm.append(g2)
    return (hp, hs, jnp.stack(kp), jnp.stack(vp), jnp.stack(fp),
            jnp.stack(ksm), jnp.stack(vsm), jnp.stack(fsm), jnp.stack(gsm))
```
